```python
import math
import jax
import jax.numpy as jnp
from jax import lax
import numpy as np

D_MODEL = 1024
BATCH = 4
SEQ = 8192
DEPTH = 2

MIX_WIDTH = D_MODEL
LRU_WIDTH = D_MODEL // 4
LRU_BLOCK_W = 64
LRU_BLOCKS = LRU_WIDTH // LRU_BLOCK_W
CONV_W = 4
LRU_C = 8.0
S5_WIDTH = D_MODEL // 4
S5_GROUP_W = 16
S5_GROUPS = S5_WIDTH // S5_GROUP_W
S5_STATE = 64
HEAD_DIM = 64
ATTN_WIDTH = MIX_WIDTH - LRU_WIDTH - S5_WIDTH
N_ATTN_HEADS = ATTN_WIDTH // HEAD_DIM
Q_BLOCK = 128
O_LRU_X = 0
O_LRU_G = O_LRU_X + LRU_WIDTH
O_S5 = O_LRU_G + LRU_WIDTH
O_Q = O_S5 + S5_WIDTH
O_K = O_Q + ATTN_WIDTH
O_V = O_K + ATTN_WIDTH
O_F = O_V + ATTN_WIDTH
IN_COLS = O_F + N_ATTN_HEADS
N_EXPERTS = 32
TOP_K = 4
D_FF = D_MODEL
SWIGLU_LIMIT = 7.0
SWIGLU_ALPHA = 1.702
MOE_BLOCK = 128
N_MOD = 6
RMS_EPS = 1e-6

kernel_name = 'hybrid_lru_s5_fox_moe_block'


def rms_norm(x, g):
    xf = x.astype(jnp.float32)
    y = xf * lax.rsqrt(jnp.mean(xf * xf, axis=-1, keepdims=True) + RMS_EPS)
    return (y * g.astype(jnp.float32)).astype(x.dtype)


def causal_conv(x, w, b):
    T = x.shape[1]
    xp = jnp.pad(x, ((0, 0), (CONV_W - 1, 0), (0, 0)))
    y = b
    for j in range(CONV_W):
        y = y + xp[:, j:j + T] * w[j]
    return y


def _linear_combine(e1, e2):
    a1, b1 = e1
    a2, b2 = e2
    return a1 * a2, a2 * b1 + b2


def _complex_combine(e1, e2):
    ar1, ai1, br1, bi1 = e1
    ar2, ai2, br2, bi2 = e2
    return (ar2 * ar1 - ai2 * ai1,
            ar2 * ai1 + ai2 * ar1,
            ar2 * br1 - ai2 * bi1 + br2,
            ar2 * bi1 + ai2 * br1 + bi2)


def rg_lru(x, w_a, b_a, w_x, b_x, lam):
    Bsz, T, C = x.shape
    xf = x.astype(jnp.float32)
    xh = xf.reshape(Bsz, T, LRU_BLOCKS, LRU_BLOCK_W)
    r = jax.nn.sigmoid(jnp.einsum('bthi,hij->bthj', xh, w_a.astype(jnp.float32)).reshape(Bsz, T, C) + b_a)
    i = jax.nn.sigmoid(jnp.einsum('bthi,hij->bthj', xh, w_x.astype(jnp.float32)).reshape(Bsz, T, C) + b_x)
    log_a = -LRU_C * r * jax.nn.softplus(-lam.astype(jnp.float32))
    a = jnp.exp(log_a)
    u = xf * i * jnp.sqrt(-jnp.expm1(2.0 * log_a))
    _, h = lax.associative_scan(_linear_combine, (a, u), axis=1)
    return h.astype(x.dtype)


def s5_mixer(u, a_re, a_im, b_re, b_im, c_re, c_im, d, log_dt, glu_w, glu_b):
    Bsz, T, C = u.shape
    f32 = jnp.float32
    uf = u.astype(f32).reshape(Bsz, T, S5_GROUPS, S5_GROUP_W)
    a_re = a_re.astype(f32)
    a_im = a_im.astype(f32)
    dt = jnp.exp(log_dt.astype(f32))[:, None]
    mag = jnp.exp(a_re * dt)
    abar_re = mag * jnp.cos(a_im * dt)
    abar_im = mag * jnp.sin(a_im * dt)
    den = a_re * a_re + a_im * a_im
    num_re = abar_re - 1.0
    k_re = (num_re * a_re + abar_im * a_im) / den
    k_im = (abar_im * a_re - num_re * a_im) / den
    b_re = b_re.astype(f32)
    b_im = b_im.astype(f32)
    bbar_re = k_re[..., None] * b_re - k_im[..., None] * b_im
    bbar_im = k_re[..., None] * b_im + k_im[..., None] * b_re
    bu_re = jnp.einsum('btgc,gpc->btgp', uf, bbar_re)
    bu_im = jnp.einsum('btgc,gpc->btgp', uf, bbar_im)
    ar = jnp.broadcast_to(abar_re, bu_re.shape)
    ai = jnp.broadcast_to(abar_im, bu_im.shape)
    _, _, xr, xi = lax.associative_scan(_complex_combine, (ar, ai, bu_re, bu_im), axis=1)
    y = (jnp.einsum('btgp,gcp->btgc', xr, c_re.astype(f32))
         - jnp.einsum('btgp,gcp->btgc', xi, c_im.astype(f32)))
    y = y.reshape(Bsz, T, C) + d.astype(f32) * uf.reshape(Bsz, T, C)
    y = jax.nn.gelu(y)
    y = y * jax.nn.sigmoid(y @ glu_w.astype(f32) + glu_b.astype(f32))
    return y.astype(u.dtype)


def forgetting_attention(q, k, v, log_f):
    Bsz, T, H, Dh = q.shape
    nq = T // Q_BLOCK
    F = jnp.cumsum(log_f, axis=1)
    FT = jnp.transpose(F, (0, 2, 1))
    q_blocks = jnp.swapaxes(q.reshape(Bsz, nq, Q_BLOCK, H, Dh), 0, 1)
    F_blocks = jnp.moveaxis(FT.reshape(Bsz, H, nq, Q_BLOCK), 2, 0)
    k_pos = jnp.arange(T)
    scale = HEAD_DIM ** -0.5

    def block(args):
        qi, Fi, bi = args
        s = jnp.einsum('bqhd,bkhd->bhqk', qi, k, preferred_element_type=jnp.float32) * scale
        s = s + Fi[..., :, None] - FT[:, :, None, :]
        q_pos = bi * Q_BLOCK + jnp.arange(Q_BLOCK)
        s = jnp.where(k_pos[None, :] <= q_pos[:, None], s, -jnp.inf)
        p = jax.nn.softmax(s, axis=-1)
        return jnp.einsum('bhqk,bkhd->bqhd', p.astype(v.dtype), v)

    out = lax.map(block, (q_blocks, F_blocks, jnp.arange(nq)))
    return jnp.swapaxes(out, 0, 1).reshape(Bsz, T, H * Dh)


def token_mix(h, w_in, conv_w, conv_b, lru_wa, lru_ba, lru_wx, lru_bx, lru_lambda,
              s5_a_re, s5_a_im, s5_b_re, s5_b_im, s5_c_re, s5_c_im, s5_d, s5_log_dt,
              s5_glu_w, s5_glu_b, fox_fb, gn_lru, gn_s5, gn_attn, w_out):
    Bsz, T, _ = h.shape
    p = h @ w_in
    xr = causal_conv(p[..., O_LRU_X:O_LRU_G], conv_w, conv_b)
    y_lru = rg_lru(xr, lru_wa, lru_ba, lru_wx, lru_bx, lru_lambda) * jax.nn.gelu(p[..., O_LRU_G:O_S5])
    y_s5 = s5_mixer(p[..., O_S5:O_Q], s5_a_re, s5_a_im, s5_b_re, s5_b_im, s5_c_re, s5_c_im,
                    s5_d, s5_log_dt, s5_glu_w, s5_glu_b)
    q = p[..., O_Q:O_K].reshape(Bsz, T, N_ATTN_HEADS, HEAD_DIM)
    k = p[..., O_K:O_V].reshape(Bsz, T, N_ATTN_HEADS, HEAD_DIM)
    v = p[..., O_V:O_F].reshape(Bsz, T, N_ATTN_HEADS, HEAD_DIM)
    log_f = jax.nn.log_sigmoid((p[..., O_F:IN_COLS] + fox_fb).astype(jnp.float32))
    y_att = forgetting_attention(q, k, v, log_f)
    y = jnp.concatenate([rms_norm(y_lru, gn_lru), rms_norm(y_s5, gn_s5), rms_norm(y_att, gn_attn)], axis=-1)
    return y @ w_out


def moe_ffn(h, router_w, router_b, w_gu, b_gu, w_dn, b_dn):
    N, D = h.shape
    NK = N * TOP_K
    n_blocks = -(-NK // MOE_BLOCK) + N_EXPERTS
    logits = (h @ router_w + router_b).astype(jnp.float32)
    top_vals, top_idx = lax.top_k(logits, TOP_K)
    gates = jax.nn.softmax(top_vals, axis=-1).astype(h.dtype)
    flat_e = top_idx.reshape(-1)
    flat_tok = jnp.arange(NK, dtype=jnp.int32) // TOP_K
    order = jnp.argsort(flat_e)
    sorted_e = flat_e[order]
    counts = jnp.bincount(flat_e, length=N_EXPERTS)
    starts = jnp.cumsum(counts) - counts
    padded = (counts + MOE_BLOCK - 1) // MOE_BLOCK * MOE_BLOCK
    pad_ends = jnp.cumsum(padded)
    pad_starts = pad_ends - padded
    dest = pad_starts[sorted_e] + jnp.arange(NK) - starts[sorted_e]
    buf_tok = jnp.zeros((n_blocks * MOE_BLOCK,), jnp.int32).at[dest].set(flat_tok[order])
    block_e = jnp.minimum(jnp.searchsorted(pad_ends, jnp.arange(n_blocks) * MOE_BLOCK, side='right'),
                          N_EXPERTS - 1)
    xb = h[buf_tok].reshape(n_blocks, MOE_BLOCK, D)

    def expert_block(args):
        xe, e = args
        gu = xe @ w_gu[e] + b_gu[e]
        g = jnp.minimum(gu[:, :D_FF], SWIGLU_LIMIT)
        up = jnp.clip(gu[:, D_FF:], -SWIGLU_LIMIT, SWIGLU_LIMIT)
        return ((up + 1.0) * (g * jax.nn.sigmoid(SWIGLU_ALPHA * g))) @ w_dn[e] + b_dn[e]

    yb = lax.map(expert_block, (xb, block_e)).reshape(n_blocks * MOE_BLOCK, D)
    y_sorted = yb[dest]
    y_assign = jnp.zeros_like(y_sorted).at[order].set(y_sorted).reshape(N, TOP_K, D)
    return jnp.einsum('nk,nkd->nd', gates, y_assign)


def setup_inputs(seed: int = 0) -> dict:
    key = jax.random.key(seed)
    keys = list(jax.random.split(key, 40))

    def nrm(shape, std):
        return std * jax.random.normal(keys.pop(), shape, jnp.float32)

    def uni(shape, lo, hi):
        return jax.random.uniform(keys.pop(), shape, jnp.float32, lo, hi)

    L = DEPTH
    n = jnp.arange(S5_STATE, dtype=jnp.float32)
    a0 = uni((L, LRU_WIDTH), 0.9, 0.999)
    s = a0 ** (1.0 / LRU_C)
    return {
        'x': nrm((BATCH, SEQ, D_MODEL), 1.0),
        'c': nrm((BATCH, D_MODEL), 1.0),
        'ada_w': nrm((L, D_MODEL, N_MOD * D_MODEL), 0.5 * D_MODEL ** -0.5),
        'ada_b': nrm((L, N_MOD * D_MODEL), 0.02),
        'pre_mix_g': 1.0 + nrm((L, D_MODEL), 0.1),
        'w_in': nrm((L, D_MODEL, IN_COLS), D_MODEL ** -0.5),
        'conv_w': nrm((L, CONV_W, LRU_WIDTH), CONV_W ** -0.5),
        'conv_b': nrm((L, LRU_WIDTH), 0.01),
        'lru_wa': nrm((L, LRU_BLOCKS, LRU_BLOCK_W, LRU_BLOCK_W), LRU_BLOCK_W ** -0.5),
        'lru_ba': nrm((L, LRU_WIDTH), 0.01),
        'lru_wx': nrm((L, LRU_BLOCKS, LRU_BLOCK_W, LRU_BLOCK_W), LRU_BLOCK_W ** -0.5),
        'lru_bx': nrm((L, LRU_WIDTH), 0.01),
        'lru_lambda': jnp.log(s) - jnp.log1p(-s),
        's5_a_re': -0.5 + nrm((L, S5_GROUPS, S5_STATE), 0.01),
        's5_a_im': jnp.pi * n + nrm((L, S5_GROUPS, S5_STATE), 0.01),
        's5_b_re': nrm((L, S5_GROUPS, S5_STATE, S5_GROUP_W), (2 * S5_GROUP_W) ** -0.5),
        's5_b_im': nrm((L, S5_GROUPS, S5_STATE, S5_GROUP_W), (2 * S5_GROUP_W) ** -0.5),
        's5_c_re': nrm((L, S5_GROUPS, S5_GROUP_W, S5_STATE), 0.5),
        's5_c_im': nrm((L, S5_GROUPS, S5_GROUP_W, S5_STATE), 0.5),
        's5_d': nrm((L, S5_WIDTH), 0.5),
        's5_log_dt': uni((L, S5_GROUPS), math.log(1e-3), math.log(1e-1)),
        's5_glu_w': nrm((L, S5_WIDTH, S5_WIDTH), S5_WIDTH ** -0.5),
        's5_glu_b': nrm((L, S5_WIDTH), 0.01),
        'fox_fb': 3.0 + nrm((L, N_ATTN_HEADS), 0.5),
        'gn_lru': 1.0 + nrm((L, LRU_WIDTH), 0.1),
        'gn_s5': 1.0 + nrm((L, S5_WIDTH), 0.1),
        'gn_attn': 1.0 + nrm((L, ATTN_WIDTH), 0.1),
        'w_out': nrm((L, MIX_WIDTH, D_MODEL), MIX_WIDTH ** -0.5),
        'post_mix_g': 1.0 + nrm((L, D_MODEL), 0.1),
        'pre_ffn_g': 1.0 + nrm((L, D_MODEL), 0.1),
        'router_w': nrm((L, D_MODEL, N_EXPERTS), D_MODEL ** -0.5),
        'router_b': nrm((L, N_EXPERTS), 0.01),
        'w_gu': nrm((L, N_EXPERTS, D_MODEL, 2 * D_FF), D_MODEL ** -0.5),
        'b_gu': nrm((L, N_EXPERTS, 2 * D_FF), 0.01),
        'w_dn': nrm((L, N_EXPERTS, D_FF, D_MODEL), D_FF ** -0.5),
        'b_dn': nrm((L, N_EXPERTS, D_MODEL), 0.01),
        'post_ffn_g': 1.0 + nrm((L, D_MODEL), 0.1),
    }


def reference(x, c, ada_w, ada_b, pre_mix_g, w_in, conv_w, conv_b, lru_wa, lru_ba, lru_wx, lru_bx,
              lru_lambda, s5_a_re, s5_a_im, s5_b_re, s5_b_im, s5_c_re, s5_c_im, s5_d, s5_log_dt,
              s5_glu_w, s5_glu_b, fox_fb, gn_lru, gn_s5, gn_attn, w_out, post_mix_g, pre_ffn_g,
              router_w, router_b, w_gu, b_gu, w_dn, b_dn, post_ffn_g):
    Bsz, T, D = x.shape
    for l in range(DEPTH):
        mod = (jax.nn.silu(c) @ ada_w[l] + ada_b[l]).reshape(Bsz, N_MOD, 1, D)
        shift_m, scale_m, gate_m = mod[:, 0], mod[:, 1], mod[:, 2]
        shift_f, scale_f, gate_f = mod[:, 3], mod[:, 4], mod[:, 5]
        h = rms_norm(x, pre_mix_g[l]) * (1.0 + scale_m) + shift_m
        y = token_mix(h, w_in[l], conv_w[l], conv_b[l], lru_wa[l], lru_ba[l], lru_wx[l], lru_bx[l],
                      lru_lambda[l], s5_a_re[l], s5_a_im[l], s5_b_re[l], s5_b_im[l], s5_c_re[l],
                      s5_c_im[l], s5_d[l], s5_log_dt[l], s5_glu_w[l], s5_glu_b[l], fox_fb[l],
                      gn_lru[l], gn_s5[l], gn_attn[l], w_out[l])
        x = x + gate_m * rms_norm(y, post_mix_g[l])
        h = rms_norm(x, pre_ffn_g[l]) * (1.0 + scale_f) + shift_f
        y = moe_ffn(h.reshape(Bsz * T, D), router_w[l], router_b[l], w_gu[l], b_gu[l],
                    w_dn[l], b_dn[l]).reshape(Bsz, T, D)
        x = x + gate_f * rms_norm(y, post_ffn_g[l])
    return x
```

```python
import functools

import jax
import jax.numpy as jnp
from jax import lax
from jax.experimental import pallas as pl
from jax.experimental.pallas import tpu as pltpu

F32 = jnp.float32
BF16 = jnp.bfloat16
I32 = jnp.int32

D = 1024
LRU_W = 256
LRU_BLOCK_W = 64
CONV_W = 4
LRU_C = 8.0
S5_W = 256
S5_GROUP_W = 16
S5_GROUPS = 16
S5_STATE = 64
S5_STATES = S5_GROUPS * S5_STATE
HEAD_DIM = 64
ATTN_W = 512
N_HEADS = 8
O_LRU_G = 256
O_S5 = 512
O_Q = 768
O_K = O_Q + ATTN_W
O_V = O_K + ATTN_W
O_F = O_V + ATTN_W
N_EXPERTS = 32
TOP_K = 4
D_FF = 1024
SWIGLU_LIMIT = 7.0
SWIGLU_ALPHA = 1.702
N_MOD = 6
RMS_EPS = 1e-6

LANES = 128
SUBLANES = 8
VMEM_LIMIT = 56 * 1024 * 1024
ROW_CHUNKS = D // LANES

C_Q = 768
C_K = C_Q + N_HEADS * LANES
C_V = C_K + N_HEADS * LANES
C_F = C_V + N_HEADS * LANES
IN_COLS_PAD = C_F + LANES
L_A0, L_A1, L_A2, L_B0, L_B1, L_B2 = 64, 65, 66, 67, 68, 69

TM_IN = 256
TC_LRU = 256
TC_S5 = 128
TQ = 512
TM_OUT = 512
TN_ROWS = 1024
TN_COMB = 256
EXP_ROWS = 256
NEG_BIG = -1e30


def _rms(x, g):
    return x * lax.rsqrt(jnp.mean(x * x, axis=-1, keepdims=True) + RMS_EPS) * g


def _shift_rows(v, d, row, fill):
    return jnp.where(row >= d, pltpu.roll(v, d, 0), fill)


def _params(vmem=None):
    return pltpu.CompilerParams(vmem_limit_bytes=vmem) if vmem else None


def _mod_kernel(c_ref, w_ref, b_ref, o_ref):
    c = c_ref[...]
    s = c * jax.nn.sigmoid(c)
    o_ref[0] = jnp.dot(s.astype(BF16), w_ref[0].astype(BF16), preferred_element_type=F32) + b_ref[0]


def _modulation(c, ada_w, ada_b):
    L, _, W = ada_w.shape
    B = c.shape[0]
    return pl.pallas_call(
        _mod_kernel,
        grid=(L, W // D),
        in_specs=[pl.BlockSpec((B, D), lambda l, j: (0, 0)),
                  pl.BlockSpec((1, D, D), lambda l, j: (l, 0, j)),
                  pl.BlockSpec((1, 1, D), lambda l, j: (l, 0, j))],
        out_specs=pl.BlockSpec((1, B, D), lambda l, j: (l, 0, j)),
        out_shape=jax.ShapeDtypeStruct((L, B, W), F32),
        name="adaln_mod",
    )(c, ada_w, ada_b.reshape(L, 1, W))


def _in_proj_kernel(x_ref, mod_ref, g_ref, w_ref, fb_ref,
                    lx_ref, lg_ref, su_ref, q_ref, k_ref, v_ref, fcarry_ref, *, tm):
    @pl.when(pl.program_id(1) == 0)
    def _():
        fcarry_ref[...] = jnp.zeros_like(fcarry_ref)

    mod = mod_ref[0]
    h = _rms(x_ref[0], g_ref[...]) * (1.0 + mod[1:2]) + mod[0:1]
    p = jnp.dot(h.astype(BF16), w_ref[...], preferred_element_type=F32)
    lx_ref[0] = p[:, 0:O_LRU_G]
    lg_ref[0] = p[:, O_LRU_G:O_S5]
    su_ref[0] = p[:, O_S5:O_Q]

    z = p[:, C_F:C_F + LANES] + fb_ref[...]
    logf = jnp.minimum(z, 0.0) - jnp.log1p(jnp.exp(-jnp.abs(z)))
    row = lax.broadcasted_iota(I32, (tm, LANES), 0)
    d = 1
    while d < tm:
        logf = logf + _shift_rows(logf, d, row, 0.0)
        d *= 2
    fsum = logf + fcarry_ref[...]
    fcarry_ref[...] = fsum[tm - 1:tm, :]

    fexp = jnp.concatenate([jnp.broadcast_to(fsum[:, h:h + 1], (tm, LANES)) for h in range(N_HEADS)], axis=1)
    hi = fexp.astype(BF16).astype(F32)
    r1 = fexp - hi
    mid = r1.astype(BF16).astype(F32)
    lo = r1 - mid
    lane = lax.broadcasted_iota(I32, (tm, N_HEADS * LANES), 1) & (LANES - 1)
    pq = p[:, C_Q:C_K] * (HEAD_DIM ** -0.5)
    pk = p[:, C_K:C_V]
    pv = p[:, C_V:C_F]
    ones = (lane >= L_B0) & (lane <= L_B2)
    qa = jnp.where(lane == L_A0, hi, jnp.where(lane == L_A1, mid, jnp.where(lane == L_A2, lo,
                                                                            jnp.where(ones, 1.0, pq))))
    ones = (lane >= L_A0) & (lane <= L_A2)
    ka = jnp.where(lane == L_B0, -hi, jnp.where(lane == L_B1, -mid, jnp.where(lane == L_B2, -lo,
                                                                              jnp.where(ones, 1.0, pk))))
    va = jnp.where(lane == HEAD_DIM, 1.0, pv)
    for h in range(N_HEADS):
        sl = slice(h * LANES, (h + 1) * LANES)
        q_ref[0, h] = qa[:, sl].astype(BF16)
        k_ref[0, h] = ka[:, sl].astype(BF16)
        v_ref[0, h] = va[:, sl].astype(BF16)


def _in_proj(x, mod, g, w_aug, fb):
    B, T, _ = x.shape
    tm = min(TM_IN, T)
    grp = jax.ShapeDtypeStruct((B, T, 256), F32)
    head = jax.ShapeDtypeStruct((B, N_HEADS, T, LANES), BF16)
    grp_spec = pl.BlockSpec((1, tm, 256), lambda b, t: (b, t, 0))
    head_spec = pl.BlockSpec((1, N_HEADS, tm, LANES), lambda b, t: (b, 0, t, 0))
    return pl.pallas_call(
        functools.partial(_in_proj_kernel, tm=tm),
        grid=(B, T // tm),
        in_specs=[pl.BlockSpec((1, tm, D), lambda b, t: (b, t, 0)),
                  pl.BlockSpec((1, N_MOD, D), lambda b, t: (b, 0, 0)),
                  pl.BlockSpec((1, D), lambda b, t: (0, 0)),
                  pl.BlockSpec((D, IN_COLS_PAD), lambda b, t: (0, 0)),
                  pl.BlockSpec((1, LANES), lambda b, t: (0, 0))],
        out_specs=[grp_spec, grp_spec, grp_spec, head_spec, head_spec, head_spec],
        out_shape=[grp, grp, grp, head, head, head],
        scratch_shapes=[pltpu.VMEM((1, LANES), F32)],
        compiler_params=_params(VMEM_LIMIT),
        name="in_proj",
    )(x, mod, g, w_aug, fb)


def _lru_kernel(lx_ref, lg_ref, cw_ref, cb_ref, wa_ref, ba_ref, wx_ref, bx_ref, lam_ref, gn_ref,
                y_ref, tail_ref, hcarry_ref, *, tc):
    @pl.when(pl.program_id(1) == 0)
    def _():
        tail_ref[...] = jnp.zeros_like(tail_ref)
        hcarry_ref[...] = jnp.zeros_like(hcarry_ref)

    x = lx_ref[0]
    xcat = jnp.concatenate([tail_ref[...], x], axis=0)
    cw = cw_ref[...]
    xr = cb_ref[...]
    for j in range(CONV_W):
        d = CONV_W - 1 - j
        xs = x if d == 0 else pltpu.roll(xcat, d, 0)[SUBLANES:SUBLANES + tc]
        xr = xr + xs * cw[j:j + 1]
    tail_ref[...] = x[tc - SUBLANES:tc]

    xb = xr.astype(BF16)
    r = jax.nn.sigmoid(jnp.dot(xb, wa_ref[...], preferred_element_type=F32) + ba_ref[...])
    i = jax.nn.sigmoid(jnp.dot(xb, wx_ref[...], preferred_element_type=F32) + bx_ref[...])
    nl = -lam_ref[...]
    softplus = jnp.maximum(nl, 0.0) + jnp.log1p(jnp.exp(-jnp.abs(nl)))
    log_a = -LRU_C * r * softplus
    a = jnp.exp(log_a)
    th = jnp.tanh(log_a)
    u = xr * i * jnp.sqrt(-2.0 * th / (1.0 - th))

    row = lax.broadcasted_iota(I32, (tc, LRU_W), 0)
    d = 1
    while d < tc:
        u = u + a * _shift_rows(u, d, row, 0.0)
        a = a * _shift_rows(a, d, row, 1.0)
        d *= 2
    h = u + a * hcarry_ref[...]
    hcarry_ref[...] = h[tc - 1:tc]

    y = h * jax.nn.gelu(lg_ref[0])
    y_ref[0] = _rms(y, gn_ref[...]).astype(BF16)


def _lru_branch(lx, lg, conv_w, conv_b, wa, ba, wx, bx, lam, gn):
    B, T, _ = lx.shape
    tc = min(TC_LRU, T)
    tile = pl.BlockSpec((1, tc, LRU_W), lambda b, t: (b, t, 0))
    vec = pl.BlockSpec((1, LRU_W), lambda b, t: (0, 0))
    mat = pl.BlockSpec((LRU_W, LRU_W), lambda b, t: (0, 0))
    return pl.pallas_call(
        functools.partial(_lru_kernel, tc=tc),
        grid=(B, T // tc),
        in_specs=[tile, tile, pl.BlockSpec((CONV_W, LRU_W), lambda b, t: (0, 0)), vec, mat, vec, mat, vec, vec, vec],
        out_specs=tile,
        out_shape=jax.ShapeDtypeStruct((B, T, LRU_W), BF16),
        scratch_shapes=[pltpu.VMEM((SUBLANES, LRU_W), F32), pltpu.VMEM((1, LRU_W), F32)],
        name="rg_lru",
    )(lx, lg, conv_w, conv_b, wa, ba, wx, bx, lam, gn)


def _s5_kernel(u_ref, bcat_ref, apr_ref, api_ref, ptr_ref, pti_ref, ccat_ref, d_ref, gw_ref, gb_ref, gn_ref,
               y_ref, cr_ref, ci_ref, *, tc):
    @pl.when(pl.program_id(1) == 0)
    def _():
        cr_ref[...] = jnp.zeros_like(cr_ref)
        ci_ref[...] = jnp.zeros_like(ci_ref)

    u = u_ref[0]
    bu = jnp.dot(u.astype(BF16), bcat_ref[...], preferred_element_type=F32)
    xr = bu[:, :S5_STATES]
    xi = bu[:, S5_STATES:]
    row = lax.broadcasted_iota(I32, (tc, S5_STATES), 0)
    k, d = 0, 1
    while d < tc:
        ar = apr_ref[k:k + 1, :]
        ai = api_ref[k:k + 1, :]
        sr = _shift_rows(xr, d, row, 0.0)
        si = _shift_rows(xi, d, row, 0.0)
        xr, xi = xr + (ar * sr - ai * si), xi + (ar * si + ai * sr)
        k, d = k + 1, d * 2
    pr = ptr_ref[...]
    pi = pti_ref[...]
    cr = cr_ref[...]
    ci = ci_ref[...]
    xr, xi = xr + (pr * cr - pi * ci), xi + (pr * ci + pi * cr)
    cr_ref[...] = xr[tc - 1:tc]
    ci_ref[...] = xi[tc - 1:tc]

    xcat = jnp.concatenate([xr, xi], axis=1).astype(BF16)
    y = jnp.dot(xcat, ccat_ref[...], preferred_element_type=F32) + d_ref[...] * u
    y = jax.nn.gelu(y)
    y = y * jax.nn.sigmoid(jnp.dot(y.astype(BF16), gw_ref[...], preferred_element_type=F32) + gb_ref[...])
    y_ref[0] = _rms(y, gn_ref[...]).astype(BF16)


def _s5_branch(u, bcat, apow_re, apow_im, ptab_re, ptab_im, ccat, d, glu_w, glu_b, gn, tc):
    B, T, _ = u.shape
    tile = pl.BlockSpec((1, tc, S5_W), lambda b, t: (b, t, 0))
    vec = pl.BlockSpec((1, S5_W), lambda b, t: (0, 0))

    def full(a):
        return pl.BlockSpec(a.shape, lambda b, t: (0, 0))

    return pl.pallas_call(
        functools.partial(_s5_kernel, tc=tc),
        grid=(B, T // tc),
        in_specs=[tile, full(bcat), full(apow_re), full(apow_im), full(ptab_re), full(ptab_im), full(ccat),
                  vec, full(glu_w), vec, vec],
        out_specs=tile,
        out_shape=jax.ShapeDtypeStruct((B, T, S5_W), BF16),
        scratch_shapes=[pltpu.VMEM((1, S5_STATES), F32), pltpu.VMEM((1, S5_STATES), F32)],
        compiler_params=_params(VMEM_LIMIT),
        name="s5",
    )(u, bcat, apow_re, apow_im, ptab_re, ptab_im, ccat, d, glu_w, glu_b, gn)


def _s5_tables(a_re, a_im, b_re, b_im, c_re, c_im, log_dt, tc):
    dt = jnp.exp(log_dt)[:, None]
    mag = jnp.exp(a_re * dt)
    abar_re = mag * jnp.cos(a_im * dt)
    abar_im = mag * jnp.sin(a_im * dt)
    den = a_re * a_re + a_im * a_im
    num_re = abar_re - 1.0
    k_re = (num_re * a_re + abar_im * a_im) / den
    k_im = (abar_im * a_re - num_re * a_im) / den
    bbar_re = k_re[..., None] * b_re - k_im[..., None] * b_im
    bbar_im = k_re[..., None] * b_im + k_im[..., None] * b_re
    eye = jnp.eye(S5_GROUPS, dtype=F32)
    bd_re = jnp.einsum('gpc,gh->gchp', bbar_re, eye).reshape(S5_W, S5_STATES)
    bd_im = jnp.einsum('gpc,gh->gchp', bbar_im, eye).reshape(S5_W, S5_STATES)
    bcat = jnp.concatenate([bd_re, bd_im], axis=1).astype(BF16)
    cd_re = jnp.einsum('gcp,gh->gphc', c_re, eye).reshape(S5_STATES, S5_W)
    cd_im = jnp.einsum('gcp,gh->gphc', c_im, eye).reshape(S5_STATES, S5_W)
    ccat = jnp.concatenate([cd_re, -cd_im], axis=0).astype(BF16)
    ar = abar_re.reshape(1, S5_STATES)
    ai = abar_im.reshape(1, S5_STATES)
    pows_r, pows_i = [ar], [ai]
    tab_r, tab_i = ar, ai
    n = 1
    while n < tc:
        sr, si = pows_r[-1], pows_i[-1]
        tab_r, tab_i = (jnp.concatenate([tab_r, tab_r * sr - tab_i * si], axis=0),
                        jnp.concatenate([tab_i, tab_r * si + tab_i * sr], axis=0))
        pows_r.append(sr * sr - si * si)
        pows_i.append(2.0 * sr * si)
        n *= 2
    levels = len(pows_r) - 1
    pad = (-levels) % SUBLANES
    apow_re = jnp.concatenate(pows_r[:levels] + [jnp.zeros((pad, S5_STATES), F32)], axis=0)
    apow_im = jnp.concatenate(pows_i[:levels] + [jnp.zeros((pad, S5_STATES), F32)], axis=0)
    return bcat, apow_re, apow_im, tab_r, tab_i, ccat


def _attn_kernel(q_ref, k_ref, v_ref, o_ref, *, tq):
    qi = pl.program_id(2)
    row = lax.broadcasted_iota(I32, (tq, tq), 0)
    col = lax.broadcasted_iota(I32, (tq, tq), 1)
    outs = []
    for hh in range(2):
        q = q_ref[0, hh]

        def scores(kj):
            k = k_ref[0, hh, pl.ds(pl.multiple_of(kj * tq, tq), tq), :]
            return lax.dot_general(q, k, (((1,), (1,)), ((), ())), preferred_element_type=F32)

        def update(kj, s, m, acc):
            v = v_ref[0, hh, pl.ds(pl.multiple_of(kj * tq, tq), tq), :]
            m_new = jnp.maximum(m, jnp.max(s, axis=1, keepdims=True))
            p = jnp.exp(s - m_new)
            acc = jnp.exp(m - m_new) * acc + jnp.dot(p.astype(BF16), v, preferred_element_type=F32)
            return m_new, acc

        def body(kj, carry):
            return update(kj, scores(kj), *carry)

        m0 = jnp.full((tq, 1), NEG_BIG, F32)
        acc0 = jnp.zeros((tq, LANES), F32)
        m, acc = lax.fori_loop(0, qi, body, (m0, acc0))
        s = jnp.where(col <= row, scores(qi), NEG_BIG)
        m, acc = update(qi, s, m, acc)
        outs.append(acc[:, :HEAD_DIM] / acc[:, HEAD_DIM:HEAD_DIM + 1])
    o_ref[0] = jnp.concatenate(outs, axis=1)


def _attention(qa, ka, va):
    B, H, T, _ = qa.shape
    tq = min(TQ, T)
    kv_spec = pl.BlockSpec((1, 2, T, LANES), lambda b, hp, qi: (b, hp, 0, 0))
    return pl.pallas_call(
        functools.partial(_attn_kernel, tq=tq),
        grid=(B, H // 2, T // tq),
        in_specs=[pl.BlockSpec((1, 2, tq, LANES), lambda b, hp, qi: (b, hp, qi, 0)), kv_spec, kv_spec],
        out_specs=pl.BlockSpec((1, tq, 2 * HEAD_DIM), lambda b, hp, qi: (b, qi, hp)),
        out_shape=jax.ShapeDtypeStruct((B, T, ATTN_W), F32),
        compiler_params=_params(VMEM_LIMIT),
        name="fox_attention",
    )(qa, ka, va)


def _out_proj_kernel(yl_ref, ys_ref, ya_ref, x_ref, mod_ref, gna_ref, wout_ref, pmg_ref, pfg_ref,
                     rwh_ref, rwl_ref, rb_ref,
                     x1_ref, h2_ref, idx_ref, gate_ref, rank_ref, cnt_ref, carry_ref, *, tm):
    @pl.when((pl.program_id(0) == 0) & (pl.program_id(1) == 0))
    def _():
        carry_ref[...] = jnp.zeros_like(carry_ref)

    mod = mod_ref[0]
    ya = _rms(ya_ref[0], gna_ref[...])
    ycat = jnp.concatenate([yl_ref[0], ys_ref[0], ya.astype(BF16)], axis=1)
    y = jnp.dot(ycat, wout_ref[...], preferred_element_type=F32)
    x1 = x_ref[0] + mod[2:3] * _rms(y, pmg_ref[...])
    x1_ref[0] = x1
    h2 = _rms(x1, pfg_ref[...]) * (1.0 + mod[4:5]) + mod[3:4]
    for c in range(ROW_CHUNKS):
        h2_ref[pl.ds(c, tm, stride=ROW_CHUNKS), :] = h2[:, c * LANES:(c + 1) * LANES]

    hh = h2.astype(BF16)
    hl = (h2 - hh.astype(F32)).astype(BF16)
    rwh = rwh_ref[...]
    logits = (jnp.dot(hh, rwh, preferred_element_type=F32) + jnp.dot(hl, rwh, preferred_element_type=F32)
              + jnp.dot(hh, rwl_ref[...], preferred_element_type=F32) + rb_ref[...])
    lane = lax.broadcasted_iota(I32, (tm, LANES), 1)
    lane_f = lane.astype(F32)
    work = jnp.where(lane < N_EXPERTS, logits, NEG_BIG)
    vals, idxs = [], []
    for _ in range(TOP_K):
        mx = jnp.max(work, axis=1, keepdims=True)
        ik = jnp.min(jnp.where(work == mx, lane_f, float(LANES)), axis=1, keepdims=True)
        vals.append(mx)
        idxs.append(ik)
        work = jnp.where(lane_f == ik, 2.0 * NEG_BIG, work)
    es = [jnp.exp(v - vals[0]) for v in vals]
    den = es[0] + es[1] + es[2] + es[3]

    onehot = jnp.zeros((tm, LANES), F32)
    for ik in idxs:
        onehot = onehot + jnp.where(lane_f == ik, 1.0, 0.0)
    r_i = lax.broadcasted_iota(I32, (tm, tm), 0)
    c_i = lax.broadcasted_iota(I32, (tm, tm), 1)
    below = jnp.where(c_i < r_i, 1.0, 0.0).astype(BF16)
    prior = jnp.dot(below, onehot.astype(BF16), preferred_element_type=F32) + carry_ref[...]
    carry = carry_ref[...] + jnp.sum(onehot, axis=0, keepdims=True)
    carry_ref[...] = carry
    cnt_ref[...] = carry.astype(I32)

    idx_full = jnp.zeros((tm, LANES), F32)
    gate_full = jnp.zeros((tm, LANES), F32)
    rank_full = jnp.zeros((tm, LANES), F32)
    for k in range(TOP_K):
        rk = jnp.sum(jnp.where(lane_f == idxs[k], prior, 0.0), axis=1, keepdims=True)
        idx_full = jnp.where(lane == k, idxs[k], idx_full)
        gate_full = jnp.where(lane == k, es[k] / den, gate_full)
        rank_full = jnp.where(lane == k, rk, rank_full)
    idx_ref[...] = idx_full[:, :TOP_K].astype(I32)
    gate_ref[...] = gate_full[:, :TOP_K]
    rank_ref[...] = rank_full[:, :TOP_K].astype(I32)


def _out_proj(yl, ys, ya, x, mod, gna, wout, pmg, pfg, rwh, rwl, rb):
    B, T, _ = x.shape
    tm = min(TM_OUT, T)
    nt = T // tm
    N = B * T

    def vec(n):
        return pl.BlockSpec((1, n), lambda b, t: (0, 0))

    tok = lambda b, t: (b * nt + t, 0)
    return pl.pallas_call(
        functools.partial(_out_proj_kernel, tm=tm),
        grid=(B, nt),
        in_specs=[pl.BlockSpec((1, tm, LRU_W), lambda b, t: (b, t, 0)),
                  pl.BlockSpec((1, tm, S5_W), lambda b, t: (b, t, 0)),
                  pl.BlockSpec((1, tm, ATTN_W), lambda b, t: (b, t, 0)),
                  pl.BlockSpec((1, tm, D), lambda b, t: (b, t, 0)),
                  pl.BlockSpec((1, N_MOD, D), lambda b, t: (b, 0, 0)),
                  vec(ATTN_W),
                  pl.BlockSpec((D, D), lambda b, t: (0, 0)),
                  vec(D), vec(D),
                  pl.BlockSpec((D, LANES), lambda b, t: (0, 0)),
                  pl.BlockSpec((D, LANES), lambda b, t: (0, 0)),
                  vec(LANES)],
        out_specs=[pl.BlockSpec((1, tm, D), lambda b, t: (b, t, 0)),
                   pl.BlockSpec((tm * ROW_CHUNKS, LANES), tok),
                   pl.BlockSpec((tm, TOP_K), tok),
                   pl.BlockSpec((tm, TOP_K), tok),
                   pl.BlockSpec((tm, TOP_K), tok),
                   pl.BlockSpec((1, LANES), lambda b, t: (0, 0))],
        out_shape=[jax.ShapeDtypeStruct((B, T, D), F32),
                   jax.ShapeDtypeStruct((N * ROW_CHUNKS, LANES), F32),
                   jax.ShapeDtypeStruct((N, TOP_K), I32),
                   jax.ShapeDtypeStruct((N, TOP_K), F32),
                   jax.ShapeDtypeStruct((N, TOP_K), I32),
                   jax.ShapeDtypeStruct((1, LANES), I32)],
        scratch_shapes=[pltpu.VMEM((1, LANES), F32)],
        compiler_params=_params(VMEM_LIMIT),
        name="out_proj_router",
    )(yl, ys, ya, x, mod, gna, wout, pmg, pfg, rwh, rwl, rb)


def _row_copy(src, src_row, dst, dst_row, sem):
    return pltpu.make_async_copy(src.at[pl.ds(pl.multiple_of(src_row * ROW_CHUNKS, ROW_CHUNKS), ROW_CHUNKS), :],
                                 dst.at[pl.ds(pl.multiple_of(dst_row * ROW_CHUNKS, ROW_CHUNKS), ROW_CHUNKS), :],
                                 sem)


def _dispatch_kernel(start_ref, cnt_ref, pad_ref, idx_ref, rank_ref, h_ref, dest_ref, xb_ref, sem, *, tn):
    i = pl.program_id(0)
    base = i * tn

    def issue(j, _):
        for k in range(TOP_K):
            a = j * TOP_K + k
            dst = start_ref[idx_ref[a]] + rank_ref[a]
            dest_ref[a] = dst
            _row_copy(h_ref, base + j, xb_ref, dst, sem).start()
        return 0

    lax.fori_loop(0, tn, issue, 0)

    def drain(j, _):
        for k in range(TOP_K):
            _row_copy(h_ref, 0, xb_ref, 0, sem).wait()
        return 0

    lax.fori_loop(0, tn, drain, 0)

    @pl.when(i == pl.num_programs(0) - 1)
    def _():
        def per_expert(e, _):
            lo = start_ref[e] + cnt_ref[e]
            hi = start_ref[e] + pad_ref[e]

            def fill(r, _):
                _row_copy(h_ref, 0, xb_ref, r, sem).start()
                return 0

            lax.fori_loop(lo, hi, fill, 0)

            def fill_wait(r, _):
                _row_copy(h_ref, 0, xb_ref, r, sem).wait()
                return 0

            lax.fori_loop(lo, hi, fill_wait, 0)
            return 0

        lax.fori_loop(0, N_EXPERTS, per_expert, 0)


def _dispatch(starts, counts, padded, idx_flat, rank_flat, h_rows, n_rows):
    NK = idx_flat.shape[0]
    N = NK // TOP_K
    tn = min(TN_ROWS, N)
    smem_blk = pl.BlockSpec((tn * TOP_K,), lambda i, *_: (i,), memory_space=pltpu.SMEM)
    grid_spec = pltpu.PrefetchScalarGridSpec(
        num_scalar_prefetch=3,
        grid=(N // tn,),
        in_specs=[smem_blk, smem_blk, pl.BlockSpec(memory_space=pl.ANY)],
        out_specs=[smem_blk, pl.BlockSpec(memory_space=pl.ANY)],
        scratch_shapes=[pltpu.SemaphoreType.DMA(())],
    )
    return pl.pallas_call(
        functools.partial(_dispatch_kernel, tn=tn),
        grid_spec=grid_spec,
        out_shape=[jax.ShapeDtypeStruct((NK,), I32),
                   jax.ShapeDtypeStruct((n_rows * ROW_CHUNKS, LANES), F32)],
        name="moe_dispatch",
    )(starts, counts, padded, idx_flat, rank_flat, h_rows)


def _expert_kernel(be_ref, nused_ref, x_ref, wgu_ref, bgu_ref, wdn_ref, bdn_ref, y_ref, wgu_s, wdn_s, *, rows):
    b = pl.program_id(0)
    prev = be_ref[jnp.maximum(b - 1, 0)]

    @pl.when((b == 0) | (be_ref[b] != prev))
    def _():
        wgu_s[...] = wgu_ref[0].astype(BF16)
        wdn_s[...] = wdn_ref[0].astype(BF16)

    @pl.when(b < nused_ref[0])
    def _():
        x = jnp.concatenate([x_ref[pl.ds(c, rows, stride=ROW_CHUNKS), :] for c in range(ROW_CHUNKS)], axis=1)
        gu = jnp.dot(x.astype(BF16), wgu_s[...], preferred_element_type=F32) + bgu_ref[0]
        g = jnp.minimum(gu[:, :D_FF], SWIGLU_LIMIT)
        up = jnp.clip(gu[:, D_FF:], -SWIGLU_LIMIT, SWIGLU_LIMIT)
        act = (up + 1.0) * (g * jax.nn.sigmoid(SWIGLU_ALPHA * g))
        y = jnp.dot(act.astype(BF16), wdn_s[...], preferred_element_type=F32) + bdn_ref[0]
        for c in range(ROW_CHUNKS):
            y_ref[pl.ds(c, rows, stride=ROW_CHUNKS), :] = y[:, c * LANES:(c + 1) * LANES]


def _experts(block_e, n_used, xb, w_gu, b_gu, w_dn, b_dn, rows):
    n_blocks = block_e.shape[0]
    E = w_gu.shape[0]
    blk = lambda b, be, nu: (jnp.minimum(b, nu[0] - 1), 0)
    grid_spec = pltpu.PrefetchScalarGridSpec(
        num_scalar_prefetch=2,
        grid=(n_blocks,),
        in_specs=[pl.BlockSpec((rows * ROW_CHUNKS, LANES), blk),
                  pl.BlockSpec((1, D, 2 * D_FF), lambda b, be, nu: (be[b], 0, 0)),
                  pl.BlockSpec((1, 1, 2 * D_FF), lambda b, be, nu: (be[b], 0, 0)),
                  pl.BlockSpec((1, D_FF, D), lambda b, be, nu: (be[b], 0, 0)),
                  pl.BlockSpec((1, 1, D), lambda b, be, nu: (be[b], 0, 0))],
        out_specs=pl.BlockSpec((rows * ROW_CHUNKS, LANES), blk),
        scratch_shapes=[pltpu.VMEM((D, 2 * D_FF), BF16), pltpu.VMEM((D_FF, D), BF16)],
    )
    return pl.pallas_call(
        functools.partial(_expert_kernel, rows=rows),
        grid_spec=grid_spec,
        out_shape=jax.ShapeDtypeStruct(xb.shape, F32),
        compiler_params=_params(VMEM_LIMIT),
        name="moe_experts",
    )(block_e, n_used, xb, w_gu, b_gu.reshape(E, 1, 2 * D_FF), w_dn, b_dn.reshape(E, 1, D))


def _combine_kernel(dest_ref, yb_ref, gate_ref, x1_ref, mod_ref, g_ref, o_ref, buf_ref, sem, *, tn):
    def issue(j, _):
        for k in range(TOP_K):
            pltpu.make_async_copy(
                yb_ref.at[pl.ds(pl.multiple_of(dest_ref[j * TOP_K + k] * ROW_CHUNKS, ROW_CHUNKS), ROW_CHUNKS), :],
                buf_ref.at[k, pl.ds(pl.multiple_of(j * ROW_CHUNKS, ROW_CHUNKS), ROW_CHUNKS), :], sem).start()
        return 0

    lax.fori_loop(0, tn, issue, 0)

    def drain(j, _):
        for k in range(TOP_K):
            pltpu.make_async_copy(yb_ref.at[pl.ds(0, ROW_CHUNKS), :],
                                  buf_ref.at[k, pl.ds(0, ROW_CHUNKS), :], sem).wait()
        return 0

    lax.fori_loop(0, tn, drain, 0)

    gates = gate_ref[...]
    y = jnp.zeros((tn, D), F32)
    for k in range(TOP_K):
        yk = jnp.concatenate([buf_ref[k, pl.ds(c, tn, stride=ROW_CHUNKS), :] for c in range(ROW_CHUNKS)], axis=1)
        y = y + gates[:, k:k + 1] * yk
    mod = mod_ref[0]
    o_ref[0] = x1_ref[0] + mod[5:6] * _rms(y, g_ref[...])


def _combine(dest_flat, yb, gates, x1, mod, g):
    B, T, _ = x1.shape
    tn = min(TN_COMB, T)
    nt = T // tn
    grid_spec = pltpu.PrefetchScalarGridSpec(
        num_scalar_prefetch=0,
        grid=(B, nt),
        in_specs=[pl.BlockSpec((tn * TOP_K,), lambda b, t: (b * nt + t,), memory_space=pltpu.SMEM),
                  pl.BlockSpec(memory_space=pl.ANY),
                  pl.BlockSpec((tn, TOP_K), lambda b, t: (b * nt + t, 0)),
                  pl.BlockSpec((1, tn, D), lambda b, t: (b, t, 0)),
                  pl.BlockSpec((1, N_MOD, D), lambda b, t: (b, 0, 0)),
                  pl.BlockSpec((1, D), lambda b, t: (0, 0))],
        out_specs=pl.BlockSpec((1, tn, D), lambda b, t: (b, t, 0)),
        scratch_shapes=[pltpu.VMEM((TOP_K, tn * ROW_CHUNKS, LANES), F32), pltpu.SemaphoreType.DMA(())],
    )
    return pl.pallas_call(
        functools.partial(_combine_kernel, tn=tn),
        grid_spec=grid_spec,
        out_shape=jax.ShapeDtypeStruct((B, T, D), F32),
        name="moe_combine",
    )(dest_flat, yb, gates, x1, mod, g)


def _head_slabs(w):
    w = w.reshape(D, N_HEADS, HEAD_DIM)
    return jnp.pad(w, ((0, 0), (0, 0), (0, LANES - HEAD_DIM))).reshape(D, N_HEADS * LANES)


def _block_diag(w):
    nb = w.shape[0]
    return jnp.einsum('hij,hg->higj', w, jnp.eye(nb, dtype=w.dtype)).reshape(LRU_W, LRU_W)


def kernel(x, c, ada_w, ada_b, pre_mix_g, w_in, conv_w, conv_b, lru_wa, lru_ba, lru_wx, lru_bx, lru_lambda, s5_a_re, s5_a_im, s5_b_re, s5_b_im, s5_c_re, s5_c_im, s5_d, s5_log_dt, s5_glu_w, s5_glu_b, fox_fb, gn_lru, gn_s5, gn_attn, w_out, post_mix_g, pre_ffn_g, router_w, router_b, w_gu, b_gu, w_dn, b_dn, post_ffn_g):
    B, T, _ = x.shape
    N = B * T
    L = ada_w.shape[0]
    tc_s5 = min(TC_S5, T)
    n_blocks = (N * TOP_K) // EXP_ROWS + N_EXPERTS
    n_rows = n_blocks * EXP_ROWS
    mod_all = _modulation(c, ada_w, ada_b).reshape(L, B, N_MOD, D)

    for l in range(L):
        mod = mod_all[l]
        wl = w_in[l]
        w_aug = jnp.concatenate(
            [wl[:, :O_Q], _head_slabs(wl[:, O_Q:O_K]), _head_slabs(wl[:, O_K:O_V]), _head_slabs(wl[:, O_V:O_F]),
             jnp.pad(wl[:, O_F:], ((0, 0), (0, LANES - N_HEADS)))], axis=1).astype(BF16)
        fb = jnp.pad(fox_fb[l], (0, LANES - N_HEADS)).reshape(1, LANES)
        lx, lg, su, qa, ka, va = _in_proj(x, mod, pre_mix_g[l].reshape(1, D), w_aug, fb)

        yl = _lru_branch(lx, lg, conv_w[l], conv_b[l].reshape(1, LRU_W),
                         _block_diag(lru_wa[l]).astype(BF16), lru_ba[l].reshape(1, LRU_W),
                         _block_diag(lru_wx[l]).astype(BF16), lru_bx[l].reshape(1, LRU_W),
                         lru_lambda[l].reshape(1, LRU_W), gn_lru[l].reshape(1, LRU_W))

        tabs = _s5_tables(s5_a_re[l], s5_a_im[l], s5_b_re[l], s5_b_im[l], s5_c_re[l], s5_c_im[l], s5_log_dt[l], tc_s5)
        ys = _s5_branch(su, *tabs, s5_d[l].reshape(1, S5_W), s5_glu_w[l].astype(BF16),
                        s5_glu_b[l].reshape(1, S5_W), gn_s5[l].reshape(1, S5_W), tc_s5)

        ya = _attention(qa, ka, va)

        rw = jnp.pad(router_w[l], ((0, 0), (0, LANES - N_EXPERTS)))
        rwh = rw.astype(BF16)
        rwl = (rw - rwh.astype(F32)).astype(BF16)
        rb = jnp.pad(router_b[l], (0, LANES - N_EXPERTS)).reshape(1, LANES)
        x1, h_rows, idx, gates, rank, cnt = _out_proj(
            yl, ys, ya, x, mod, gn_attn[l].reshape(1, ATTN_W), w_out[l].astype(BF16),
            post_mix_g[l].reshape(1, D), pre_ffn_g[l].reshape(1, D), rwh, rwl, rb)

        counts = cnt[0, :N_EXPERTS]
        blocks_e = (counts + EXP_ROWS - 1) // EXP_ROWS
        padded = blocks_e * EXP_ROWS
        block_ends = jnp.cumsum(blocks_e)
        starts = (block_ends - blocks_e) * EXP_ROWS
        n_used = block_ends[-1:]
        bid = jnp.minimum(jnp.arange(n_blocks, dtype=I32), n_used[0] - 1)
        block_e = jnp.minimum(jnp.searchsorted(block_ends, bid, side='right'), N_EXPERTS - 1).astype(I32)

        dest, xb = _dispatch(starts.astype(I32), counts, padded.astype(I32),
                             idx.reshape(N * TOP_K), rank.reshape(N * TOP_K), h_rows, n_rows)
        yb = _experts(block_e, n_used.astype(I32), xb, w_gu[l], b_gu[l], w_dn[l], b_dn[l], EXP_ROWS)
        x = _combine(dest, yb, gates, x1, mod, post_ffn_g[l].reshape(1, D))
    return x
```

```python
import functools

import jax
import jax.numpy as jnp
from jax import lax
from jax.experimental import pallas as pl
from jax.experimental.pallas import tpu as pltpu

F32 = jnp.float32
BF16 = jnp.bfloat16
I32 = jnp.int32

D = 1024
LRU_W = 256
LRU_BLOCK_W = 64
CONV_W = 4
LRU_C = 8.0
S5_W = 256
S5_GROUP_W = 16
S5_GROUPS = 16
S5_STATE = 64
S5_STATES = S5_GROUPS * S5_STATE
HEAD_DIM = 64
ATTN_W = 512
N_HEADS = 8
O_LRU_G = 256
O_S5 = 512
O_Q = 768
O_K = O_Q + ATTN_W
O_V = O_K + ATTN_W
O_F = O_V + ATTN_W
N_EXPERTS = 32
TOP_K = 4
D_FF = 1024
SWIGLU_LIMIT = 7.0
SWIGLU_ALPHA = 1.702
N_MOD = 6
RMS_EPS = 1e-6

LANES = 128
SUBLANES = 8
VMEM_LIMIT = 56 * 1024 * 1024
ROW_CHUNKS = D // LANES

C_Q = 768
C_K = C_Q + N_HEADS * LANES
C_V = C_K + N_HEADS * LANES
C_F = C_V + N_HEADS * LANES
IN_COLS_PAD = C_F + LANES
L_A0, L_A1, L_A2, L_B0, L_B1, L_B2 = 64, 65, 66, 67, 68, 69

TM_IN = 256
TC_LRU = 256
TC_S5 = 128
TQ = 512
TM_OUT = 512
TN_ROWS = 1024
TN_COMB = 256
EXP_ROWS = 256
NEG_BIG = -1e30


def _rms(x, g):
    return x * lax.rsqrt(jnp.mean(x * x, axis=-1, keepdims=True) + RMS_EPS) * g


def _shift_rows(v, d, row, fill):
    return jnp.where(row >= d, pltpu.roll(v, d, 0), fill)


def _params(vmem=None):
    return pltpu.CompilerParams(vmem_limit_bytes=vmem) if vmem else None


def _mod_kernel(c_ref, w_ref, b_ref, o_ref):
    c = c_ref[...]
    s = c * jax.nn.sigmoid(c)
    o_ref[0] = jnp.dot(s.astype(BF16), w_ref[0].astype(BF16), preferred_element_type=F32) + b_ref[0]


def _modulation(c, ada_w, ada_b):
    L, _, W = ada_w.shape
    B = c.shape[0]
    return pl.pallas_call(
        _mod_kernel,
        grid=(L, W // D),
        in_specs=[pl.BlockSpec((B, D), lambda l, j: (0, 0)),
                  pl.BlockSpec((1, D, D), lambda l, j: (l, 0, j)),
                  pl.BlockSpec((1, 1, D), lambda l, j: (l, 0, j))],
        out_specs=pl.BlockSpec((1, B, D), lambda l, j: (l, 0, j)),
        out_shape=jax.ShapeDtypeStruct((L, B, W), F32),
        name="adaln_mod",
    )(c, ada_w, ada_b.reshape(L, 1, W))


def _in_proj_kernel(x_ref, mod_ref, g_ref, w_ref, fb_ref,
                    lx_ref, lg_ref, su_ref, q_ref, k_ref, v_ref, fcarry_ref, *, tm):
    @pl.when(pl.program_id(1) == 0)
    def _():
        fcarry_ref[...] = jnp.zeros_like(fcarry_ref)

    mod = mod_ref[0]
    h = _rms(x_ref[0], g_ref[...]) * (1.0 + mod[1:2]) + mod[0:1]
    p = jnp.dot(h.astype(BF16), w_ref[...], preferred_element_type=F32)
    lx_ref[0] = p[:, 0:O_LRU_G]
    lg_ref[0] = p[:, O_LRU_G:O_S5]
    su_ref[0] = p[:, O_S5:O_Q]

    z = p[:, C_F:C_F + LANES] + fb_ref[...]
    logf = jnp.minimum(z, 0.0) - jnp.log1p(jnp.exp(-jnp.abs(z)))
    row = lax.broadcasted_iota(I32, (tm, LANES), 0)
    d = 1
    while d < tm:
        logf = logf + _shift_rows(logf, d, row, 0.0)
        d *= 2
    fsum = logf + fcarry_ref[...]
    fcarry_ref[...] = fsum[tm - 1:tm, :]

    fexp = jnp.concatenate([jnp.broadcast_to(fsum[:, h:h + 1], (tm, LANES)) for h in range(N_HEADS)], axis=1)
    hi = fexp.astype(BF16).astype(F32)
    r1 = fexp - hi
    mid = r1.astype(BF16).astype(F32)
    lo = r1 - mid
    lane = lax.broadcasted_iota(I32, (tm, N_HEADS * LANES), 1) & (LANES - 1)
    pq = p[:, C_Q:C_K] * (HEAD_DIM ** -0.5)
    pk = p[:, C_K:C_V]
    pv = p[:, C_V:C_F]
    ones = (lane >= L_B0) & (lane <= L_B2)
    qa = jnp.where(lane == L_A0, hi, jnp.where(lane == L_A1, mid, jnp.where(lane == L_A2, lo,
                                                                            jnp.where(ones, 1.0, pq))))
    ones = (lane >= L_A0) & (lane <= L_A2)
    ka = jnp.where(lane == L_B0, -hi, jnp.where(lane == L_B1, -mid, jnp.where(lane == L_B2, -lo,
                                                                              jnp.where(ones, 1.0, pk))))
    va = jnp.where(lane == HEAD_DIM, 1.0, pv)
    for h in range(N_HEADS):
        sl = slice(h * LANES, (h + 1) * LANES)
        q_ref[0, h] = qa[:, sl].astype(BF16)
        k_ref[0, h] = ka[:, sl].astype(BF16)
        v_ref[0, h] = va[:, sl].astype(BF16)


def _in_proj(x, mod, g, w_aug, fb):
    B, T, _ = x.shape
    tm = min(TM_IN, T)
    grp = jax.ShapeDtypeStruct((B, T, 256), F32)
    head = jax.ShapeDtypeStruct((B, N_HEADS, T, LANES), BF16)
    grp_spec = pl.BlockSpec((1, tm, 256), lambda b, t: (b, t, 0))
    head_spec = pl.BlockSpec((1, N_HEADS, tm, LANES), lambda b, t: (b, 0, t, 0))
    return pl.pallas_call(
        functools.partial(_in_proj_kernel, tm=tm),
        grid=(B, T // tm),
        in_specs=[pl.BlockSpec((1, tm, D), lambda b, t: (b, t, 0)),
                  pl.BlockSpec((1, N_MOD, D), lambda b, t: (b, 0, 0)),
                  pl.BlockSpec((1, D), lambda b, t: (0, 0)),
                  pl.BlockSpec((D, IN_COLS_PAD), lambda b, t: (0, 0)),
                  pl.BlockSpec((1, LANES), lambda b, t: (0, 0))],
        out_specs=[grp_spec, grp_spec, grp_spec, head_spec, head_spec, head_spec],
        out_shape=[grp, grp, grp, head, head, head],
        scratch_shapes=[pltpu.VMEM((1, LANES), F32)],
        compiler_params=_params(VMEM_LIMIT),
        name="in_proj",
    )(x, mod, g, w_aug, fb)


def _lru_kernel(lx_ref, lg_ref, cw_ref, cb_ref, wa_ref, ba_ref, wx_ref, bx_ref, lam_ref, gn_ref,
                y_ref, tail_ref, hcarry_ref, *, tc):
    @pl.when(pl.program_id(1) == 0)
    def _():
        tail_ref[...] = jnp.zeros_like(tail_ref)
        hcarry_ref[...] = jnp.zeros_like(hcarry_ref)

    x = lx_ref[0]
    xcat = jnp.concatenate([tail_ref[...], x], axis=0)
    cw = cw_ref[...]
    xr = cb_ref[...]
    for j in range(CONV_W):
        d = CONV_W - 1 - j
        xs = x if d == 0 else pltpu.roll(xcat, d, 0)[SUBLANES:SUBLANES + tc]
        xr = xr + xs * cw[j:j + 1]
    tail_ref[...] = x[tc - SUBLANES:tc]

    xb = xr.astype(BF16)
    r = jax.nn.sigmoid(jnp.dot(xb, wa_ref[...], preferred_element_type=F32) + ba_ref[...])
    i = jax.nn.sigmoid(jnp.dot(xb, wx_ref[...], preferred_element_type=F32) + bx_ref[...])
    nl = -lam_ref[...]
    softplus = jnp.maximum(nl, 0.0) + jnp.log1p(jnp.exp(-jnp.abs(nl)))
    log_a = -LRU_C * r * softplus
    a = jnp.exp(log_a)
    th = jnp.tanh(log_a)
    u = xr * i * jnp.sqrt(-2.0 * th / (1.0 - th))

    row = lax.broadcasted_iota(I32, (tc, LRU_W), 0)
    d = 1
    while d < tc:
        u = u + a * _shift_rows(u, d, row, 0.0)
        a = a * _shift_rows(a, d, row, 1.0)
        d *= 2
    h = u + a * hcarry_ref[...]
    hcarry_ref[...] = h[tc - 1:tc]

    y = h * jax.nn.gelu(lg_ref[0])
    y_ref[0] = _rms(y, gn_ref[...]).astype(BF16)


def _lru_branch(lx, lg, conv_w, conv_b, wa, ba, wx, bx, lam, gn):
    B, T, _ = lx.shape
    tc = min(TC_LRU, T)
    tile = pl.BlockSpec((1, tc, LRU_W), lambda b, t: (b, t, 0))
    vec = pl.BlockSpec((1, LRU_W), lambda b, t: (0, 0))
    mat = pl.BlockSpec((LRU_W, LRU_W), lambda b, t: (0, 0))
    return pl.pallas_call(
        functools.partial(_lru_kernel, tc=tc),
        grid=(B, T // tc),
        in_specs=[tile, tile, pl.BlockSpec((CONV_W, LRU_W), lambda b, t: (0, 0)), vec, mat, vec, mat, vec, vec, vec],
        out_specs=tile,
        out_shape=jax.ShapeDtypeStruct((B, T, LRU_W), BF16),
        scratch_shapes=[pltpu.VMEM((SUBLANES, LRU_W), F32), pltpu.VMEM((1, LRU_W), F32)],
        name="rg_lru",
    )(lx, lg, conv_w, conv_b, wa, ba, wx, bx, lam, gn)


def _s5_kernel(u_ref, bcat_ref, apr_ref, api_ref, ptr_ref, pti_ref, ccat_ref, d_ref, gw_ref, gb_ref, gn_ref,
               y_ref, cr_ref, ci_ref, *, tc):
    @pl.when(pl.program_id(1) == 0)
    def _():
        cr_ref[...] = jnp.zeros_like(cr_ref)
        ci_ref[...] = jnp.zeros_like(ci_ref)

    u = u_ref[0]
    bu = jnp.dot(u.astype(BF16), bcat_ref[...], preferred_element_type=F32)
    xr = bu[:, :S5_STATES]
    xi = bu[:, S5_STATES:]
    row = lax.broadcasted_iota(I32, (tc, S5_STATES), 0)
    k, d = 0, 1
    while d < tc:
        ar = apr_ref[k:k + 1, :]
        ai = api_ref[k:k + 1, :]
        sr = _shift_rows(xr, d, row, 0.0)
        si = _shift_rows(xi, d, row, 0.0)
        xr, xi = xr + (ar * sr - ai * si), xi + (ar * si + ai * sr)
        k, d = k + 1, d * 2
    pr = ptr_ref[...]
    pi = pti_ref[...]
    cr = cr_ref[...]
    ci = ci_ref[...]
    xr, xi = xr + (pr * cr - pi * ci), xi + (pr * ci + pi * cr)
    cr_ref[...] = xr[tc - 1:tc]
    ci_ref[...] = xi[tc - 1:tc]

    xcat = jnp.concatenate([xr, xi], axis=1).astype(BF16)
    y = jnp.dot(xcat, ccat_ref[...], preferred_element_type=F32) + d_ref[...] * u
    y = jax.nn.gelu(y)
    y = y * jax.nn.sigmoid(jnp.dot(y.astype(BF16), gw_ref[...], preferred_element_type=F32) + gb_ref[...])
    y_ref[0] = _rms(y, gn_ref[...]).astype(BF16)


def _s5_branch(u, bcat, apow_re, apow_im, ptab_re, ptab_im, ccat, d, glu_w, glu_b, gn, tc):
    B, T, _ = u.shape
    tile = pl.BlockSpec((1, tc, S5_W), lambda b, t: (b, t, 0))
    vec = pl.BlockSpec((1, S5_W), lambda b, t: (0, 0))

    def full(a):
        return pl.BlockSpec(a.shape, lambda b, t: (0, 0))

    return pl.pallas_call(
        functools.partial(_s5_kernel, tc=tc),
        grid=(B, T // tc),
        in_specs=[tile, full(bcat), full(apow_re), full(apow_im), full(ptab_re), full(ptab_im), full(ccat),
                  vec, full(glu_w), vec, vec],
        out_specs=tile,
        out_shape=jax.ShapeDtypeStruct((B, T, S5_W), BF16),
        scratch_shapes=[pltpu.VMEM((1, S5_STATES), F32), pltpu.VMEM((1, S5_STATES), F32)],
        compiler_params=_params(VMEM_LIMIT),
        name="s5",
    )(u, bcat, apow_re, apow_im, ptab_re, ptab_im, ccat, d, glu_w, glu_b, gn)


def _s5_tables(a_re, a_im, b_re, b_im, c_re, c_im, log_dt, tc):
    dt = jnp.exp(log_dt)[:, None]
    mag = jnp.exp(a_re * dt)
    abar_re = mag * jnp.cos(a_im * dt)
    abar_im = mag * jnp.sin(a_im * dt)
    den = a_re * a_re + a_im * a_im
    num_re = abar_re - 1.0
    k_re = (num_re * a_re + abar_im * a_im) / den
    k_im = (abar_im * a_re - num_re * a_im) / den
    bbar_re = k_re[..., None] * b_re - k_im[..., None] * b_im
    bbar_im = k_re[..., None] * b_im + k_im[..., None] * b_re
    eye = jnp.eye(S5_GROUPS, dtype=F32)
    bd_re = jnp.einsum('gpc,gh->gchp', bbar_re, eye).reshape(S5_W, S5_STATES)
    bd_im = jnp.einsum('gpc,gh->gchp', bbar_im, eye).reshape(S5_W, S5_STATES)
    bcat = jnp.concatenate([bd_re, bd_im], axis=1).astype(BF16)
    cd_re = jnp.einsum('gcp,gh->gphc', c_re, eye).reshape(S5_STATES, S5_W)
    cd_im = jnp.einsum('gcp,gh->gphc', c_im, eye).reshape(S5_STATES, S5_W)
    ccat = jnp.concatenate([cd_re, -cd_im], axis=0).astype(BF16)
    ar = abar_re.reshape(1, S5_STATES)
    ai = abar_im.reshape(1, S5_STATES)
    pows_r, pows_i = [ar], [ai]
    tab_r, tab_i = ar, ai
    n = 1
    while n < tc:
        sr, si = pows_r[-1], pows_i[-1]
        tab_r, tab_i = (jnp.concatenate([tab_r, tab_r * sr - tab_i * si], axis=0),
                        jnp.concatenate([tab_i, tab_r * si + tab_i * sr], axis=0))
        pows_r.append(sr * sr - si * si)
        pows_i.append(2.0 * sr * si)
        n *= 2
    levels = len(pows_r) - 1
    pad = (-levels) % SUBLANES
    apow_re = jnp.concatenate(pows_r[:levels] + [jnp.zeros((pad, S5_STATES), F32)], axis=0)
    apow_im = jnp.concatenate(pows_i[:levels] + [jnp.zeros((pad, S5_STATES), F32)], axis=0)
    return bcat, apow_re, apow_im, tab_r, tab_i, ccat


def _attn_kernel(q_ref, k_ref, v_ref, o_ref, *, tq):
    qi = pl.program_id(2)
    row = lax.broadcasted_iota(I32, (tq, tq), 0)
    col = lax.broadcasted_iota(I32, (tq, tq), 1)
    outs = []
    for hh in range(2):
        q = q_ref[0, hh]

        def scores(kj):
            k = k_ref[0, hh, pl.ds(pl.multiple_of(kj * tq, tq), tq), :]
            return lax.dot_general(q, k, (((1,), (1,)), ((), ())), preferred_element_type=F32)

        def update(kj, s, m, acc):
            v = v_ref[0, hh, pl.ds(pl.multiple_of(kj * tq, tq), tq), :]
            m_new = jnp.maximum(m, jnp.max(s, axis=1, keepdims=True))
            p = jnp.exp(s - m_new)
            acc = jnp.exp(m - m_new) * acc + jnp.dot(p.astype(BF16), v, preferred_element_type=F32)
            return m_new, acc

        def body(kj, carry):
            return update(kj, scores(kj), *carry)

        m0 = jnp.full((tq, 1), NEG_BIG, F32)
        acc0 = jnp.zeros((tq, LANES), F32)
        m, acc = lax.fori_loop(0, qi, body, (m0, acc0))
        s = jnp.where(col <= row, scores(qi), NEG_BIG)
        m, acc = update(qi, s, m, acc)
        outs.append(acc[:, :HEAD_DIM] / acc[:, HEAD_DIM:HEAD_DIM + 1])
    o_ref[0] = jnp.concatenate(outs, axis=1)


def _attention(qa, ka, va):
    B, H, T, _ = qa.shape
    tq = min(TQ, T)
    kv_spec = pl.BlockSpec((1, 2, T, LANES), lambda b, hp, qi: (b, hp, 0, 0))
    return pl.pallas_call(
        functools.partial(_attn_kernel, tq=tq),
        grid=(B, H // 2, T // tq),
        in_specs=[pl.BlockSpec((1, 2, tq, LANES), lambda b, hp, qi: (b, hp, qi, 0)), kv_spec, kv_spec],
        out_specs=pl.BlockSpec((1, tq, 2 * HEAD_DIM), lambda b, hp, qi: (b, qi, hp)),
        out_shape=jax.ShapeDtypeStruct((B, T, ATTN_W), F32),
        compiler_params=_params(VMEM_LIMIT),
        name="fox_attention",
    )(qa, ka, va)


def _out_proj_kernel(yl_ref, ys_ref, ya_ref, x_ref, mod_ref, gna_ref, wout_ref, pmg_ref, pfg_ref,
                     rwh_ref, rwl_ref, rb_ref,
                     x1_ref, h2_ref, idx_ref, gate_ref, rank_ref, cnt_ref, carry_ref, *, tm):
    @pl.when((pl.program_id(0) == 0) & (pl.program_id(1) == 0))
    def _():
        carry_ref[...] = jnp.zeros_like(carry_ref)

    mod = mod_ref[0]
    ya = _rms(ya_ref[0], gna_ref[...])
    ycat = jnp.concatenate([yl_ref[0], ys_ref[0], ya.astype(BF16)], axis=1)
    y = jnp.dot(ycat, wout_ref[...], preferred_element_type=F32)
    x1 = x_ref[0] + mod[2:3] * _rms(y, pmg_ref[...])
    x1_ref[0] = x1
    h2 = _rms(x1, pfg_ref[...]) * (1.0 + mod[4:5]) + mod[3:4]
    for c in range(ROW_CHUNKS):
        h2_ref[pl.ds(c, tm, stride=ROW_CHUNKS), :] = h2[:, c * LANES:(c + 1) * LANES]

    hh = h2.astype(BF16)
    hl = (h2 - hh.astype(F32)).astype(BF16)
    rwh = rwh_ref[...]
    logits = (jnp.dot(hh, rwh, preferred_element_type=F32) + jnp.dot(hl, rwh, preferred_element_type=F32)
              + jnp.dot(hh, rwl_ref[...], preferred_element_type=F32) + rb_ref[...])
    lane = lax.broadcasted_iota(I32, (tm, LANES), 1)
    lane_f = lane.astype(F32)
    work = jnp.where(lane < N_EXPERTS, logits, NEG_BIG)
    vals, idxs = [], []
    for _ in range(TOP_K):
        mx = jnp.max(work, axis=1, keepdims=True)
        ik = jnp.min(jnp.where(work == mx, lane_f, float(LANES)), axis=1, keepdims=True)
        vals.append(mx)
        idxs.append(ik)
        work = jnp.where(lane_f == ik, 2.0 * NEG_BIG, work)
    es = [jnp.exp(v - vals[0]) for v in vals]
    den = es[0] + es[1] + es[2] + es[3]

    onehot = jnp.zeros((tm, LANES), F32)
    for ik in idxs:
        onehot = onehot + jnp.where(lane_f == ik, 1.0, 0.0)
    r_i = lax.broadcasted_iota(I32, (tm, tm), 0)
    c_i = lax.broadcasted_iota(I32, (tm, tm), 1)
    below = jnp.where(c_i < r_i, 1.0, 0.0).astype(BF16)
    prior = jnp.dot(below, onehot.astype(BF16), preferred_element_type=F32) + carry_ref[...]
    carry = carry_ref[...] + jnp.sum(onehot, axis=0, keepdims=True)
    carry_ref[...] = carry
    cnt_ref[...] = carry.astype(I32)

    idx_full = jnp.zeros((tm, LANES), F32)
    gate_full = jnp.zeros((tm, LANES), F32)
    rank_full = jnp.zeros((tm, LANES), F32)
    for k in range(TOP_K):
        rk = jnp.sum(jnp.where(lane_f == idxs[k], prior, 0.0), axis=1, keepdims=True)
        idx_full = jnp.where(lane == k, idxs[k], idx_full)
        gate_full = jnp.where(lane == k, es[k] / den, gate_full)
        rank_full = jnp.where(lane == k, rk, rank_full)
    idx_ref[...] = idx_full[:, :TOP_K].astype(I32)
    gate_ref[...] = gate_full[:, :TOP_K]
    rank_ref[...] = rank_full[:, :TOP_K].astype(I32)


def _out_proj(yl, ys, ya, x, mod, gna, wout, pmg, pfg, rwh, rwl, rb):
    B, T, _ = x.shape
    tm = min(TM_OUT, T)
    nt = T // tm
    N = B * T

    def vec(n):
        return pl.BlockSpec((1, n), lambda b, t: (0, 0))

    tok = lambda b, t: (b * nt + t, 0)
    return pl.pallas_call(
        functools.partial(_out_proj_kernel, tm=tm),
        grid=(B, nt),
        in_specs=[pl.BlockSpec((1, tm, LRU_W), lambda b, t: (b, t, 0)),
                  pl.BlockSpec((1, tm, S5_W), lambda b, t: (b, t, 0)),
                  pl.BlockSpec((1, tm, ATTN_W), lambda b, t: (b, t, 0)),
                  pl.BlockSpec((1, tm, D), lambda b, t: (b, t, 0)),
                  pl.BlockSpec((1, N_MOD, D), lambda b, t: (b, 0, 0)),
                  vec(ATTN_W),
                  pl.BlockSpec((D, D), lambda b, t: (0, 0)),
                  vec(D), vec(D),
                  pl.BlockSpec((D, LANES), lambda b, t: (0, 0)),
                  pl.BlockSpec((D, LANES), lambda b, t: (0, 0)),
                  vec(LANES)],
        out_specs=[pl.BlockSpec((1, tm, D), lambda b, t: (b, t, 0)),
                   pl.BlockSpec((tm * ROW_CHUNKS, LANES), tok),
                   pl.BlockSpec((tm, TOP_K), tok),
                   pl.BlockSpec((tm, TOP_K), tok),
                   pl.BlockSpec((tm, TOP_K), tok),
                   pl.BlockSpec((1, LANES), lambda b, t: (0, 0))],
        out_shape=[jax.ShapeDtypeStruct((B, T, D), F32),
                   jax.ShapeDtypeStruct((N * ROW_CHUNKS, LANES), F32),
                   jax.ShapeDtypeStruct((N, TOP_K), I32),
                   jax.ShapeDtypeStruct((N, TOP_K), F32),
                   jax.ShapeDtypeStruct((N, TOP_K), I32),
                   jax.ShapeDtypeStruct((1, LANES), I32)],
        scratch_shapes=[pltpu.VMEM((1, LANES), F32)],
        compiler_params=_params(VMEM_LIMIT),
        name="out_proj_router",
    )(yl, ys, ya, x, mod, gna, wout, pmg, pfg, rwh, rwl, rb)


def _row_copy(src, src_row, dst, dst_row, sem):
    return pltpu.make_async_copy(src.at[pl.ds(pl.multiple_of(src_row * ROW_CHUNKS, ROW_CHUNKS), ROW_CHUNKS), :],
                                 dst.at[pl.ds(pl.multiple_of(dst_row * ROW_CHUNKS, ROW_CHUNKS), ROW_CHUNKS), :],
                                 sem)


def _dispatch_kernel(start_ref, cnt_ref, pad_ref, idx_ref, rank_ref, h_ref, dest_ref, xb_ref, sem, *, tn):
    i = pl.program_id(0)

    def issue(j, _):
        for k in range(TOP_K):
            a = j * TOP_K + k
            dst = start_ref[idx_ref[a]] + rank_ref[a]
            dest_ref[a] = dst
            _row_copy(h_ref, j, xb_ref, dst, sem).start()
        return 0

    lax.fori_loop(0, tn, issue, 0)

    def drain(j, _):
        for k in range(TOP_K):
            _row_copy(h_ref, 0, xb_ref, 0, sem).wait()
        return 0

    lax.fori_loop(0, tn, drain, 0)

    @pl.when(i == pl.num_programs(0) - 1)
    def _():
        def per_expert(e, _):
            lo = start_ref[e] + cnt_ref[e]
            hi = start_ref[e] + pad_ref[e]

            def fill(r, _):
                _row_copy(h_ref, 0, xb_ref, r, sem).start()
                return 0

            lax.fori_loop(lo, hi, fill, 0)

            def fill_wait(r, _):
                _row_copy(h_ref, 0, xb_ref, r, sem).wait()
                return 0

            lax.fori_loop(lo, hi, fill_wait, 0)
            return 0

        lax.fori_loop(0, N_EXPERTS, per_expert, 0)


def _dispatch(starts, counts, padded, idx_flat, rank_flat, h_rows, n_rows):
    NK = idx_flat.shape[0]
    N = NK // TOP_K
    tn = min(TN_ROWS, N)
    smem_blk = pl.BlockSpec((tn * TOP_K,), lambda i, *_: (i,), memory_space=pltpu.SMEM)
    grid_spec = pltpu.PrefetchScalarGridSpec(
        num_scalar_prefetch=3,
        grid=(N // tn,),
        in_specs=[smem_blk, smem_blk, pl.BlockSpec((tn * ROW_CHUNKS, LANES), lambda i, *_: (i, 0))],
        out_specs=[smem_blk, pl.BlockSpec(memory_space=pl.ANY)],
        scratch_shapes=[pltpu.SemaphoreType.DMA(())],
    )
    return pl.pallas_call(
        functools.partial(_dispatch_kernel, tn=tn),
        grid_spec=grid_spec,
        out_shape=[jax.ShapeDtypeStruct((NK,), I32),
                   jax.ShapeDtypeStruct((n_rows * ROW_CHUNKS, LANES), F32)],
        name="moe_dispatch",
    )(starts, counts, padded, idx_flat, rank_flat, h_rows)


def _expert_kernel(be_ref, nused_ref, x_ref, wgu_ref, bgu_ref, wdn_ref, bdn_ref, y_ref, wgu_s, wdn_s, *, rows):
    b = pl.program_id(0)
    prev = be_ref[jnp.maximum(b - 1, 0)]

    @pl.when((b == 0) | (be_ref[b] != prev))
    def _():
        wgu_s[...] = wgu_ref[0, 0].astype(BF16)
        wdn_s[...] = wdn_ref[0, 0].astype(BF16)

    @pl.when(b < nused_ref[0])
    def _():
        x = jnp.concatenate([x_ref[pl.ds(c, rows, stride=ROW_CHUNKS), :] for c in range(ROW_CHUNKS)], axis=1)
        gu = jnp.dot(x.astype(BF16), wgu_s[...], preferred_element_type=F32) + bgu_ref[0, 0]
        g = jnp.minimum(gu[:, :D_FF], SWIGLU_LIMIT)
        up = jnp.clip(gu[:, D_FF:], -SWIGLU_LIMIT, SWIGLU_LIMIT)
        act = (up + 1.0) * (g * jax.nn.sigmoid(SWIGLU_ALPHA * g))
        y = jnp.dot(act.astype(BF16), wdn_s[...], preferred_element_type=F32) + bdn_ref[0, 0]
        for c in range(ROW_CHUNKS):
            y_ref[pl.ds(c, rows, stride=ROW_CHUNKS), :] = y[:, c * LANES:(c + 1) * LANES]


def _experts(layer, block_e, n_used, xb, w_gu, b_gu, w_dn, b_dn, rows):
    n_blocks = block_e.shape[0]
    L, E = w_gu.shape[:2]
    blk = lambda b, be, nu: (jnp.minimum(b, nu[0] - 1), 0)
    wsel = lambda b, be, nu: (layer, be[b], 0, 0)
    grid_spec = pltpu.PrefetchScalarGridSpec(
        num_scalar_prefetch=2,
        grid=(n_blocks,),
        in_specs=[pl.BlockSpec((rows * ROW_CHUNKS, LANES), blk),
                  pl.BlockSpec((1, 1, D, 2 * D_FF), wsel),
                  pl.BlockSpec((1, 1, 1, 2 * D_FF), wsel),
                  pl.BlockSpec((1, 1, D_FF, D), wsel),
                  pl.BlockSpec((1, 1, 1, D), wsel)],
        out_specs=pl.BlockSpec((rows * ROW_CHUNKS, LANES), blk),
        scratch_shapes=[pltpu.VMEM((D, 2 * D_FF), BF16), pltpu.VMEM((D_FF, D), BF16)],
    )
    return pl.pallas_call(
        functools.partial(_expert_kernel, rows=rows),
        grid_spec=grid_spec,
        out_shape=jax.ShapeDtypeStruct(xb.shape, F32),
        compiler_params=_params(VMEM_LIMIT),
        name="moe_experts",
    )(block_e, n_used, xb, w_gu, b_gu.reshape(L, E, 1, 2 * D_FF), w_dn, b_dn.reshape(L, E, 1, D))


def _combine_kernel(dest_ref, yb_ref, gate_ref, x1_ref, mod_ref, g_ref, o_ref, buf_ref, sem, *, tn):
    def issue(j, _):
        for k in range(TOP_K):
            pltpu.make_async_copy(
                yb_ref.at[pl.ds(pl.multiple_of(dest_ref[j * TOP_K + k] * ROW_CHUNKS, ROW_CHUNKS), ROW_CHUNKS), :],
                buf_ref.at[k, pl.ds(pl.multiple_of(j * ROW_CHUNKS, ROW_CHUNKS), ROW_CHUNKS), :], sem).start()
        return 0

    lax.fori_loop(0, tn, issue, 0)

    def drain(j, _):
        for k in range(TOP_K):
            pltpu.make_async_copy(yb_ref.at[pl.ds(0, ROW_CHUNKS), :],
                                  buf_ref.at[k, pl.ds(0, ROW_CHUNKS), :], sem).wait()
        return 0

    lax.fori_loop(0, tn, drain, 0)

    gates = gate_ref[...]
    y = jnp.zeros((tn, D), F32)
    for k in range(TOP_K):
        yk = jnp.concatenate([buf_ref[k, pl.ds(c, tn, stride=ROW_CHUNKS), :] for c in range(ROW_CHUNKS)], axis=1)
        y = y + gates[:, k:k + 1] * yk
    mod = mod_ref[0]
    o_ref[0] = x1_ref[0] + mod[5:6] * _rms(y, g_ref[...])


def _combine(dest_flat, yb, gates, x1, mod, g):
    B, T, _ = x1.shape
    tn = min(TN_COMB, T)
    nt = T // tn
    grid_spec = pltpu.PrefetchScalarGridSpec(
        num_scalar_prefetch=0,
        grid=(B, nt),
        in_specs=[pl.BlockSpec((tn * TOP_K,), lambda b, t: (b * nt + t,), memory_space=pltpu.SMEM),
                  pl.BlockSpec(memory_space=pl.ANY),
                  pl.BlockSpec((tn, TOP_K), lambda b, t: (b * nt + t, 0)),
                  pl.BlockSpec((1, tn, D), lambda b, t: (b, t, 0)),
                  pl.BlockSpec((1, N_MOD, D), lambda b, t: (b, 0, 0)),
                  pl.BlockSpec((1, D), lambda b, t: (0, 0))],
        out_specs=pl.BlockSpec((1, tn, D), lambda b, t: (b, t, 0)),
        scratch_shapes=[pltpu.VMEM((TOP_K, tn * ROW_CHUNKS, LANES), F32), pltpu.SemaphoreType.DMA(())],
    )
    return pl.pallas_call(
        functools.partial(_combine_kernel, tn=tn),
        grid_spec=grid_spec,
        out_shape=jax.ShapeDtypeStruct((B, T, D), F32),
        name="moe_combine",
    )(dest_flat, yb, gates, x1, mod, g)


def _head_slabs(w):
    w = w.reshape(D, N_HEADS, HEAD_DIM)
    return jnp.pad(w, ((0, 0), (0, 0), (0, LANES - HEAD_DIM))).reshape(D, N_HEADS * LANES)


def _block_diag(w):
    nb = w.shape[0]
    return jnp.einsum('hij,hg->higj', w, jnp.eye(nb, dtype=w.dtype)).reshape(LRU_W, LRU_W)


def kernel(x, c, ada_w, ada_b, pre_mix_g, w_in, conv_w, conv_b, lru_wa, lru_ba, lru_wx, lru_bx, lru_lambda, s5_a_re, s5_a_im, s5_b_re, s5_b_im, s5_c_re, s5_c_im, s5_d, s5_log_dt, s5_glu_w, s5_glu_b, fox_fb, gn_lru, gn_s5, gn_attn, w_out, post_mix_g, pre_ffn_g, router_w, router_b, w_gu, b_gu, w_dn, b_dn, post_ffn_g):
    B, T, _ = x.shape
    N = B * T
    L = ada_w.shape[0]
    tc_s5 = min(TC_S5, T)
    n_blocks = (N * TOP_K) // EXP_ROWS + N_EXPERTS
    n_rows = n_blocks * EXP_ROWS
    mod_all = _modulation(c, ada_w, ada_b).reshape(L, B, N_MOD, D)

    for l in range(L):
        mod = mod_all[l]
        wl = w_in[l]
        w_aug = jnp.concatenate(
            [wl[:, :O_Q], _head_slabs(wl[:, O_Q:O_K]), _head_slabs(wl[:, O_K:O_V]), _head_slabs(wl[:, O_V:O_F]),
             jnp.pad(wl[:, O_F:], ((0, 0), (0, LANES - N_HEADS)))], axis=1).astype(BF16)
        fb = jnp.pad(fox_fb[l], (0, LANES - N_HEADS)).reshape(1, LANES)
        lx, lg, su, qa, ka, va = _in_proj(x, mod, pre_mix_g[l].reshape(1, D), w_aug, fb)

        yl = _lru_branch(lx, lg, conv_w[l], conv_b[l].reshape(1, LRU_W),
                         _block_diag(lru_wa[l]).astype(BF16), lru_ba[l].reshape(1, LRU_W),
                         _block_diag(lru_wx[l]).astype(BF16), lru_bx[l].reshape(1, LRU_W),
                         lru_lambda[l].reshape(1, LRU_W), gn_lru[l].reshape(1, LRU_W))

        tabs = _s5_tables(s5_a_re[l], s5_a_im[l], s5_b_re[l], s5_b_im[l], s5_c_re[l], s5_c_im[l], s5_log_dt[l], tc_s5)
        ys = _s5_branch(su, *tabs, s5_d[l].reshape(1, S5_W), s5_glu_w[l].astype(BF16),
                        s5_glu_b[l].reshape(1, S5_W), gn_s5[l].reshape(1, S5_W), tc_s5)

        ya = _attention(qa, ka, va)

        rw = jnp.pad(router_w[l], ((0, 0), (0, LANES - N_EXPERTS)))
        rwh = rw.astype(BF16)
        rwl = (rw - rwh.astype(F32)).astype(BF16)
        rb = jnp.pad(router_b[l], (0, LANES - N_EXPERTS)).reshape(1, LANES)
        x1, h_rows, idx, gates, rank, cnt = _out_proj(
            yl, ys, ya, x, mod, gn_attn[l].reshape(1, ATTN_W), w_out[l].astype(BF16),
            post_mix_g[l].reshape(1, D), pre_ffn_g[l].reshape(1, D), rwh, rwl, rb)

        counts = cnt[0, :N_EXPERTS]
        blocks_e = (counts + EXP_ROWS - 1) // EXP_ROWS
        padded = blocks_e * EXP_ROWS
        block_ends = jnp.cumsum(blocks_e)
        starts = (block_ends - blocks_e) * EXP_ROWS
        n_used = block_ends[-1:]
        bid = jnp.minimum(jnp.arange(n_blocks, dtype=I32), n_used[0] - 1)
        block_e = jnp.minimum(jnp.sum((bid[:, None] >= block_ends[None, :]).astype(I32), axis=1), N_EXPERTS - 1)

        dest, xb = _dispatch(starts.astype(I32), counts, padded.astype(I32),
                             idx.reshape(N * TOP_K), rank.reshape(N * TOP_K), h_rows, n_rows)
        yb = _experts(l, block_e, n_used.astype(I32), xb, w_gu, b_gu, w_dn, b_dn, EXP_ROWS)
        x = _combine(dest, yb, gates, x1, mod, post_ffn_g[l].reshape(1, D))
    return x
```

```python
import functools

import jax
import jax.numpy as jnp
from jax import lax
from jax.experimental import pallas as pl
from jax.experimental.pallas import tpu as pltpu

F32 = jnp.float32
BF16 = jnp.bfloat16
I32 = jnp.int32

D = 1024
LRU_W = 256
LRU_BLOCK_W = 64
CONV_W = 4
LRU_C = 8.0
S5_W = 256
S5_GROUP_W = 16
S5_GROUPS = 16
S5_STATE = 64
S5_STATES = S5_GROUPS * S5_STATE
HEAD_DIM = 64
ATTN_W = 512
N_HEADS = 8
O_LRU_G = 256
O_S5 = 512
O_Q = 768
O_K = O_Q + ATTN_W
O_V = O_K + ATTN_W
O_F = O_V + ATTN_W
N_EXPERTS = 32
TOP_K = 4
D_FF = 1024
SWIGLU_LIMIT = 7.0
SWIGLU_ALPHA = 1.702
N_MOD = 6
RMS_EPS = 1e-6

LANES = 128
SUBLANES = 8
VMEM_LIMIT = 56 * 1024 * 1024
ROW_CHUNKS = D // LANES

C_Q = 768
C_K = C_Q + N_HEADS * LANES
C_V = C_K + N_HEADS * LANES
C_F = C_V + N_HEADS * LANES
IN_COLS_PAD = C_F + LANES
L_A0, L_A1, L_A2, L_B0, L_B1, L_B2 = 64, 65, 66, 67, 68, 69

TM_IN = 256
TC_LRU = 256
TC_S5 = 128
TQ = 512
ATTN_HEADS_PER_STEP = 4
TM_OUT = 512
TN_ROWS = 1024
TN_COMB = 256
EXP_ROWS = 256
MAP_BLOCK = 1024
NEG_BIG = -1e30


def _rms(x, g):
    return x * lax.rsqrt(jnp.mean(x * x, axis=-1, keepdims=True) + RMS_EPS) * g


def _shift_rows(v, d, row, fill):
    return jnp.where(row >= d, pltpu.roll(v, d, 0), fill)


def _params(vmem=None):
    return pltpu.CompilerParams(vmem_limit_bytes=vmem) if vmem else None


def _mod_kernel(c_ref, w_ref, b_ref, o_ref):
    c = c_ref[...]
    s = c * jax.nn.sigmoid(c)
    o_ref[0] = jnp.dot(s.astype(BF16), w_ref[0].astype(BF16), preferred_element_type=F32) + b_ref[0]


def _modulation(c, ada_w, ada_b):
    L, _, W = ada_w.shape
    B = c.shape[0]
    return pl.pallas_call(
        _mod_kernel,
        grid=(L, W // D),
        in_specs=[pl.BlockSpec((B, D), lambda l, j: (0, 0)),
                  pl.BlockSpec((1, D, D), lambda l, j: (l, 0, j)),
                  pl.BlockSpec((1, 1, D), lambda l, j: (l, 0, j))],
        out_specs=pl.BlockSpec((1, B, D), lambda l, j: (l, 0, j)),
        out_shape=jax.ShapeDtypeStruct((L, B, W), F32),
        name="adaln_mod",
    )(c, ada_w, ada_b.reshape(L, 1, W))


def _in_proj_kernel(x_ref, mod_ref, g_ref, w_ref, fb_ref,
                    lx_ref, lg_ref, su_ref, q_ref, k_ref, v_ref, fcarry_ref, *, tm):
    @pl.when(pl.program_id(1) == 0)
    def _():
        fcarry_ref[...] = jnp.zeros_like(fcarry_ref)

    mod = mod_ref[0]
    h = _rms(x_ref[0], g_ref[...]) * (1.0 + mod[1:2]) + mod[0:1]
    p = jnp.dot(h.astype(BF16), w_ref[...], preferred_element_type=F32)
    lx_ref[0] = p[:, 0:O_LRU_G]
    lg_ref[0] = p[:, O_LRU_G:O_S5]
    su_ref[0] = p[:, O_S5:O_Q]

    z = p[:, C_F:C_F + LANES] + fb_ref[...]
    logf = jnp.minimum(z, 0.0) - jnp.log1p(jnp.exp(-jnp.abs(z)))
    row = lax.broadcasted_iota(I32, (tm, LANES), 0)
    d = 1
    while d < tm:
        logf = logf + _shift_rows(logf, d, row, 0.0)
        d *= 2
    fsum = logf + fcarry_ref[...]
    fcarry_ref[...] = fsum[tm - 1:tm, :]

    fexp = jnp.concatenate([jnp.broadcast_to(fsum[:, h:h + 1], (tm, LANES)) for h in range(N_HEADS)], axis=1)
    hi = fexp.astype(BF16).astype(F32)
    r1 = fexp - hi
    mid = r1.astype(BF16).astype(F32)
    lo = r1 - mid
    lane = lax.broadcasted_iota(I32, (tm, N_HEADS * LANES), 1) & (LANES - 1)
    pq = p[:, C_Q:C_K] * (HEAD_DIM ** -0.5)
    pk = p[:, C_K:C_V]
    pv = p[:, C_V:C_F]
    ones = (lane >= L_B0) & (lane <= L_B2)
    qa = jnp.where(lane == L_A0, hi, jnp.where(lane == L_A1, mid, jnp.where(lane == L_A2, lo,
                                                                            jnp.where(ones, 1.0, pq))))
    ones = (lane >= L_A0) & (lane <= L_A2)
    ka = jnp.where(lane == L_B0, -hi, jnp.where(lane == L_B1, -mid, jnp.where(lane == L_B2, -lo,
                                                                              jnp.where(ones, 1.0, pk))))
    va = jnp.where(lane == HEAD_DIM, 1.0, pv)
    for h in range(N_HEADS):
        sl = slice(h * LANES, (h + 1) * LANES)
        q_ref[0, h] = qa[:, sl].astype(BF16)
        k_ref[0, h] = ka[:, sl].astype(BF16)
        v_ref[0, h] = va[:, sl].astype(BF16)


def _in_proj(x, mod, g, w_aug, fb):
    B, T, _ = x.shape
    tm = min(TM_IN, T)
    grp = jax.ShapeDtypeStruct((B, T, 256), F32)
    head = jax.ShapeDtypeStruct((B, N_HEADS, T, LANES), BF16)
    grp_spec = pl.BlockSpec((1, tm, 256), lambda b, t: (b, t, 0))
    head_spec = pl.BlockSpec((1, N_HEADS, tm, LANES), lambda b, t: (b, 0, t, 0))
    return pl.pallas_call(
        functools.partial(_in_proj_kernel, tm=tm),
        grid=(B, T // tm),
        in_specs=[pl.BlockSpec((1, tm, D), lambda b, t: (b, t, 0)),
                  pl.BlockSpec((1, N_MOD, D), lambda b, t: (b, 0, 0)),
                  pl.BlockSpec((1, D), lambda b, t: (0, 0)),
                  pl.BlockSpec((D, IN_COLS_PAD), lambda b, t: (0, 0)),
                  pl.BlockSpec((1, LANES), lambda b, t: (0, 0))],
        out_specs=[grp_spec, grp_spec, grp_spec, head_spec, head_spec, head_spec],
        out_shape=[grp, grp, grp, head, head, head],
        scratch_shapes=[pltpu.VMEM((1, LANES), F32)],
        compiler_params=_params(VMEM_LIMIT),
        name="in_proj",
    )(x, mod, g, w_aug, fb)


def _lru_kernel(lx_ref, lg_ref, cw_ref, cb_ref, wa_ref, ba_ref, wx_ref, bx_ref, lam_ref, gn_ref,
                y_ref, tail_ref, hcarry_ref, *, tc):
    @pl.when(pl.program_id(1) == 0)
    def _():
        tail_ref[...] = jnp.zeros_like(tail_ref)
        hcarry_ref[...] = jnp.zeros_like(hcarry_ref)

    x = lx_ref[0]
    xcat = jnp.concatenate([tail_ref[...], x], axis=0)
    cw = cw_ref[...]
    xr = cb_ref[...]
    for j in range(CONV_W):
        d = CONV_W - 1 - j
        xs = x if d == 0 else pltpu.roll(xcat, d, 0)[SUBLANES:SUBLANES + tc]
        xr = xr + xs * cw[j:j + 1]
    tail_ref[...] = x[tc - SUBLANES:tc]

    xb = xr.astype(BF16)
    r = jax.nn.sigmoid(jnp.dot(xb, wa_ref[...], preferred_element_type=F32) + ba_ref[...])
    i = jax.nn.sigmoid(jnp.dot(xb, wx_ref[...], preferred_element_type=F32) + bx_ref[...])
    nl = -lam_ref[...]
    softplus = jnp.maximum(nl, 0.0) + jnp.log1p(jnp.exp(-jnp.abs(nl)))
    log_a = -LRU_C * r * softplus
    a = jnp.exp(log_a)
    th = jnp.tanh(log_a)
    u = xr * i * jnp.sqrt(-2.0 * th / (1.0 - th))

    row = lax.broadcasted_iota(I32, (tc, LRU_W), 0)
    d = 1
    while d < tc:
        u = u + a * _shift_rows(u, d, row, 0.0)
        a = a * _shift_rows(a, d, row, 1.0)
        d *= 2
    h = u + a * hcarry_ref[...]
    hcarry_ref[...] = h[tc - 1:tc]

    y = h * jax.nn.gelu(lg_ref[0])
    y_ref[0] = _rms(y, gn_ref[...]).astype(BF16)


def _lru_branch(lx, lg, conv_w, conv_b, wa, ba, wx, bx, lam, gn):
    B, T, _ = lx.shape
    tc = min(TC_LRU, T)
    tile = pl.BlockSpec((1, tc, LRU_W), lambda b, t: (b, t, 0))
    vec = pl.BlockSpec((1, LRU_W), lambda b, t: (0, 0))
    mat = pl.BlockSpec((LRU_W, LRU_W), lambda b, t: (0, 0))
    return pl.pallas_call(
        functools.partial(_lru_kernel, tc=tc),
        grid=(B, T // tc),
        in_specs=[tile, tile, pl.BlockSpec((CONV_W, LRU_W), lambda b, t: (0, 0)), vec, mat, vec, mat, vec, vec, vec],
        out_specs=tile,
        out_shape=jax.ShapeDtypeStruct((B, T, LRU_W), BF16),
        scratch_shapes=[pltpu.VMEM((SUBLANES, LRU_W), F32), pltpu.VMEM((1, LRU_W), F32)],
        name="rg_lru",
    )(lx, lg, conv_w, conv_b, wa, ba, wx, bx, lam, gn)


def _s5_kernel(u_ref, bcat_ref, apr_ref, api_ref, ptr_ref, pti_ref, ccat_ref, d_ref, gw_ref, gb_ref, gn_ref,
               y_ref, cr_ref, ci_ref, *, tc):
    @pl.when(pl.program_id(1) == 0)
    def _():
        cr_ref[...] = jnp.zeros_like(cr_ref)
        ci_ref[...] = jnp.zeros_like(ci_ref)

    u = u_ref[0]
    bu = jnp.dot(u.astype(BF16), bcat_ref[...], preferred_element_type=F32)
    xr = bu[:, :S5_STATES]
    xi = bu[:, S5_STATES:]
    row = lax.broadcasted_iota(I32, (tc, S5_STATES), 0)
    k, d = 0, 1
    while d < tc:
        ar = apr_ref[k:k + 1, :]
        ai = api_ref[k:k + 1, :]
        sr = _shift_rows(xr, d, row, 0.0)
        si = _shift_rows(xi, d, row, 0.0)
        xr, xi = xr + (ar * sr - ai * si), xi + (ar * si + ai * sr)
        k, d = k + 1, d * 2
    pr = ptr_ref[...]
    pi = pti_ref[...]
    cr = cr_ref[...]
    ci = ci_ref[...]
    xr, xi = xr + (pr * cr - pi * ci), xi + (pr * ci + pi * cr)
    cr_ref[...] = xr[tc - 1:tc]
    ci_ref[...] = xi[tc - 1:tc]

    xcat = jnp.concatenate([xr, xi], axis=1).astype(BF16)
    y = jnp.dot(xcat, ccat_ref[...], preferred_element_type=F32) + d_ref[...] * u
    y = jax.nn.gelu(y)
    y = y * jax.nn.sigmoid(jnp.dot(y.astype(BF16), gw_ref[...], preferred_element_type=F32) + gb_ref[...])
    y_ref[0] = _rms(y, gn_ref[...]).astype(BF16)


def _s5_branch(u, bcat, apow_re, apow_im, ptab_re, ptab_im, ccat, d, glu_w, glu_b, gn, tc):
    B, T, _ = u.shape
    tile = pl.BlockSpec((1, tc, S5_W), lambda b, t: (b, t, 0))
    vec = pl.BlockSpec((1, S5_W), lambda b, t: (0, 0))

    def full(a):
        return pl.BlockSpec(a.shape, lambda b, t: (0, 0))

    return pl.pallas_call(
        functools.partial(_s5_kernel, tc=tc),
        grid=(B, T // tc),
        in_specs=[tile, full(bcat), full(apow_re), full(apow_im), full(ptab_re), full(ptab_im), full(ccat),
                  vec, full(glu_w), vec, vec],
        out_specs=tile,
        out_shape=jax.ShapeDtypeStruct((B, T, S5_W), BF16),
        scratch_shapes=[pltpu.VMEM((1, S5_STATES), F32), pltpu.VMEM((1, S5_STATES), F32)],
        compiler_params=_params(VMEM_LIMIT),
        name="s5",
    )(u, bcat, apow_re, apow_im, ptab_re, ptab_im, ccat, d, glu_w, glu_b, gn)


def _s5_tables(a_re, a_im, b_re, b_im, c_re, c_im, log_dt, tc):
    dt = jnp.exp(log_dt)[:, None]
    mag = jnp.exp(a_re * dt)
    abar_re = mag * jnp.cos(a_im * dt)
    abar_im = mag * jnp.sin(a_im * dt)
    den = a_re * a_re + a_im * a_im
    num_re = abar_re - 1.0
    k_re = (num_re * a_re + abar_im * a_im) / den
    k_im = (abar_im * a_re - num_re * a_im) / den
    bbar_re = k_re[..., None] * b_re - k_im[..., None] * b_im
    bbar_im = k_re[..., None] * b_im + k_im[..., None] * b_re
    eye = jnp.eye(S5_GROUPS, dtype=F32)
    bd_re = jnp.einsum('gpc,gh->gchp', bbar_re, eye).reshape(S5_W, S5_STATES)
    bd_im = jnp.einsum('gpc,gh->gchp', bbar_im, eye).reshape(S5_W, S5_STATES)
    bcat = jnp.concatenate([bd_re, bd_im], axis=1).astype(BF16)
    cd_re = jnp.einsum('gcp,gh->gphc', c_re, eye).reshape(S5_STATES, S5_W)
    cd_im = jnp.einsum('gcp,gh->gphc', c_im, eye).reshape(S5_STATES, S5_W)
    ccat = jnp.concatenate([cd_re, -cd_im], axis=0).astype(BF16)
    ar = abar_re.reshape(1, S5_STATES)
    ai = abar_im.reshape(1, S5_STATES)
    pows_r, pows_i = [ar], [ai]
    tab_r, tab_i = ar, ai
    n = 1
    while n < tc:
        sr, si = pows_r[-1], pows_i[-1]
        tab_r, tab_i = (jnp.concatenate([tab_r, tab_r * sr - tab_i * si], axis=0),
                        jnp.concatenate([tab_i, tab_r * si + tab_i * sr], axis=0))
        pows_r.append(sr * sr - si * si)
        pows_i.append(2.0 * sr * si)
        n *= 2
    levels = len(pows_r) - 1
    pad = (-levels) % SUBLANES
    apow_re = jnp.concatenate(pows_r[:levels] + [jnp.zeros((pad, S5_STATES), F32)], axis=0)
    apow_im = jnp.concatenate(pows_i[:levels] + [jnp.zeros((pad, S5_STATES), F32)], axis=0)
    return bcat, apow_re, apow_im, tab_r, tab_i, ccat


def _attn_kernel(qt_ref, k_ref, vt_ref, o_ref, p_ref, *, tq):
    qi = pl.program_id(2)
    key = lax.broadcasted_iota(I32, (tq, tq), 0)
    qry = lax.broadcasted_iota(I32, (tq, tq), 1)
    heads = range(ATTN_HEADS_PER_STEP)
    qts =[qt_ref[0, hh] for hh in heads]

    def scores(hh, kj):
        k = k_ref[0, hh, pl.ds(pl.multiple_of(kj * tq, tq), tq), :]
        return jnp.dot(k, qts[hh], preferred_element_type=F32)

    def weighted_values(hh, kj, p):
        vt = vt_ref[0, hh, :, pl.ds(pl.multiple_of(kj * tq, tq), tq)]
        return jnp.dot(vt, p, preferred_element_type=F32)

    def softmax_step(s, m):
        m_new = jnp.maximum(m, jnp.max(s, axis=0, keepdims=True))
        return m_new, jnp.exp(m - m_new), jnp.exp(s - m_new).astype(BF16)

    p_ref[...] = jnp.zeros_like(p_ref)

    def body(kj, carry):
        ss = [scores(hh, kj) for hh in heads]
        pvs = [weighted_values(hh, jnp.maximum(kj - 1, 0), p_ref[hh]) for hh in heads]
        out = []
        for hh in heads:
            m, alpha, acc = carry[hh]
            acc = alpha * acc + pvs[hh]
            m, alpha, p = softmax_step(ss[hh], m)
            p_ref[hh] = p
            out.append((m, alpha, acc))
        return tuple(out)

    init = tuple((jnp.full((1, tq), NEG_BIG, F32), jnp.ones((1, tq), F32), jnp.zeros((LANES, tq), F32))
                 for _ in heads)
    carry = lax.fori_loop(0, qi, body, init)
    outs = []
    for hh in heads:
        m, alpha, acc = carry[hh]
        acc = alpha * acc + weighted_values(hh, jnp.maximum(qi - 1, 0), p_ref[hh])
        s = jnp.where(key <= qry, scores(hh, qi), NEG_BIG)
        m, alpha, p = softmax_step(s, m)
        acc = alpha * acc + weighted_values(hh, qi, p)
        out_t = acc / acc[HEAD_DIM:HEAD_DIM + 1, :]
        outs.append(out_t.T[:, :HEAD_DIM])
    o_ref[0] = jnp.concatenate(outs, axis=1)


def _attention(qa, ka, va):
    B, H, T, _ = qa.shape
    tq = min(TQ, T)
    hs = ATTN_HEADS_PER_STEP
    qt = jnp.swapaxes(qa, 2, 3)
    vt = jnp.swapaxes(va, 2, 3)
    return pl.pallas_call(
        functools.partial(_attn_kernel, tq=tq),
        grid=(B, H // hs, T // tq),
        in_specs=[pl.BlockSpec((1, hs, LANES, tq), lambda b, hp, qi: (b, hp, 0, qi)),
                  pl.BlockSpec((1, hs, T, LANES), lambda b, hp, qi: (b, hp, 0, 0)),
                  pl.BlockSpec((1, hs, LANES, T), lambda b, hp, qi: (b, hp, 0, 0))],
        out_specs=pl.BlockSpec((1, tq, hs * HEAD_DIM), lambda b, hp, qi: (b, qi, hp)),
        out_shape=jax.ShapeDtypeStruct((B, T, ATTN_W), F32),
        scratch_shapes=[pltpu.VMEM((hs, tq, tq), BF16)],
        compiler_params=_params(VMEM_LIMIT),
        name="fox_attention",
    )(qt, ka, vt)


def _out_proj_kernel(yl_ref, ys_ref, ya_ref, x_ref, mod_ref, gna_ref, wout_ref, pmg_ref, pfg_ref,
                     rwh_ref, rwl_ref, rb_ref,
                     x1_ref, h2_ref, idx_ref, gate_ref, rank_ref, cnt_ref, carry_ref, *, tm):
    @pl.when((pl.program_id(0) == 0) & (pl.program_id(1) == 0))
    def _():
        carry_ref[...] = jnp.zeros_like(carry_ref)

    mod = mod_ref[0]
    ya = _rms(ya_ref[0], gna_ref[...])
    ycat = jnp.concatenate([yl_ref[0], ys_ref[0], ya.astype(BF16)], axis=1)
    y = jnp.dot(ycat, wout_ref[...], preferred_element_type=F32)
    x1 = x_ref[0] + mod[2:3] * _rms(y, pmg_ref[...])
    x1_ref[0] = x1
    h2 = _rms(x1, pfg_ref[...]) * (1.0 + mod[4:5]) + mod[3:4]
    for c in range(ROW_CHUNKS):
        h2_ref[pl.ds(c, tm, stride=ROW_CHUNKS), :] = h2[:, c * LANES:(c + 1) * LANES]

    hh = h2.astype(BF16)
    hl = (h2 - hh.astype(F32)).astype(BF16)
    rwh = rwh_ref[...]
    logits = (jnp.dot(hh, rwh, preferred_element_type=F32) + jnp.dot(hl, rwh, preferred_element_type=F32)
              + jnp.dot(hh, rwl_ref[...], preferred_element_type=F32) + rb_ref[...])
    lane = lax.broadcasted_iota(I32, (tm, LANES), 1)
    lane_f = lane.astype(F32)
    work = jnp.where(lane < N_EXPERTS, logits, NEG_BIG)
    vals, idxs = [], []
    for _ in range(TOP_K):
        mx = jnp.max(work, axis=1, keepdims=True)
        ik = jnp.min(jnp.where(work == mx, lane_f, float(LANES)), axis=1, keepdims=True)
        vals.append(mx)
        idxs.append(ik)
        work = jnp.where(lane_f == ik, 2.0 * NEG_BIG, work)
    es = [jnp.exp(v - vals[0]) for v in vals]
    den = es[0] + es[1] + es[2] + es[3]

    onehot = jnp.zeros((tm, LANES), F32)
    for ik in idxs:
        onehot = onehot + jnp.where(lane_f == ik, 1.0, 0.0)
    r_i = lax.broadcasted_iota(I32, (tm, tm), 0)
    c_i = lax.broadcasted_iota(I32, (tm, tm), 1)
    below = jnp.where(c_i < r_i, 1.0, 0.0).astype(BF16)
    prior = jnp.dot(below, onehot.astype(BF16), preferred_element_type=F32) + carry_ref[...]
    carry = carry_ref[...] + jnp.sum(onehot, axis=0, keepdims=True)
    carry_ref[...] = carry
    cnt_ref[...] = carry.astype(I32)

    idx_full = jnp.zeros((tm, LANES), F32)
    gate_full = jnp.zeros((tm, LANES), F32)
    rank_full = jnp.zeros((tm, LANES), F32)
    for k in range(TOP_K):
        rk = jnp.sum(jnp.where(lane_f == idxs[k], prior, 0.0), axis=1, keepdims=True)
        idx_full = jnp.where(lane == k, idxs[k], idx_full)
        gate_full = jnp.where(lane == k, es[k] / den, gate_full)
        rank_full = jnp.where(lane == k, rk, rank_full)
    idx_ref[...] = idx_full[:, :TOP_K].astype(I32)
    gate_ref[...] = gate_full[:, :TOP_K]
    rank_ref[...] = rank_full[:, :TOP_K].astype(I32)


def _out_proj(yl, ys, ya, x, mod, gna, wout, pmg, pfg, rwh, rwl, rb):
    B, T, _ = x.shape
    tm = min(TM_OUT, T)
    nt = T // tm
    N = B * T

    def vec(n):
        return pl.BlockSpec((1, n), lambda b, t: (0, 0))

    tok = lambda b, t: (b * nt + t, 0)
    return pl.pallas_call(
        functools.partial(_out_proj_kernel, tm=tm),
        grid=(B, nt),
        in_specs=[pl.BlockSpec((1, tm, LRU_W), lambda b, t: (b, t, 0)),
                  pl.BlockSpec((1, tm, S5_W), lambda b, t: (b, t, 0)),
                  pl.BlockSpec((1, tm, ATTN_W), lambda b, t: (b, t, 0)),
                  pl.BlockSpec((1, tm, D), lambda b, t: (b, t, 0)),
                  pl.BlockSpec((1, N_MOD, D), lambda b, t: (b, 0, 0)),
                  vec(ATTN_W),
                  pl.BlockSpec((D, D), lambda b, t: (0, 0)),
                  vec(D), vec(D),
                  pl.BlockSpec((D, LANES), lambda b, t: (0, 0)),
                  pl.BlockSpec((D, LANES), lambda b, t: (0, 0)),
                  vec(LANES)],
        out_specs=[pl.BlockSpec((1, tm, D), lambda b, t: (b, t, 0)),
                   pl.BlockSpec((tm * ROW_CHUNKS, LANES), tok),
                   pl.BlockSpec((tm, TOP_K), tok),
                   pl.BlockSpec((tm, TOP_K), tok),
                   pl.BlockSpec((tm, TOP_K), tok),
                   pl.BlockSpec((1, LANES), lambda b, t: (0, 0))],
        out_shape=[jax.ShapeDtypeStruct((B, T, D), F32),
                   jax.ShapeDtypeStruct((N * ROW_CHUNKS, LANES), F32),
                   jax.ShapeDtypeStruct((N, TOP_K), I32),
                   jax.ShapeDtypeStruct((N, TOP_K), F32),
                   jax.ShapeDtypeStruct((N, TOP_K), I32),
                   jax.ShapeDtypeStruct((1, LANES), I32)],
        scratch_shapes=[pltpu.VMEM((1, LANES), F32)],
        compiler_params=_params(VMEM_LIMIT),
        name="out_proj_router",
    )(yl, ys, ya, x, mod, gna, wout, pmg, pfg, rwh, rwl, rb)


def _row_copy(src, src_row, dst, dst_row, sem):
    return pltpu.make_async_copy(src.at[pl.ds(pl.multiple_of(src_row * ROW_CHUNKS, ROW_CHUNKS), ROW_CHUNKS), :],
                                 dst.at[pl.ds(pl.multiple_of(dst_row * ROW_CHUNKS, ROW_CHUNKS), ROW_CHUNKS), :],
                                 sem)


def _dispatch_kernel(start_ref, cnt_ref, pad_ref, idx_ref, rank_ref, h_ref, dest_ref, xb_ref, sem, *, tn):
    i = pl.program_id(0)

    def issue(j, _):
        for k in range(TOP_K):
            a = j * TOP_K + k
            dst = start_ref[idx_ref[a]] + rank_ref[a]
            dest_ref[a] = dst
            _row_copy(h_ref, j, xb_ref, dst, sem).start()
        return 0

    lax.fori_loop(0, tn, issue, 0)

    def drain(j, _):
        for k in range(TOP_K):
            _row_copy(h_ref, 0, xb_ref, 0, sem).wait()
        return 0

    lax.fori_loop(0, tn, drain, 0)

    @pl.when(i == pl.num_programs(0) - 1)
    def _():
        def per_expert(e, _):
            lo = start_ref[e] + cnt_ref[e]
            hi = start_ref[e] + pad_ref[e]

            def fill(r, _):
                _row_copy(h_ref, 0, xb_ref, r, sem).start()
                return 0

            lax.fori_loop(lo, hi, fill, 0)

            def fill_wait(r, _):
                _row_copy(h_ref, 0, xb_ref, r, sem).wait()
                return 0

            lax.fori_loop(lo, hi, fill_wait, 0)
            return 0

        lax.fori_loop(0, N_EXPERTS, per_expert, 0)


def _dispatch(starts, counts, padded, idx_flat, rank_flat, h_rows, n_rows):
    NK = idx_flat.shape[0]
    N = NK // TOP_K
    tn = min(TN_ROWS, N)
    smem_blk = pl.BlockSpec((tn * TOP_K,), lambda i, *_: (i,), memory_space=pltpu.SMEM)
    grid_spec = pltpu.PrefetchScalarGridSpec(
        num_scalar_prefetch=3,
        grid=(N // tn,),
        in_specs=[smem_blk, smem_blk, pl.BlockSpec((tn * ROW_CHUNKS, LANES), lambda i, *_: (i, 0))],
        out_specs=[smem_blk, pl.BlockSpec(memory_space=pl.ANY)],
        scratch_shapes=[pltpu.SemaphoreType.DMA(())],
    )
    return pl.pallas_call(
        functools.partial(_dispatch_kernel, tn=tn),
        grid_spec=grid_spec,
        out_shape=[jax.ShapeDtypeStruct((NK,), I32),
                   jax.ShapeDtypeStruct((n_rows * ROW_CHUNKS, LANES), F32)],
        name="moe_dispatch",
    )(starts, counts, padded, idx_flat, rank_flat, h_rows)


def _expert_kernel(be_ref, nused_ref, x_ref, wgu_ref, bgu_ref, wdn_ref, bdn_ref, y_ref, wgu_s, wdn_s, *, rows):
    b = pl.program_id(0)
    prev = be_ref[jnp.maximum(b - 1, 0)]

    @pl.when((b == 0) | (be_ref[b] != prev))
    def _():
        wgu_s[...] = wgu_ref[0, 0].astype(BF16)
        wdn_s[...] = wdn_ref[0, 0].astype(BF16)

    @pl.when(b < nused_ref[0])
    def _():
        x = jnp.concatenate([x_ref[pl.ds(c, rows, stride=ROW_CHUNKS), :] for c in range(ROW_CHUNKS)], axis=1)
        gu = jnp.dot(x.astype(BF16), wgu_s[...], preferred_element_type=F32) + bgu_ref[0, 0]
        g = jnp.minimum(gu[:, :D_FF], SWIGLU_LIMIT)
        up = jnp.clip(gu[:, D_FF:], -SWIGLU_LIMIT, SWIGLU_LIMIT)
        act = (up + 1.0) * (g * jax.nn.sigmoid(SWIGLU_ALPHA * g))
        y = jnp.dot(act.astype(BF16), wdn_s[...], preferred_element_type=F32) + bdn_ref[0, 0]
        for c in range(ROW_CHUNKS):
            y_ref[pl.ds(c, rows, stride=ROW_CHUNKS), :] = y[:, c * LANES:(c + 1) * LANES]


def _experts(layer, block_e, n_used, xb, w_gu, b_gu, w_dn, b_dn, rows):
    n_blocks = block_e.shape[0]
    L, E = w_gu.shape[:2]
    blk = lambda b, be, nu: (jnp.minimum(b, nu[0] - 1), 0)
    wsel = lambda b, be, nu: (layer, be[b], 0, 0)
    grid_spec = pltpu.PrefetchScalarGridSpec(
        num_scalar_prefetch=2,
        grid=(n_blocks,),
        in_specs=[pl.BlockSpec((rows * ROW_CHUNKS, LANES), blk),
                  pl.BlockSpec((1, 1, D, 2 * D_FF), wsel),
                  pl.BlockSpec((1, 1, 1, 2 * D_FF), wsel),
                  pl.BlockSpec((1, 1, D_FF, D), wsel),
                  pl.BlockSpec((1, 1, 1, D), wsel)],
        out_specs=pl.BlockSpec((rows * ROW_CHUNKS, LANES), blk),
        scratch_shapes=[pltpu.VMEM((D, 2 * D_FF), BF16), pltpu.VMEM((D_FF, D), BF16)],
    )
    return pl.pallas_call(
        functools.partial(_expert_kernel, rows=rows),
        grid_spec=grid_spec,
        out_shape=jax.ShapeDtypeStruct(xb.shape, F32),
        compiler_params=_params(VMEM_LIMIT),
        name="moe_experts",
    )(block_e, n_used, xb, w_gu, b_gu.reshape(L, E, 1, 2 * D_FF), w_dn, b_dn.reshape(L, E, 1, D))


def _combine_kernel(dest_ref, yb_ref, gate_ref, x1_ref, mod_ref, g_ref, o_ref, buf_ref, sem, *, tn):
    def issue(j, _):
        for k in range(TOP_K):
            pltpu.make_async_copy(
                yb_ref.at[pl.ds(pl.multiple_of(dest_ref[j * TOP_K + k] * ROW_CHUNKS, ROW_CHUNKS), ROW_CHUNKS), :],
                buf_ref.at[k, pl.ds(pl.multiple_of(j * ROW_CHUNKS, ROW_CHUNKS), ROW_CHUNKS), :], sem).start()
        return 0

    lax.fori_loop(0, tn, issue, 0)

    def drain(j, _):
        for k in range(TOP_K):
            pltpu.make_async_copy(yb_ref.at[pl.ds(0, ROW_CHUNKS), :],
                                  buf_ref.at[k, pl.ds(0, ROW_CHUNKS), :], sem).wait()
        return 0

    lax.fori_loop(0, tn, drain, 0)

    gates = gate_ref[...]
    y = jnp.zeros((tn, D), F32)
    for k in range(TOP_K):
        yk = jnp.concatenate([buf_ref[k, pl.ds(c, tn, stride=ROW_CHUNKS), :] for c in range(ROW_CHUNKS)], axis=1)
        y = y + gates[:, k:k + 1] * yk
    mod = mod_ref[0]
    o_ref[0] = x1_ref[0] + mod[5:6] * _rms(y, g_ref[...])


def _combine(dest_flat, yb, gates, x1, mod, g):
    B, T, _ = x1.shape
    tn = min(TN_COMB, T)
    nt = T // tn
    grid_spec = pltpu.PrefetchScalarGridSpec(
        num_scalar_prefetch=0,
        grid=(B, nt),
        in_specs=[pl.BlockSpec((tn * TOP_K,), lambda b, t: (b * nt + t,), memory_space=pltpu.SMEM),
                  pl.BlockSpec(memory_space=pl.ANY),
                  pl.BlockSpec((tn, TOP_K), lambda b, t: (b * nt + t, 0)),
                  pl.BlockSpec((1, tn, D), lambda b, t: (b, t, 0)),
                  pl.BlockSpec((1, N_MOD, D), lambda b, t: (b, 0, 0)),
                  pl.BlockSpec((1, D), lambda b, t: (0, 0))],
        out_specs=pl.BlockSpec((1, tn, D), lambda b, t: (b, t, 0)),
        scratch_shapes=[pltpu.VMEM((TOP_K, tn * ROW_CHUNKS, LANES), F32), pltpu.SemaphoreType.DMA(())],
    )
    return pl.pallas_call(
        functools.partial(_combine_kernel, tn=tn),
        grid_spec=grid_spec,
        out_shape=jax.ShapeDtypeStruct((B, T, D), F32),
        name="moe_combine",
    )(dest_flat, yb, gates, x1, mod, g)


def _fused_expert_kernel(be_ref, nused_ref, src_ref, dst_ref, h_ref, wgu_ref, bgu_ref, wdn_ref, bdn_ref, y_ref,
                         xbuf, ybuf, wgu_s, wdn_s, gsem, ssem, *, rows, n_blocks, n_assign):
    b = pl.program_id(0)
    nu = nused_ref[0]
    slot = lax.rem(b, 2)
    other = 1 - slot
    src_off = lax.rem(jnp.minimum(b + 1, n_blocks - 1) * rows, MAP_BLOCK)
    dst_off = lax.rem(jnp.maximum(b - 1, 0) * rows, MAP_BLOCK)

    def gather(i, tok, buf_slot):
        return pltpu.make_async_copy(
            h_ref.at[pl.ds(pl.multiple_of(tok * ROW_CHUNKS, ROW_CHUNKS), ROW_CHUNKS), :],
            xbuf.at[buf_slot, pl.ds(i * ROW_CHUNKS, ROW_CHUNKS), :], gsem.at[buf_slot])

    def scatter(i, dst, buf_slot):
        return pltpu.make_async_copy(
            ybuf.at[buf_slot, pl.ds(i * ROW_CHUNKS, ROW_CHUNKS), :],
            y_ref.at[pl.ds(pl.multiple_of(dst * ROW_CHUNKS, ROW_CHUNKS), ROW_CHUNKS), :], ssem)

    @pl.when(b == 0)
    def _():
        ybuf[1] = jnp.zeros(ybuf.shape[1:], F32)
        for i in range(rows):
            gather(i, src_ref[i], 0).start()

    @pl.when((b >= 1) & (b <= nu))
    def _():
        for i in range(rows):
            scatter(i, 0, 0).wait()

    @pl.when(b <= nu)
    def _():
        for i in range(rows):
            gather(i, 0, slot).wait()

    prev = be_ref[jnp.maximum(b - 1, 0)]

    @pl.when((b == 0) | (be_ref[b] != prev))
    def _():
        wgu_s[...] = wgu_ref[0, 0].astype(BF16)
        wdn_s[...] = wdn_ref[0, 0].astype(BF16)

    @pl.when(b < nu)
    def _():
        for i in range(rows):
            gather(i, src_ref[src_off + i], other).start()
        for i in range(rows):
            dst = jnp.where(b == 0, n_assign + i, dst_ref[dst_off + i])
            scatter(i, dst, other).start()
        x = jnp.concatenate([xbuf[slot, pl.ds(c, rows, stride=ROW_CHUNKS), :] for c in range(ROW_CHUNKS)], axis=1)
        gu = jnp.dot(x.astype(BF16), wgu_s[...], preferred_element_type=F32) + bgu_ref[0, 0]
        g = jnp.minimum(gu[:, :D_FF], SWIGLU_LIMIT)
        up = jnp.clip(gu[:, D_FF:], -SWIGLU_LIMIT, SWIGLU_LIMIT)
        act = (up + 1.0) * (g * jax.nn.sigmoid(SWIGLU_ALPHA * g))
        y = jnp.dot(act.astype(BF16), wdn_s[...], preferred_element_type=F32) + bdn_ref[0, 0]
        for c in range(ROW_CHUNKS):
            ybuf[slot, pl.ds(c, rows, stride=ROW_CHUNKS), :] = y[:, c * LANES:(c + 1) * LANES]

    @pl.when(b == nu)
    def _():
        for i in range(rows):
            scatter(i, dst_ref[dst_off + i], other).start()
        for i in range(rows):
            scatter(i, 0, 0).wait()


def _fused_experts(layer, block_e, n_used, src_tok, dst_row, h_rows, w_gu, b_gu, w_dn, b_dn, rows, n_assign):
    n_blocks = block_e.shape[0]
    L, E = w_gu.shape[:2]
    wsel = lambda b, be, nu: (layer, be[b], 0, 0)
    grid_spec = pltpu.PrefetchScalarGridSpec(
        num_scalar_prefetch=2,
        grid=(n_blocks,),
        in_specs=[pl.BlockSpec((MAP_BLOCK,), lambda b, be, nu: (jnp.minimum(b + 1, n_blocks - 1) * rows // MAP_BLOCK,),
                               memory_space=pltpu.SMEM),
                  pl.BlockSpec((MAP_BLOCK,), lambda b, be, nu: (jnp.maximum(b - 1, 0) * rows // MAP_BLOCK,),
                               memory_space=pltpu.SMEM),
                  pl.BlockSpec(memory_space=pl.ANY),
                  pl.BlockSpec((1, 1, D, 2 * D_FF), wsel),
                  pl.BlockSpec((1, 1, 1, 2 * D_FF), wsel),
                  pl.BlockSpec((1, 1, D_FF, D), wsel),
                  pl.BlockSpec((1, 1, 1, D), wsel)],
        out_specs=pl.BlockSpec(memory_space=pl.ANY),
        scratch_shapes=[pltpu.VMEM((2, rows * ROW_CHUNKS, LANES), F32),
                        pltpu.VMEM((2, rows * ROW_CHUNKS, LANES), F32),
                        pltpu.VMEM((D, 2 * D_FF), BF16), pltpu.VMEM((D_FF, D), BF16),
                        pltpu.SemaphoreType.DMA((2,)), pltpu.SemaphoreType.DMA(())],
    )
    return pl.pallas_call(
        functools.partial(_fused_expert_kernel, rows=rows, n_blocks=n_blocks, n_assign=n_assign),
        grid_spec=grid_spec,
        out_shape=jax.ShapeDtypeStruct(((n_assign + 2 * rows) * ROW_CHUNKS, LANES), F32),
        compiler_params=_params(VMEM_LIMIT),
        name="moe_experts",
    )(block_e, n_used, src_tok, dst_row, h_rows, w_gu, b_gu.reshape(L, E, 1, 2 * D_FF), w_dn,
      b_dn.reshape(L, E, 1, D))


def _dense_combine_kernel(y_ref, gate_ref, x1_ref, mod_ref, g_ref, o_ref, *, tn):
    gates = gate_ref[...]
    y = jnp.zeros((tn, D), F32)
    for k in range(TOP_K):
        yk = jnp.concatenate(
            [y_ref[pl.ds(k * ROW_CHUNKS + c, tn, stride=TOP_K * ROW_CHUNKS), :] for c in range(ROW_CHUNKS)], axis=1)
        y = y + gates[:, k:k + 1] * yk
    mod = mod_ref[0]
    o_ref[0] = x1_ref[0] + mod[5:6] * _rms(y, g_ref[...])


def _dense_combine(y_rows, gates, x1, mod, g):
    B, T, _ = x1.shape
    tn = min(TN_COMB, T)
    nt = T // tn
    return pl.pallas_call(
        functools.partial(_dense_combine_kernel, tn=tn),
        grid=(B, nt),
        in_specs=[pl.BlockSpec((tn * TOP_K * ROW_CHUNKS, LANES), lambda b, t: (b * nt + t, 0)),
                  pl.BlockSpec((tn, TOP_K), lambda b, t: (b * nt + t, 0)),
                  pl.BlockSpec((1, tn, D), lambda b, t: (b, t, 0)),
                  pl.BlockSpec((1, N_MOD, D), lambda b, t: (b, 0, 0)),
                  pl.BlockSpec((1, D), lambda b, t: (0, 0))],
        out_specs=pl.BlockSpec((1, tn, D), lambda b, t: (b, t, 0)),
        out_shape=jax.ShapeDtypeStruct((B, T, D), F32),
        name="moe_combine",
    )(y_rows, gates, x1, mod, g)


def _head_slabs(w):
    w = w.reshape(D, N_HEADS, HEAD_DIM)
    return jnp.pad(w, ((0, 0), (0, 0), (0, LANES - HEAD_DIM))).reshape(D, N_HEADS * LANES)


def _block_diag(w):
    nb = w.shape[0]
    return jnp.einsum('hij,hg->higj', w, jnp.eye(nb, dtype=w.dtype)).reshape(LRU_W, LRU_W)


def kernel(x, c, ada_w, ada_b, pre_mix_g, w_in, conv_w, conv_b, lru_wa, lru_ba, lru_wx, lru_bx, lru_lambda, s5_a_re, s5_a_im, s5_b_re, s5_b_im, s5_c_re, s5_c_im, s5_d, s5_log_dt, s5_glu_w, s5_glu_b, fox_fb, gn_lru, gn_s5, gn_attn, w_out, post_mix_g, pre_ffn_g, router_w, router_b, w_gu, b_gu, w_dn, b_dn, post_ffn_g):
    B, T, _ = x.shape
    N = B * T
    L = ada_w.shape[0]
    tc_s5 = min(TC_S5, T)
    n_blocks = (N * TOP_K) // EXP_ROWS + N_EXPERTS
    n_rows = n_blocks * EXP_ROWS
    mod_all = _modulation(c, ada_w, ada_b).reshape(L, B, N_MOD, D)

    for l in range(L):
        mod = mod_all[l]
        wl = w_in[l]
        w_aug = jnp.concatenate(
            [wl[:, :O_Q], _head_slabs(wl[:, O_Q:O_K]), _head_slabs(wl[:, O_K:O_V]), _head_slabs(wl[:, O_V:O_F]),
             jnp.pad(wl[:, O_F:], ((0, 0), (0, LANES - N_HEADS)))], axis=1).astype(BF16)
        fb = jnp.pad(fox_fb[l], (0, LANES - N_HEADS)).reshape(1, LANES)
        lx, lg, su, qa, ka, va = _in_proj(x, mod, pre_mix_g[l].reshape(1, D), w_aug, fb)

        yl = _lru_branch(lx, lg, conv_w[l], conv_b[l].reshape(1, LRU_W),
                         _block_diag(lru_wa[l]).astype(BF16), lru_ba[l].reshape(1, LRU_W),
                         _block_diag(lru_wx[l]).astype(BF16), lru_bx[l].reshape(1, LRU_W),
                         lru_lambda[l].reshape(1, LRU_W), gn_lru[l].reshape(1, LRU_W))

        tabs = _s5_tables(s5_a_re[l], s5_a_im[l], s5_b_re[l], s5_b_im[l], s5_c_re[l], s5_c_im[l], s5_log_dt[l], tc_s5)
        ys = _s5_branch(su, *tabs, s5_d[l].reshape(1, S5_W), s5_glu_w[l].astype(BF16),
                        s5_glu_b[l].reshape(1, S5_W), gn_s5[l].reshape(1, S5_W), tc_s5)

        ya = _attention(qa, ka, va)

        rw = jnp.pad(router_w[l], ((0, 0), (0, LANES - N_EXPERTS)))
        rwh = rw.astype(BF16)
        rwl = (rw - rwh.astype(F32)).astype(BF16)
        rb = jnp.pad(router_b[l], (0, LANES - N_EXPERTS)).reshape(1, LANES)
        x1, h_rows, idx, gates, rank, cnt = _out_proj(
            yl, ys, ya, x, mod, gn_attn[l].reshape(1, ATTN_W), w_out[l].astype(BF16),
            post_mix_g[l].reshape(1, D), pre_ffn_g[l].reshape(1, D), rwh, rwl, rb)

        counts = cnt[0, :N_EXPERTS]
        blocks_e = (counts + EXP_ROWS - 1) // EXP_ROWS
        padded = blocks_e * EXP_ROWS
        block_ends = jnp.cumsum(blocks_e)
        starts = (block_ends - blocks_e) * EXP_ROWS
        n_used = block_ends[-1:]
        bid = jnp.minimum(jnp.arange(n_blocks, dtype=I32), n_used[0] - 1)
        block_e = jnp.minimum(jnp.sum((bid[:, None] >= block_ends[None, :]).astype(I32), axis=1), N_EXPERTS - 1)

        n_assign = N * TOP_K
        dest = (jnp.take(starts.astype(I32), idx) + rank).reshape(n_assign)
        row_assign = jnp.full((n_rows,), n_assign, I32).at[dest].set(
            jnp.arange(n_assign, dtype=I32), unique_indices=True)
        is_real = row_assign < n_assign
        src_tok = jnp.where(is_real, row_assign // TOP_K, 0)
        spare = n_assign + jnp.arange(n_rows, dtype=I32) % (2 * EXP_ROWS)
        dst_row = jnp.where(is_real, row_assign, spare)

        y_rows = _fused_experts(l, block_e, n_used.astype(I32), src_tok, dst_row, h_rows,
                                w_gu, b_gu, w_dn, b_dn, EXP_ROWS, n_assign)
        x = _dense_combine(y_rows, gates, x1, mod, post_ffn_g[l].reshape(1, D))
    return x
```

```python
import functools

import jax
import jax.numpy as jnp
from jax import lax
from jax.experimental import pallas as pl
from jax.experimental.pallas import tpu as pltpu

F32 = jnp.float32
BF16 = jnp.bfloat16
I32 = jnp.int32

D = 1024
LRU_W = 256
LRU_BLOCK_W = 64
CONV_W = 4
LRU_C = 8.0
S5_W = 256
S5_GROUP_W = 16
S5_GROUPS = 16
S5_STATE = 64
S5_STATES = S5_GROUPS * S5_STATE
HEAD_DIM = 64
ATTN_W = 512
N_HEADS = 8
O_LRU_G = 256
O_S5 = 512
O_Q = 768
O_K = O_Q + ATTN_W
O_V = O_K + ATTN_W
O_F = O_V + ATTN_W
N_EXPERTS = 32
TOP_K = 4
D_FF = 1024
SWIGLU_LIMIT = 7.0
SWIGLU_ALPHA = 1.702
N_MOD = 6
RMS_EPS = 1e-6

LANES = 128
SUBLANES = 8
VMEM_LIMIT = 56 * 1024 * 1024
ROW_CHUNKS = D // LANES

C_Q = 768
C_K = C_Q + N_HEADS * LANES
C_V = C_K + N_HEADS * LANES
C_F = C_V + N_HEADS * LANES
IN_COLS_PAD = C_F + LANES
L_A0, L_A1, L_A2, L_B0, L_B1, L_B2 = 64, 65, 66, 67, 68, 69

TM_IN = 256
TC_LRU = 256
TC_S5 = 128
TQ = 512
ATTN_HEADS_PER_STEP = 4
TM_OUT = 512
TN_ROWS = 1024
TN_COMB = 256
EXP_ROWS = 256
MAP_BLOCK = 1024
NEG_BIG = -1e30


def _rms(x, g):
    return x * lax.rsqrt(jnp.mean(x * x, axis=-1, keepdims=True) + RMS_EPS) * g


def _shift_rows(v, d, row, fill):
    return jnp.where(row >= d, pltpu.roll(v, d, 0), fill)


def _params(vmem=None):
    return pltpu.CompilerParams(vmem_limit_bytes=vmem) if vmem else None


def _mod_kernel(c_ref, w_ref, b_ref, o_ref):
    c = c_ref[...]
    s = c * jax.nn.sigmoid(c)
    o_ref[0] = jnp.dot(s.astype(BF16), w_ref[0].astype(BF16), preferred_element_type=F32) + b_ref[0]


def _modulation(c, ada_w, ada_b):
    L, _, W = ada_w.shape
    B = c.shape[0]
    return pl.pallas_call(
        _mod_kernel,
        grid=(L, W // D),
        in_specs=[pl.BlockSpec((B, D), lambda l, j: (0, 0)),
                  pl.BlockSpec((1, D, D), lambda l, j: (l, 0, j)),
                  pl.BlockSpec((1, 1, D), lambda l, j: (l, 0, j))],
        out_specs=pl.BlockSpec((1, B, D), lambda l, j: (l, 0, j)),
        out_shape=jax.ShapeDtypeStruct((L, B, W), F32),
        name="adaln_mod",
    )(c, ada_w, ada_b.reshape(L, 1, W))


def _in_proj_kernel(x_ref, mod_ref, g_ref, w_ref, fb_ref,
                    lx_ref, lg_ref, su_ref, q_ref, k_ref, v_ref, fcarry_ref, *, tm):
    @pl.when(pl.program_id(1) == 0)
    def _():
        fcarry_ref[...] = jnp.zeros_like(fcarry_ref)

    mod = mod_ref[0]
    h = _rms(x_ref[0], g_ref[...]) * (1.0 + mod[1:2]) + mod[0:1]
    p = jnp.dot(h.astype(BF16), w_ref[...], preferred_element_type=F32)
    lx_ref[0] = p[:, 0:O_LRU_G]
    lg_ref[0] = p[:, O_LRU_G:O_S5]
    su_ref[0] = p[:, O_S5:O_Q]

    z = p[:, C_F:C_F + LANES] + fb_ref[...]
    logf = jnp.minimum(z, 0.0) - jnp.log1p(jnp.exp(-jnp.abs(z)))
    row = lax.broadcasted_iota(I32, (tm, LANES), 0)
    d = 1
    while d < tm:
        logf = logf + _shift_rows(logf, d, row, 0.0)
        d *= 2
    fsum = logf + fcarry_ref[...]
    fcarry_ref[...] = fsum[tm - 1:tm, :]

    fexp = jnp.concatenate([jnp.broadcast_to(fsum[:, h:h + 1], (tm, LANES)) for h in range(N_HEADS)], axis=1)
    hi = fexp.astype(BF16).astype(F32)
    r1 = fexp - hi
    mid = r1.astype(BF16).astype(F32)
    lo = r1 - mid
    lane = lax.broadcasted_iota(I32, (tm, N_HEADS * LANES), 1) & (LANES - 1)
    pq = p[:, C_Q:C_K] * (HEAD_DIM ** -0.5)
    pk = p[:, C_K:C_V]
    pv = p[:, C_V:C_F]
    ones = (lane >= L_B0) & (lane <= L_B2)
    qa = jnp.where(lane == L_A0, hi, jnp.where(lane == L_A1, mid, jnp.where(lane == L_A2, lo,
                                                                            jnp.where(ones, 1.0, pq))))
    ones = (lane >= L_A0) & (lane <= L_A2)
    ka = jnp.where(lane == L_B0, -hi, jnp.where(lane == L_B1, -mid, jnp.where(lane == L_B2, -lo,
                                                                              jnp.where(ones, 1.0, pk))))
    va = jnp.where(lane == HEAD_DIM, 1.0, pv)
    for h in range(N_HEADS):
        sl = slice(h * LANES, (h + 1) * LANES)
        q_ref[0, h] = qa[:, sl].astype(BF16)
        k_ref[0, h] = ka[:, sl].astype(BF16)
        v_ref[0, h] = va[:, sl].astype(BF16)


def _in_proj(x, mod, g, w_aug, fb):
    B, T, _ = x.shape
    tm = min(TM_IN, T)
    grp = jax.ShapeDtypeStruct((B, T, 256), F32)
    head = jax.ShapeDtypeStruct((B, N_HEADS, T, LANES), BF16)
    grp_spec = pl.BlockSpec((1, tm, 256), lambda b, t: (b, t, 0))
    head_spec = pl.BlockSpec((1, N_HEADS, tm, LANES), lambda b, t: (b, 0, t, 0))
    return pl.pallas_call(
        functools.partial(_in_proj_kernel, tm=tm),
        grid=(B, T // tm),
        in_specs=[pl.BlockSpec((1, tm, D), lambda b, t: (b, t, 0)),
                  pl.BlockSpec((1, N_MOD, D), lambda b, t: (b, 0, 0)),
                  pl.BlockSpec((1, D), lambda b, t: (0, 0)),
                  pl.BlockSpec((D, IN_COLS_PAD), lambda b, t: (0, 0)),
                  pl.BlockSpec((1, LANES), lambda b, t: (0, 0))],
        out_specs=[grp_spec, grp_spec, grp_spec, head_spec, head_spec, head_spec],
        out_shape=[grp, grp, grp, head, head, head],
        scratch_shapes=[pltpu.VMEM((1, LANES), F32)],
        compiler_params=_params(VMEM_LIMIT),
        name="in_proj",
    )(x, mod, g, w_aug, fb)


def _lru_kernel(lx_ref, lg_ref, cw_ref, cb_ref, wa_ref, ba_ref, wx_ref, bx_ref, lam_ref, gn_ref,
                y_ref, tail_ref, hcarry_ref, *, tc):
    @pl.when(pl.program_id(1) == 0)
    def _():
        tail_ref[...] = jnp.zeros_like(tail_ref)
        hcarry_ref[...] = jnp.zeros_like(hcarry_ref)

    x = lx_ref[0]
    xcat = jnp.concatenate([tail_ref[...], x], axis=0)
    cw = cw_ref[...]
    xr = cb_ref[...]
    for j in range(CONV_W):
        d = CONV_W - 1 - j
        xs = x if d == 0 else pltpu.roll(xcat, d, 0)[SUBLANES:SUBLANES + tc]
        xr = xr + xs * cw[j:j + 1]
    tail_ref[...] = x[tc - SUBLANES:tc]

    xb = xr.astype(BF16)
    r = jax.nn.sigmoid(jnp.dot(xb, wa_ref[...], preferred_element_type=F32) + ba_ref[...])
    i = jax.nn.sigmoid(jnp.dot(xb, wx_ref[...], preferred_element_type=F32) + bx_ref[...])
    nl = -lam_ref[...]
    softplus = jnp.maximum(nl, 0.0) + jnp.log1p(jnp.exp(-jnp.abs(nl)))
    log_a = -LRU_C * r * softplus
    a = jnp.exp(log_a)
    th = jnp.tanh(log_a)
    u = xr * i * jnp.sqrt(-2.0 * th / (1.0 - th))

    row = lax.broadcasted_iota(I32, (tc, LRU_W), 0)
    d = 1
    while d < tc:
        u = u + a * _shift_rows(u, d, row, 0.0)
        a = a * _shift_rows(a, d, row, 1.0)
        d *= 2
    h = u + a * hcarry_ref[...]
    hcarry_ref[...] = h[tc - 1:tc]

    y = h * jax.nn.gelu(lg_ref[0])
    y_ref[0] = _rms(y, gn_ref[...]).astype(BF16)


def _lru_branch(lx, lg, conv_w, conv_b, wa, ba, wx, bx, lam, gn):
    B, T, _ = lx.shape
    tc = min(TC_LRU, T)
    tile = pl.BlockSpec((1, tc, LRU_W), lambda b, t: (b, t, 0))
    vec = pl.BlockSpec((1, LRU_W), lambda b, t: (0, 0))
    mat = pl.BlockSpec((LRU_W, LRU_W), lambda b, t: (0, 0))
    return pl.pallas_call(
        functools.partial(_lru_kernel, tc=tc),
        grid=(B, T // tc),
        in_specs=[tile, tile, pl.BlockSpec((CONV_W, LRU_W), lambda b, t: (0, 0)), vec, mat, vec, mat, vec, vec, vec],
        out_specs=tile,
        out_shape=jax.ShapeDtypeStruct((B, T, LRU_W), BF16),
        scratch_shapes=[pltpu.VMEM((SUBLANES, LRU_W), F32), pltpu.VMEM((1, LRU_W), F32)],
        name="rg_lru",
    )(lx, lg, conv_w, conv_b, wa, ba, wx, bx, lam, gn)


def _s5_kernel(u_ref, bcat_ref, apr_ref, api_ref, ptr_ref, pti_ref, ccat_ref, d_ref, gw_ref, gb_ref, gn_ref,
               y_ref, cr_ref, ci_ref, *, tc):
    @pl.when(pl.program_id(1) == 0)
    def _():
        cr_ref[...] = jnp.zeros_like(cr_ref)
        ci_ref[...] = jnp.zeros_like(ci_ref)

    u = u_ref[0]
    bu = jnp.dot(u.astype(BF16), bcat_ref[...], preferred_element_type=F32)
    xr = bu[:, :S5_STATES]
    xi = bu[:, S5_STATES:]
    row = lax.broadcasted_iota(I32, (tc, S5_STATES), 0)
    k, d = 0, 1
    while d < tc:
        ar = apr_ref[k:k + 1, :]
        ai = api_ref[k:k + 1, :]
        sr = _shift_rows(xr, d, row, 0.0)
        si = _shift_rows(xi, d, row, 0.0)
        xr, xi = xr + (ar * sr - ai * si), xi + (ar * si + ai * sr)
        k, d = k + 1, d * 2
    pr = ptr_ref[...]
    pi = pti_ref[...]
    cr = cr_ref[...]
    ci = ci_ref[...]
    xr, xi = xr + (pr * cr - pi * ci), xi + (pr * ci + pi * cr)
    cr_ref[...] = xr[tc - 1:tc]
    ci_ref[...] = xi[tc - 1:tc]

    xcat = jnp.concatenate([xr, xi], axis=1).astype(BF16)
    y = jnp.dot(xcat, ccat_ref[...], preferred_element_type=F32) + d_ref[...] * u
    y = jax.nn.gelu(y)
    y = y * jax.nn.sigmoid(jnp.dot(y.astype(BF16), gw_ref[...], preferred_element_type=F32) + gb_ref[...])
    y_ref[0] = _rms(y, gn_ref[...]).astype(BF16)


def _s5_branch(u, bcat, apow_re, apow_im, ptab_re, ptab_im, ccat, d, glu_w, glu_b, gn, tc):
    B, T, _ = u.shape
    tile = pl.BlockSpec((1, tc, S5_W), lambda b, t: (b, t, 0))
    vec = pl.BlockSpec((1, S5_W), lambda b, t: (0, 0))

    def full(a):
        return pl.BlockSpec(a.shape, lambda b, t: (0, 0))

    return pl.pallas_call(
        functools.partial(_s5_kernel, tc=tc),
        grid=(B, T // tc),
        in_specs=[tile, full(bcat), full(apow_re), full(apow_im), full(ptab_re), full(ptab_im), full(ccat),
                  vec, full(glu_w), vec, vec],
        out_specs=tile,
        out_shape=jax.ShapeDtypeStruct((B, T, S5_W), BF16),
        scratch_shapes=[pltpu.VMEM((1, S5_STATES), F32), pltpu.VMEM((1, S5_STATES), F32)],
        compiler_params=_params(VMEM_LIMIT),
        name="s5",
    )(u, bcat, apow_re, apow_im, ptab_re, ptab_im, ccat, d, glu_w, glu_b, gn)


def _s5_tables(a_re, a_im, b_re, b_im, c_re, c_im, log_dt, tc):
    dt = jnp.exp(log_dt)[:, None]
    mag = jnp.exp(a_re * dt)
    abar_re = mag * jnp.cos(a_im * dt)
    abar_im = mag * jnp.sin(a_im * dt)
    den = a_re * a_re + a_im * a_im
    num_re = abar_re - 1.0
    k_re = (num_re * a_re + abar_im * a_im) / den
    k_im = (abar_im * a_re - num_re * a_im) / den
    bbar_re = k_re[..., None] * b_re - k_im[..., None] * b_im
    bbar_im = k_re[..., None] * b_im + k_im[..., None] * b_re
    eye = jnp.eye(S5_GROUPS, dtype=F32)
    bd_re = jnp.einsum('gpc,gh->gchp', bbar_re, eye).reshape(S5_W, S5_STATES)
    bd_im = jnp.einsum('gpc,gh->gchp', bbar_im, eye).reshape(S5_W, S5_STATES)
    bcat = jnp.concatenate([bd_re, bd_im], axis=1).astype(BF16)
    cd_re = jnp.einsum('gcp,gh->gphc', c_re, eye).reshape(S5_STATES, S5_W)
    cd_im = jnp.einsum('gcp,gh->gphc', c_im, eye).reshape(S5_STATES, S5_W)
    ccat = jnp.concatenate([cd_re, -cd_im], axis=0).astype(BF16)
    ar = abar_re.reshape(1, S5_STATES)
    ai = abar_im.reshape(1, S5_STATES)
    pows_r, pows_i = [ar], [ai]
    tab_r, tab_i = ar, ai
    n = 1
    while n < tc:
        sr, si = pows_r[-1], pows_i[-1]
        tab_r, tab_i = (jnp.concatenate([tab_r, tab_r * sr - tab_i * si], axis=0),
                        jnp.concatenate([tab_i, tab_r * si + tab_i * sr], axis=0))
        pows_r.append(sr * sr - si * si)
        pows_i.append(2.0 * sr * si)
        n *= 2
    levels = len(pows_r) - 1
    pad = (-levels) % SUBLANES
    apow_re = jnp.concatenate(pows_r[:levels] + [jnp.zeros((pad, S5_STATES), F32)], axis=0)
    apow_im = jnp.concatenate(pows_i[:levels] + [jnp.zeros((pad, S5_STATES), F32)], axis=0)
    return bcat, apow_re, apow_im, tab_r, tab_i, ccat


def _attn_kernel(qt_ref, k_ref, vt_ref, o_ref, p_ref, *, tq):
    qi = pl.program_id(2)
    key = lax.broadcasted_iota(I32, (tq, tq), 0)
    qry = lax.broadcasted_iota(I32, (tq, tq), 1)
    heads = range(ATTN_HEADS_PER_STEP)
    qts =[qt_ref[0, hh] for hh in heads]

    def scores(hh, kj):
        k = k_ref[0, hh, pl.ds(pl.multiple_of(kj * tq, tq), tq), :]
        return jnp.dot(k, qts[hh], preferred_element_type=F32)

    def weighted_values(hh, kj, p):
        vt = vt_ref[0, hh, :, pl.ds(pl.multiple_of(kj * tq, tq), tq)]
        return jnp.dot(vt, p, preferred_element_type=F32)

    def softmax_step(s, m):
        m_new = jnp.maximum(m, jnp.max(s, axis=0, keepdims=True))
        return m_new, jnp.exp(m - m_new), jnp.exp(s - m_new).astype(BF16)

    p_ref[...] = jnp.zeros_like(p_ref)

    def body(kj, carry):
        ss = [scores(hh, kj) for hh in heads]
        pvs = [weighted_values(hh, jnp.maximum(kj - 1, 0), p_ref[hh]) for hh in heads]
        out = []
        for hh in heads:
            m, alpha, acc = carry[hh]
            acc = alpha * acc + pvs[hh]
            m, alpha, p = softmax_step(ss[hh], m)
            p_ref[hh] = p
            out.append((m, alpha, acc))
        return tuple(out)

    init = tuple((jnp.full((1, tq), NEG_BIG, F32), jnp.ones((1, tq), F32), jnp.zeros((LANES, tq), F32))
                 for _ in heads)
    carry = lax.fori_loop(0, qi, body, init)
    outs = []
    for hh in heads:
        m, alpha, acc = carry[hh]
        acc = alpha * acc + weighted_values(hh, jnp.maximum(qi - 1, 0), p_ref[hh])
        s = jnp.where(key <= qry, scores(hh, qi), NEG_BIG)
        m, alpha, p = softmax_step(s, m)
        acc = alpha * acc + weighted_values(hh, qi, p)
        out_t = acc / acc[HEAD_DIM:HEAD_DIM + 1, :]
        outs.append(out_t.T[:, :HEAD_DIM])
    o_ref[0] = jnp.concatenate(outs, axis=1)


def _attention(qa, ka, va):
    B, H, T, _ = qa.shape
    tq = min(TQ, T)
    hs = ATTN_HEADS_PER_STEP
    qt = jnp.swapaxes(qa, 2, 3)
    vt = jnp.swapaxes(va, 2, 3)
    return pl.pallas_call(
        functools.partial(_attn_kernel, tq=tq),
        grid=(B, H // hs, T // tq),
        in_specs=[pl.BlockSpec((1, hs, LANES, tq), lambda b, hp, qi: (b, hp, 0, qi)),
                  pl.BlockSpec((1, hs, T, LANES), lambda b, hp, qi: (b, hp, 0, 0)),
                  pl.BlockSpec((1, hs, LANES, T), lambda b, hp, qi: (b, hp, 0, 0))],
        out_specs=pl.BlockSpec((1, tq, hs * HEAD_DIM), lambda b, hp, qi: (b, qi, hp)),
        out_shape=jax.ShapeDtypeStruct((B, T, ATTN_W), F32),
        scratch_shapes=[pltpu.VMEM((hs, tq, tq), BF16)],
        compiler_params=_params(VMEM_LIMIT),
        name="fox_attention",
    )(qt, ka, vt)


def _out_proj_kernel(yl_ref, ys_ref, ya_ref, x_ref, mod_ref, gna_ref, wout_ref, pmg_ref, pfg_ref,
                     rwh_ref, rwl_ref, rb_ref,
                     x1_ref, h2_ref, idx_ref, gate_ref, rank_ref, cnt_ref, carry_ref, *, tm):
    @pl.when((pl.program_id(0) == 0) & (pl.program_id(1) == 0))
    def _():
        carry_ref[...] = jnp.zeros_like(carry_ref)

    mod = mod_ref[0]
    ya = _rms(ya_ref[0], gna_ref[...])
    ycat = jnp.concatenate([yl_ref[0], ys_ref[0], ya.astype(BF16)], axis=1)
    y = jnp.dot(ycat, wout_ref[...], preferred_element_type=F32)
    x1 = x_ref[0] + mod[2:3] * _rms(y, pmg_ref[...])
    x1_ref[0] = x1
    h2 = _rms(x1, pfg_ref[...]) * (1.0 + mod[4:5]) + mod[3:4]
    for c in range(ROW_CHUNKS):
        h2_ref[pl.ds(c, tm, stride=ROW_CHUNKS), :] = h2[:, c * LANES:(c + 1) * LANES]

    hh = h2.astype(BF16)
    hl = (h2 - hh.astype(F32)).astype(BF16)
    rwh = rwh_ref[...]
    logits = (jnp.dot(hh, rwh, preferred_element_type=F32) + jnp.dot(hl, rwh, preferred_element_type=F32)
              + jnp.dot(hh, rwl_ref[...], preferred_element_type=F32) + rb_ref[...])
    lane = lax.broadcasted_iota(I32, (tm, LANES), 1)
    lane_f = lane.astype(F32)
    work = jnp.where(lane < N_EXPERTS, logits, NEG_BIG)
    vals, idxs = [], []
    for _ in range(TOP_K):
        mx = jnp.max(work, axis=1, keepdims=True)
        ik = jnp.min(jnp.where(work == mx, lane_f, float(LANES)), axis=1, keepdims=True)
        vals.append(mx)
        idxs.append(ik)
        work = jnp.where(lane_f == ik, 2.0 * NEG_BIG, work)
    es = [jnp.exp(v - vals[0]) for v in vals]
    den = es[0] + es[1] + es[2] + es[3]

    onehot = jnp.zeros((tm, LANES), F32)
    for ik in idxs:
        onehot = onehot + jnp.where(lane_f == ik, 1.0, 0.0)
    r_i = lax.broadcasted_iota(I32, (tm, tm), 0)
    c_i = lax.broadcasted_iota(I32, (tm, tm), 1)
    below = jnp.where(c_i < r_i, 1.0, 0.0).astype(BF16)
    prior = jnp.dot(below, onehot.astype(BF16), preferred_element_type=F32) + carry_ref[...]
    carry = carry_ref[...] + jnp.sum(onehot, axis=0, keepdims=True)
    carry_ref[...] = carry
    cnt_ref[...] = carry.astype(I32)

    idx_full = jnp.zeros((tm, LANES), F32)
    gate_full = jnp.zeros((tm, LANES), F32)
    rank_full = jnp.zeros((tm, LANES), F32)
    for k in range(TOP_K):
        rk = jnp.sum(jnp.where(lane_f == idxs[k], prior, 0.0), axis=1, keepdims=True)
        idx_full = jnp.where(lane == k, idxs[k], idx_full)
        gate_full = jnp.where(lane == k, es[k] / den, gate_full)
        rank_full = jnp.where(lane == k, rk, rank_full)
    idx_ref[...] = idx_full[:, :TOP_K].astype(I32)
    gate_ref[...] = gate_full[:, :TOP_K]
    rank_ref[...] = rank_full[:, :TOP_K].astype(I32)


def _out_proj(yl, ys, ya, x, mod, gna, wout, pmg, pfg, rwh, rwl, rb):
    B, T, _ = x.shape
    tm = min(TM_OUT, T)
    nt = T // tm
    N = B * T

    def vec(n):
        return pl.BlockSpec((1, n), lambda b, t: (0, 0))

    tok = lambda b, t: (b * nt + t, 0)
    return pl.pallas_call(
        functools.partial(_out_proj_kernel, tm=tm),
        grid=(B, nt),
        in_specs=[pl.BlockSpec((1, tm, LRU_W), lambda b, t: (b, t, 0)),
                  pl.BlockSpec((1, tm, S5_W), lambda b, t: (b, t, 0)),
                  pl.BlockSpec((1, tm, ATTN_W), lambda b, t: (b, t, 0)),
                  pl.BlockSpec((1, tm, D), lambda b, t: (b, t, 0)),
                  pl.BlockSpec((1, N_MOD, D), lambda b, t: (b, 0, 0)),
                  vec(ATTN_W),
                  pl.BlockSpec((D, D), lambda b, t: (0, 0)),
                  vec(D), vec(D),
                  pl.BlockSpec((D, LANES), lambda b, t: (0, 0)),
                  pl.BlockSpec((D, LANES), lambda b, t: (0, 0)),
                  vec(LANES)],
        out_specs=[pl.BlockSpec((1, tm, D), lambda b, t: (b, t, 0)),
                   pl.BlockSpec((tm * ROW_CHUNKS, LANES), tok),
                   pl.BlockSpec((tm, TOP_K), tok),
                   pl.BlockSpec((tm, TOP_K), tok),
                   pl.BlockSpec((tm, TOP_K), tok),
                   pl.BlockSpec((1, LANES), lambda b, t: (0, 0))],
        out_shape=[jax.ShapeDtypeStruct((B, T, D), F32),
                   jax.ShapeDtypeStruct((N * ROW_CHUNKS, LANES), F32),
                   jax.ShapeDtypeStruct((N, TOP_K), I32),
                   jax.ShapeDtypeStruct((N, TOP_K), F32),
                   jax.ShapeDtypeStruct((N, TOP_K), I32),
                   jax.ShapeDtypeStruct((1, LANES), I32)],
        scratch_shapes=[pltpu.VMEM((1, LANES), F32)],
        compiler_params=_params(VMEM_LIMIT),
        name="out_proj_router",
    )(yl, ys, ya, x, mod, gna, wout, pmg, pfg, rwh, rwl, rb)


def _row_copy(src, src_row, dst, dst_row, sem):
    return pltpu.make_async_copy(src.at[pl.ds(pl.multiple_of(src_row * ROW_CHUNKS, ROW_CHUNKS), ROW_CHUNKS), :],
                                 dst.at[pl.ds(pl.multiple_of(dst_row * ROW_CHUNKS, ROW_CHUNKS), ROW_CHUNKS), :],
                                 sem)


def _dispatch_kernel(start_ref, cnt_ref, pad_ref, idx_ref, rank_ref, h_ref, dest_ref, xb_ref, sem, *, tn):
    i = pl.program_id(0)

    def issue(j, _):
        for k in range(TOP_K):
            a = j * TOP_K + k
            dst = start_ref[idx_ref[a]] + rank_ref[a]
            dest_ref[a] = dst
            _row_copy(h_ref, j, xb_ref, dst, sem).start()
        return 0

    lax.fori_loop(0, tn, issue, 0)

    def drain(j, _):
        for k in range(TOP_K):
            _row_copy(h_ref, 0, xb_ref, 0, sem).wait()
        return 0

    lax.fori_loop(0, tn, drain, 0)

    @pl.when(i == pl.num_programs(0) - 1)
    def _():
        def per_expert(e, _):
            lo = start_ref[e] + cnt_ref[e]
            hi = start_ref[e] + pad_ref[e]

            def fill(r, _):
                _row_copy(h_ref, 0, xb_ref, r, sem).start()
                return 0

            lax.fori_loop(lo, hi, fill, 0)

            def fill_wait(r, _):
                _row_copy(h_ref, 0, xb_ref, r, sem).wait()
                return 0

            lax.fori_loop(lo, hi, fill_wait, 0)
            return 0

        lax.fori_loop(0, N_EXPERTS, per_expert, 0)


def _dispatch(starts, counts, padded, idx_flat, rank_flat, h_rows, n_rows):
    NK = idx_flat.shape[0]
    N = NK // TOP_K
    tn = min(TN_ROWS, N)
    smem_blk = pl.BlockSpec((tn * TOP_K,), lambda i, *_: (i,), memory_space=pltpu.SMEM)
    grid_spec = pltpu.PrefetchScalarGridSpec(
        num_scalar_prefetch=3,
        grid=(N // tn,),
        in_specs=[smem_blk, smem_blk, pl.BlockSpec((tn * ROW_CHUNKS, LANES), lambda i, *_: (i, 0))],
        out_specs=[smem_blk, pl.BlockSpec(memory_space=pl.ANY)],
        scratch_shapes=[pltpu.SemaphoreType.DMA(())],
    )
    return pl.pallas_call(
        functools.partial(_dispatch_kernel, tn=tn),
        grid_spec=grid_spec,
        out_shape=[jax.ShapeDtypeStruct((NK,), I32),
                   jax.ShapeDtypeStruct((n_rows * ROW_CHUNKS, LANES), F32)],
        name="moe_dispatch",
    )(starts, counts, padded, idx_flat, rank_flat, h_rows)


def _expert_kernel(be_ref, nused_ref, x_ref, wgu_ref, bgu_ref, wdn_ref, bdn_ref, y_ref, wgu_s, wdn_s, *, rows):
    b = pl.program_id(0)
    prev = be_ref[jnp.maximum(b - 1, 0)]

    @pl.when((b == 0) | (be_ref[b] != prev))
    def _():
        wgu_s[...] = wgu_ref[0, 0].astype(BF16)
        wdn_s[...] = wdn_ref[0, 0].astype(BF16)

    @pl.when(b < nused_ref[0])
    def _():
        x = jnp.concatenate([x_ref[pl.ds(c, rows, stride=ROW_CHUNKS), :] for c in range(ROW_CHUNKS)], axis=1)
        gu = jnp.dot(x.astype(BF16), wgu_s[...], preferred_element_type=F32) + bgu_ref[0, 0]
        g = jnp.minimum(gu[:, :D_FF], SWIGLU_LIMIT)
        up = jnp.clip(gu[:, D_FF:], -SWIGLU_LIMIT, SWIGLU_LIMIT)
        act = (up + 1.0) * (g * jax.nn.sigmoid(SWIGLU_ALPHA * g))
        y = jnp.dot(act.astype(BF16), wdn_s[...], preferred_element_type=F32) + bdn_ref[0, 0]
        for c in range(ROW_CHUNKS):
            y_ref[pl.ds(c, rows, stride=ROW_CHUNKS), :] = y[:, c * LANES:(c + 1) * LANES]


def _experts(layer, block_e, n_used, xb, w_gu, b_gu, w_dn, b_dn, rows):
    n_blocks = block_e.shape[0]
    L, E = w_gu.shape[:2]
    blk = lambda b, be, nu: (jnp.minimum(b, nu[0] - 1), 0)
    wsel = lambda b, be, nu: (layer, be[b], 0, 0)
    grid_spec = pltpu.PrefetchScalarGridSpec(
        num_scalar_prefetch=2,
        grid=(n_blocks,),
        in_specs=[pl.BlockSpec((rows * ROW_CHUNKS, LANES), blk),
                  pl.BlockSpec((1, 1, D, 2 * D_FF), wsel),
                  pl.BlockSpec((1, 1, 1, 2 * D_FF), wsel),
                  pl.BlockSpec((1, 1, D_FF, D), wsel),
                  pl.BlockSpec((1, 1, 1, D), wsel)],
        out_specs=pl.BlockSpec((rows * ROW_CHUNKS, LANES), blk),
        scratch_shapes=[pltpu.VMEM((D, 2 * D_FF), BF16), pltpu.VMEM((D_FF, D), BF16)],
    )
    return pl.pallas_call(
        functools.partial(_expert_kernel, rows=rows),
        grid_spec=grid_spec,
        out_shape=jax.ShapeDtypeStruct(xb.shape, F32),
        compiler_params=_params(VMEM_LIMIT),
        name="moe_experts",
    )(block_e, n_used, xb, w_gu, b_gu.reshape(L, E, 1, 2 * D_FF), w_dn, b_dn.reshape(L, E, 1, D))


def _combine_kernel(dest_ref, yb_ref, gate_ref, x1_ref, mod_ref, g_ref, o_ref, buf_ref, sem, *, tn):
    def issue(j, _):
        for k in range(TOP_K):
            pltpu.make_async_copy(
                yb_ref.at[pl.ds(pl.multiple_of(dest_ref[j * TOP_K + k] * ROW_CHUNKS, ROW_CHUNKS), ROW_CHUNKS), :],
                buf_ref.at[k, pl.ds(pl.multiple_of(j * ROW_CHUNKS, ROW_CHUNKS), ROW_CHUNKS), :], sem).start()
        return 0

    lax.fori_loop(0, tn, issue, 0)

    def drain(j, _):
        for k in range(TOP_K):
            pltpu.make_async_copy(yb_ref.at[pl.ds(0, ROW_CHUNKS), :],
                                  buf_ref.at[k, pl.ds(0, ROW_CHUNKS), :], sem).wait()
        return 0

    lax.fori_loop(0, tn, drain, 0)

    gates = gate_ref[...]
    y = jnp.zeros((tn, D), F32)
    for k in range(TOP_K):
        yk = jnp.concatenate([buf_ref[k, pl.ds(c, tn, stride=ROW_CHUNKS), :] for c in range(ROW_CHUNKS)], axis=1)
        y = y + gates[:, k:k + 1] * yk
    mod = mod_ref[0]
    o_ref[0] = x1_ref[0] + mod[5:6] * _rms(y, g_ref[...])


def _combine(dest_flat, yb, gates, x1, mod, g):
    B, T, _ = x1.shape
    tn = min(TN_COMB, T)
    nt = T // tn
    grid_spec = pltpu.PrefetchScalarGridSpec(
        num_scalar_prefetch=0,
        grid=(B, nt),
        in_specs=[pl.BlockSpec((tn * TOP_K,), lambda b, t: (b * nt + t,), memory_space=pltpu.SMEM),
                  pl.BlockSpec(memory_space=pl.ANY),
                  pl.BlockSpec((tn, TOP_K), lambda b, t: (b * nt + t, 0)),
                  pl.BlockSpec((1, tn, D), lambda b, t: (b, t, 0)),
                  pl.BlockSpec((1, N_MOD, D), lambda b, t: (b, 0, 0)),
                  pl.BlockSpec((1, D), lambda b, t: (0, 0))],
        out_specs=pl.BlockSpec((1, tn, D), lambda b, t: (b, t, 0)),
        scratch_shapes=[pltpu.VMEM((TOP_K, tn * ROW_CHUNKS, LANES), F32), pltpu.SemaphoreType.DMA(())],
    )
    return pl.pallas_call(
        functools.partial(_combine_kernel, tn=tn),
        grid_spec=grid_spec,
        out_shape=jax.ShapeDtypeStruct((B, T, D), F32),
        name="moe_combine",
    )(dest_flat, yb, gates, x1, mod, g)


def _fused_expert_kernel(be_ref, nused_ref, src_ref, dst_ref, h_ref, wgu_ref, bgu_ref, wdn_ref, bdn_ref, y_ref,
                         xbuf, ybuf, wgu_s, wdn_s, gsem, ssem, *, rows, n_blocks, n_assign):
    b = pl.program_id(0)
    nu = nused_ref[0]
    slot = lax.rem(b, 2)
    other = 1 - slot
    xslot = lax.rem(b, 3)
    xnext = lax.rem(b + 2, 3)
    src_off = lax.rem(jnp.minimum(b + 2, n_blocks - 1) * rows, MAP_BLOCK)
    dst_off = lax.rem(jnp.maximum(b - 1, 0) * rows, MAP_BLOCK)

    def gather(i, tok, buf_slot):
        return pltpu.make_async_copy(
            h_ref.at[pl.ds(pl.multiple_of(tok * ROW_CHUNKS, ROW_CHUNKS), ROW_CHUNKS), :],
            xbuf.at[buf_slot, pl.ds(i * ROW_CHUNKS, ROW_CHUNKS), :], gsem.at[buf_slot])

    def scatter(i, dst, buf_slot):
        return pltpu.make_async_copy(
            ybuf.at[buf_slot, pl.ds(i * ROW_CHUNKS, ROW_CHUNKS), :],
            y_ref.at[pl.ds(pl.multiple_of(dst * ROW_CHUNKS, ROW_CHUNKS), ROW_CHUNKS), :], ssem.at[buf_slot])

    @pl.when(b == 0)
    def _():
        ybuf[...] = jnp.zeros(ybuf.shape, F32)
        for i in range(rows):
            gather(i, src_ref[i], 0).start()
            gather(i, src_ref[rows + i], 1).start()
            scatter(i, n_assign + 2 * rows + i, 0).start()

    @pl.when(b <= nu)
    def _():
        for i in range(rows):
            gather(i, 0, xslot).wait()

    prev = be_ref[jnp.maximum(b - 1, 0)]

    @pl.when((b == 0) | (be_ref[b] != prev))
    def _():
        wgu_s[...] = wgu_ref[0, 0].astype(BF16)
        wdn_s[...] = wdn_ref[0, 0].astype(BF16)

    @pl.when(b < nu)
    def _():
        for i in range(rows):
            gather(i, src_ref[src_off + i], xnext).start()
        for i in range(rows):
            dst = jnp.where(b == 0, n_assign + 3 * rows + i, dst_ref[dst_off + i])
            scatter(i, dst, other).start()
        x = jnp.concatenate([xbuf[xslot, pl.ds(c, rows, stride=ROW_CHUNKS), :] for c in range(ROW_CHUNKS)], axis=1)
        gu = jnp.dot(x.astype(BF16), wgu_s[...], preferred_element_type=F32) + bgu_ref[0, 0]
        g = jnp.minimum(gu[:, :D_FF], SWIGLU_LIMIT)
        up = jnp.clip(gu[:, D_FF:], -SWIGLU_LIMIT, SWIGLU_LIMIT)
        act = (up + 1.0) * (g * jax.nn.sigmoid(SWIGLU_ALPHA * g))
        y = jnp.dot(act.astype(BF16), wdn_s[...], preferred_element_type=F32) + bdn_ref[0, 0]
        for i in range(rows):
            scatter(i, 0, slot).wait()
        for c in range(ROW_CHUNKS):
            ybuf[slot, pl.ds(c, rows, stride=ROW_CHUNKS), :] = y[:, c * LANES:(c + 1) * LANES]

    @pl.when(b == nu)
    def _():
        for i in range(rows):
            gather(i, 0, lax.rem(b + 1, 3)).wait()
        for i in range(rows):
            scatter(i, 0, slot).wait()
        for i in range(rows):
            scatter(i, dst_ref[dst_off + i], other).start()
        for i in range(rows):
            scatter(i, 0, other).wait()


def _fused_experts(layer, block_e, n_used, src_tok, dst_row, h_rows, w_gu, b_gu, w_dn, b_dn, rows, n_assign):
    n_blocks = block_e.shape[0]
    L, E = w_gu.shape[:2]
    wsel = lambda b, be, nu: (layer, be[b], 0, 0)
    grid_spec = pltpu.PrefetchScalarGridSpec(
        num_scalar_prefetch=2,
        grid=(n_blocks,),
        in_specs=[pl.BlockSpec((MAP_BLOCK,), lambda b, be, nu: (jnp.minimum(b + 2, n_blocks - 1) * rows // MAP_BLOCK,),
                               memory_space=pltpu.SMEM),
                  pl.BlockSpec((MAP_BLOCK,), lambda b, be, nu: (jnp.maximum(b - 1, 0) * rows // MAP_BLOCK,),
                               memory_space=pltpu.SMEM),
                  pl.BlockSpec(memory_space=pl.ANY),
                  pl.BlockSpec((1, 1, D, 2 * D_FF), wsel),
                  pl.BlockSpec((1, 1, 1, 2 * D_FF), wsel),
                  pl.BlockSpec((1, 1, D_FF, D), wsel),
                  pl.BlockSpec((1, 1, 1, D), wsel)],
        out_specs=pl.BlockSpec(memory_space=pl.ANY),
        scratch_shapes=[pltpu.VMEM((3, rows * ROW_CHUNKS, LANES), F32),
                        pltpu.VMEM((2, rows * ROW_CHUNKS, LANES), F32),
                        pltpu.VMEM((D, 2 * D_FF), BF16), pltpu.VMEM((D_FF, D), BF16),
                        pltpu.SemaphoreType.DMA((3,)), pltpu.SemaphoreType.DMA((2,))],
    )
    return pl.pallas_call(
        functools.partial(_fused_expert_kernel, rows=rows, n_blocks=n_blocks, n_assign=n_assign),
        grid_spec=grid_spec,
        out_shape=jax.ShapeDtypeStruct(((n_assign + 4 * rows) * ROW_CHUNKS, LANES), F32),
        compiler_params=_params(VMEM_LIMIT),
        name="moe_experts",
    )(block_e, n_used, src_tok, dst_row, h_rows, w_gu, b_gu.reshape(L, E, 1, 2 * D_FF), w_dn,
      b_dn.reshape(L, E, 1, D))


def _dense_combine_kernel(y_ref, gate_ref, x1_ref, mod_ref, g_ref, o_ref, *, tn):
    gates = gate_ref[...]
    y = jnp.zeros((tn, D), F32)
    for k in range(TOP_K):
        yk = jnp.concatenate(
            [y_ref[pl.ds(k * ROW_CHUNKS + c, tn, stride=TOP_K * ROW_CHUNKS), :] for c in range(ROW_CHUNKS)], axis=1)
        y = y + gates[:, k:k + 1] * yk
    mod = mod_ref[0]
    o_ref[0] = x1_ref[0] + mod[5:6] * _rms(y, g_ref[...])


def _dense_combine(y_rows, gates, x1, mod, g):
    B, T, _ = x1.shape
    tn = min(TN_COMB, T)
    nt = T // tn
    return pl.pallas_call(
        functools.partial(_dense_combine_kernel, tn=tn),
        grid=(B, nt),
        in_specs=[pl.BlockSpec((tn * TOP_K * ROW_CHUNKS, LANES), lambda b, t: (b * nt + t, 0)),
                  pl.BlockSpec((tn, TOP_K), lambda b, t: (b * nt + t, 0)),
                  pl.BlockSpec((1, tn, D), lambda b, t: (b, t, 0)),
                  pl.BlockSpec((1, N_MOD, D), lambda b, t: (b, 0, 0)),
                  pl.BlockSpec((1, D), lambda b, t: (0, 0))],
        out_specs=pl.BlockSpec((1, tn, D), lambda b, t: (b, t, 0)),
        out_shape=jax.ShapeDtypeStruct((B, T, D), F32),
        name="moe_combine",
    )(y_rows, gates, x1, mod, g)


def _head_slabs(w):
    w = w.reshape(D, N_HEADS, HEAD_DIM)
    return jnp.pad(w, ((0, 0), (0, 0), (0, LANES - HEAD_DIM))).reshape(D, N_HEADS * LANES)


def _block_diag(w):
    nb = w.shape[0]
    return jnp.einsum('hij,hg->higj', w, jnp.eye(nb, dtype=w.dtype)).reshape(LRU_W, LRU_W)


def kernel(x, c, ada_w, ada_b, pre_mix_g, w_in, conv_w, conv_b, lru_wa, lru_ba, lru_wx, lru_bx, lru_lambda, s5_a_re, s5_a_im, s5_b_re, s5_b_im, s5_c_re, s5_c_im, s5_d, s5_log_dt, s5_glu_w, s5_glu_b, fox_fb, gn_lru, gn_s5, gn_attn, w_out, post_mix_g, pre_ffn_g, router_w, router_b, w_gu, b_gu, w_dn, b_dn, post_ffn_g):
    B, T, _ = x.shape
    N = B * T
    L = ada_w.shape[0]
    tc_s5 = min(TC_S5, T)
    n_blocks = (N * TOP_K) // EXP_ROWS + N_EXPERTS
    n_rows = n_blocks * EXP_ROWS
    mod_all = _modulation(c, ada_w, ada_b).reshape(L, B, N_MOD, D)

    for l in range(L):
        mod = mod_all[l]
        wl = w_in[l]
        w_aug = jnp.concatenate(
            [wl[:, :O_Q], _head_slabs(wl[:, O_Q:O_K]), _head_slabs(wl[:, O_K:O_V]), _head_slabs(wl[:, O_V:O_F]),
             jnp.pad(wl[:, O_F:], ((0, 0), (0, LANES - N_HEADS)))], axis=1).astype(BF16)
        fb = jnp.pad(fox_fb[l], (0, LANES - N_HEADS)).reshape(1, LANES)
        lx, lg, su, qa, ka, va = _in_proj(x, mod, pre_mix_g[l].reshape(1, D), w_aug, fb)

        yl = _lru_branch(lx, lg, conv_w[l], conv_b[l].reshape(1, LRU_W),
                         _block_diag(lru_wa[l]).astype(BF16), lru_ba[l].reshape(1, LRU_W),
                         _block_diag(lru_wx[l]).astype(BF16), lru_bx[l].reshape(1, LRU_W),
                         lru_lambda[l].reshape(1, LRU_W), gn_lru[l].reshape(1, LRU_W))

        tabs = _s5_tables(s5_a_re[l], s5_a_im[l], s5_b_re[l], s5_b_im[l], s5_c_re[l], s5_c_im[l], s5_log_dt[l], tc_s5)
        ys = _s5_branch(su, *tabs, s5_d[l].reshape(1, S5_W), s5_glu_w[l].astype(BF16),
                        s5_glu_b[l].reshape(1, S5_W), gn_s5[l].reshape(1, S5_W), tc_s5)

        ya = _attention(qa, ka, va)

        rw = jnp.pad(router_w[l], ((0, 0), (0, LANES - N_EXPERTS)))
        rwh = rw.astype(BF16)
        rwl = (rw - rwh.astype(F32)).astype(BF16)
        rb = jnp.pad(router_b[l], (0, LANES - N_EXPERTS)).reshape(1, LANES)
        x1, h_rows, idx, gates, rank, cnt = _out_proj(
            yl, ys, ya, x, mod, gn_attn[l].reshape(1, ATTN_W), w_out[l].astype(BF16),
            post_mix_g[l].reshape(1, D), pre_ffn_g[l].reshape(1, D), rwh, rwl, rb)

        counts = cnt[0, :N_EXPERTS]
        blocks_e = (counts + EXP_ROWS - 1) // EXP_ROWS
        padded = blocks_e * EXP_ROWS
        block_ends = jnp.cumsum(blocks_e)
        starts = (block_ends - blocks_e) * EXP_ROWS
        n_used = block_ends[-1:]
        bid = jnp.minimum(jnp.arange(n_blocks, dtype=I32), n_used[0] - 1)
        block_e = jnp.minimum(jnp.sum((bid[:, None] >= block_ends[None, :]).astype(I32), axis=1), N_EXPERTS - 1)

        n_assign = N * TOP_K
        dest = (jnp.take(starts.astype(I32), idx) + rank).reshape(n_assign)
        row_assign = jnp.full((n_rows,), n_assign, I32).at[dest].set(
            jnp.arange(n_assign, dtype=I32), unique_indices=True)
        is_real = row_assign < n_assign
        src_tok = jnp.where(is_real, row_assign // TOP_K, 0)
        spare = n_assign + jnp.arange(n_rows, dtype=I32) % (2 * EXP_ROWS)
        dst_row = jnp.where(is_real, row_assign, spare)

        y_rows = _fused_experts(l, block_e, n_used.astype(I32), src_tok, dst_row, h_rows,
                                w_gu, b_gu, w_dn, b_dn, EXP_ROWS, n_assign)
        x = _dense_combine(y_rows, gates, x1, mod, post_ffn_g[l].reshape(1, D))
    return x
```

```python
import functools

import jax
import jax.numpy as jnp
from jax import lax
from jax.experimental import pallas as pl
from jax.experimental.pallas import tpu as pltpu

F32 = jnp.float32
BF16 = jnp.bfloat16
I32 = jnp.int32

D = 1024
LRU_W = 256
LRU_BLOCK_W = 64
CONV_W = 4
LRU_C = 8.0
S5_W = 256
S5_GROUP_W = 16
S5_GROUPS = 16
S5_STATE = 64
S5_STATES = S5_GROUPS * S5_STATE
HEAD_DIM = 64
ATTN_W = 512
N_HEADS = 8
O_LRU_G = 256
O_S5 = 512
O_Q = 768
O_K = O_Q + ATTN_W
O_V = O_K + ATTN_W
O_F = O_V + ATTN_W
N_EXPERTS = 32
TOP_K = 4
D_FF = 1024
SWIGLU_LIMIT = 7.0
SWIGLU_ALPHA = 1.702
N_MOD = 6
RMS_EPS = 1e-6

LANES = 128
SUBLANES = 8
VMEM_LIMIT = 56 * 1024 * 1024
ROW_CHUNKS = D // LANES

C_Q = 768
C_K = C_Q + N_HEADS * LANES
C_V = C_K + N_HEADS * LANES
C_F = C_V + N_HEADS * LANES
IN_COLS_PAD = C_F + LANES
L_A0, L_A1, L_A2, L_B0, L_B1, L_B2 = 64, 65, 66, 67, 68, 69

TM_IN = 256
TC_LRU = 256
TC_S5 = 128
TQ = 512
ATTN_HEADS_PER_STEP = 4
TM_OUT = 512
TN_ROWS = 1024
TN_COMB = 256
EXP_ROWS = 256
MAP_BLOCK = 1024
NEG_BIG = -1e30


def _rms(x, g):
    return x * lax.rsqrt(jnp.mean(x * x, axis=-1, keepdims=True) + RMS_EPS) * g


def _shift_rows(v, d, row, fill):
    return jnp.where(row >= d, pltpu.roll(v, d, 0), fill)


def _params(vmem=None):
    return pltpu.CompilerParams(vmem_limit_bytes=vmem) if vmem else None


def _mod_kernel(c_ref, w_ref, b_ref, o_ref):
    c = c_ref[...]
    s = c * jax.nn.sigmoid(c)
    o_ref[0] = jnp.dot(s.astype(BF16), w_ref[0].astype(BF16), preferred_element_type=F32) + b_ref[0]


def _modulation(c, ada_w, ada_b):
    L, _, W = ada_w.shape
    B = c.shape[0]
    return pl.pallas_call(
        _mod_kernel,
        grid=(L, W // D),
        in_specs=[pl.BlockSpec((B, D), lambda l, j: (0, 0)),
                  pl.BlockSpec((1, D, D), lambda l, j: (l, 0, j)),
                  pl.BlockSpec((1, 1, D), lambda l, j: (l, 0, j))],
        out_specs=pl.BlockSpec((1, B, D), lambda l, j: (l, 0, j)),
        out_shape=jax.ShapeDtypeStruct((L, B, W), F32),
        name="adaln_mod",
    )(c, ada_w, ada_b.reshape(L, 1, W))


def _in_proj_kernel(x_ref, mod_ref, g_ref, w_ref, fb_ref,
                    lx_ref, lg_ref, su_ref, q_ref, k_ref, v_ref, fcarry_ref, *, tm):
    @pl.when(pl.program_id(1) == 0)
    def _():
        fcarry_ref[...] = jnp.zeros_like(fcarry_ref)

    mod = mod_ref[0]
    h = _rms(x_ref[0], g_ref[...]) * (1.0 + mod[1:2]) + mod[0:1]
    p = jnp.dot(h.astype(BF16), w_ref[...], preferred_element_type=F32)
    lx_ref[0] = p[:, 0:O_LRU_G]
    lg_ref[0] = p[:, O_LRU_G:O_S5]
    su_ref[0] = p[:, O_S5:O_Q]

    z = p[:, C_F:C_F + LANES] + fb_ref[...]
    logf = jnp.minimum(z, 0.0) - jnp.log1p(jnp.exp(-jnp.abs(z)))
    row = lax.broadcasted_iota(I32, (tm, LANES), 0)
    d = 1
    while d < tm:
        logf = logf + _shift_rows(logf, d, row, 0.0)
        d *= 2
    fsum = logf + fcarry_ref[...]
    fcarry_ref[...] = fsum[tm - 1:tm, :]

    fexp = jnp.concatenate([jnp.broadcast_to(fsum[:, h:h + 1], (tm, LANES)) for h in range(N_HEADS)], axis=1)
    hi = fexp.astype(BF16).astype(F32)
    r1 = fexp - hi
    mid = r1.astype(BF16).astype(F32)
    lo = r1 - mid
    lane = lax.broadcasted_iota(I32, (tm, N_HEADS * LANES), 1) & (LANES - 1)
    pq = p[:, C_Q:C_K] * (HEAD_DIM ** -0.5)
    pk = p[:, C_K:C_V]
    pv = p[:, C_V:C_F]
    ones = (lane >= L_B0) & (lane <= L_B2)
    qa = jnp.where(lane == L_A0, hi, jnp.where(lane == L_A1, mid, jnp.where(lane == L_A2, lo,
                                                                            jnp.where(ones, 1.0, pq))))
    ones = (lane >= L_A0) & (lane <= L_A2)
    ka = jnp.where(lane == L_B0, -hi, jnp.where(lane == L_B1, -mid, jnp.where(lane == L_B2, -lo,
                                                                              jnp.where(ones, 1.0, pk))))
    va = jnp.where(lane == HEAD_DIM, 1.0, pv)
    for h in range(N_HEADS):
        sl = slice(h * LANES, (h + 1) * LANES)
        q_ref[0, h] = qa[:, sl].astype(BF16)
        k_ref[0, h] = ka[:, sl].astype(BF16)
        v_ref[0, h] = va[:, sl].astype(BF16)


def _in_proj(x, mod, g, w_aug, fb):
    B, T, _ = x.shape
    tm = min(TM_IN, T)
    grp = jax.ShapeDtypeStruct((B, T, 256), F32)
    head = jax.ShapeDtypeStruct((B, N_HEADS, T, LANES), BF16)
    grp_spec = pl.BlockSpec((1, tm, 256), lambda b, t: (b, t, 0))
    head_spec = pl.BlockSpec((1, N_HEADS, tm, LANES), lambda b, t: (b, 0, t, 0))
    return pl.pallas_call(
        functools.partial(_in_proj_kernel, tm=tm),
        grid=(B, T // tm),
        in_specs=[pl.BlockSpec((1, tm, D), lambda b, t: (b, t, 0)),
                  pl.BlockSpec((1, N_MOD, D), lambda b, t: (b, 0, 0)),
                  pl.BlockSpec((1, D), lambda b, t: (0, 0)),
                  pl.BlockSpec((D, IN_COLS_PAD), lambda b, t: (0, 0)),
                  pl.BlockSpec((1, LANES), lambda b, t: (0, 0))],
        out_specs=[grp_spec, grp_spec, grp_spec, head_spec, head_spec, head_spec],
        out_shape=[grp, grp, grp, head, head, head],
        scratch_shapes=[pltpu.VMEM((1, LANES), F32)],
        compiler_params=_params(VMEM_LIMIT),
        name="in_proj",
    )(x, mod, g, w_aug, fb)


def _lru_kernel(lx_ref, lg_ref, cw_ref, cb_ref, wa_ref, ba_ref, wx_ref, bx_ref, lam_ref, gn_ref,
                y_ref, tail_ref, hcarry_ref, *, tc):
    @pl.when(pl.program_id(1) == 0)
    def _():
        tail_ref[...] = jnp.zeros_like(tail_ref)
        hcarry_ref[...] = jnp.zeros_like(hcarry_ref)

    x = lx_ref[0]
    xcat = jnp.concatenate([tail_ref[...], x], axis=0)
    cw = cw_ref[...]
    xr = cb_ref[...]
    for j in range(CONV_W):
        d = CONV_W - 1 - j
        xs = x if d == 0 else pltpu.roll(xcat, d, 0)[SUBLANES:SUBLANES + tc]
        xr = xr + xs * cw[j:j + 1]
    tail_ref[...] = x[tc - SUBLANES:tc]

    xb = xr.astype(BF16)
    r = jax.nn.sigmoid(jnp.dot(xb, wa_ref[...], preferred_element_type=F32) + ba_ref[...])
    i = jax.nn.sigmoid(jnp.dot(xb, wx_ref[...], preferred_element_type=F32) + bx_ref[...])
    nl = -lam_ref[...]
    softplus = jnp.maximum(nl, 0.0) + jnp.log1p(jnp.exp(-jnp.abs(nl)))
    log_a = -LRU_C * r * softplus
    a = jnp.exp(log_a)
    th = jnp.tanh(log_a)
    u = xr * i * jnp.sqrt(-2.0 * th / (1.0 - th))

    row = lax.broadcasted_iota(I32, (tc, LRU_W), 0)
    d = 1
    while d < tc:
        u = u + a * _shift_rows(u, d, row, 0.0)
        a = a * _shift_rows(a, d, row, 1.0)
        d *= 2
    h = u + a * hcarry_ref[...]
    hcarry_ref[...] = h[tc - 1:tc]

    y = h * jax.nn.gelu(lg_ref[0])
    y_ref[0] = _rms(y, gn_ref[...]).astype(BF16)


def _lru_branch(lx, lg, conv_w, conv_b, wa, ba, wx, bx, lam, gn):
    B, T, _ = lx.shape
    tc = min(TC_LRU, T)
    tile = pl.BlockSpec((1, tc, LRU_W), lambda b, t: (b, t, 0))
    vec = pl.BlockSpec((1, LRU_W), lambda b, t: (0, 0))
    mat = pl.BlockSpec((LRU_W, LRU_W), lambda b, t: (0, 0))
    return pl.pallas_call(
        functools.partial(_lru_kernel, tc=tc),
        grid=(B, T // tc),
        in_specs=[tile, tile, pl.BlockSpec((CONV_W, LRU_W), lambda b, t: (0, 0)), vec, mat, vec, mat, vec, vec, vec],
        out_specs=tile,
        out_shape=jax.ShapeDtypeStruct((B, T, LRU_W), BF16),
        scratch_shapes=[pltpu.VMEM((SUBLANES, LRU_W), F32), pltpu.VMEM((1, LRU_W), F32)],
        name="rg_lru",
    )(lx, lg, conv_w, conv_b, wa, ba, wx, bx, lam, gn)


def _s5_kernel(u_ref, bcat_ref, apr_ref, api_ref, p8r_ref, p8i_ref, bpr_ref, bpi_ref, ccat_ref, d_ref, gw_ref,
               gb_ref, gn_ref, y_ref, cr_ref, ci_ref, xsr_ref, xsi_ref, gr_ref, gi_ref, *, tc):
    @pl.when(pl.program_id(1) == 0)
    def _():
        cr_ref[...] = jnp.zeros_like(cr_ref)
        ci_ref[...] = jnp.zeros_like(ci_ref)

    def cmul_add(xr, xi, ar, ai, sr, si):
        return xr + (ar * sr - ai * si), xi + (ar * si + ai * sr)

    u = u_ref[0]
    bu = jnp.dot(u.astype(BF16), bcat_ref[...], preferred_element_type=F32)
    xr = bu[:, :S5_STATES]
    xi = bu[:, S5_STATES:]
    nb = tc // SUBLANES
    blocks = (nb, SUBLANES, S5_STATES)
    sub = lax.broadcasted_iota(I32, blocks, 1)
    xr = xr.reshape(blocks)
    xi = xi.reshape(blocks)
    for k in range(3):
        d = 1 << k
        ar = apr_ref[k:k + 1, :].reshape(1, 1, S5_STATES)
        ai = api_ref[k:k + 1, :].reshape(1, 1, S5_STATES)
        xr, xi = cmul_add(xr, xi, ar, ai, jnp.where(sub >= d, pltpu.roll(xr, d, 1), 0.0),
                          jnp.where(sub >= d, pltpu.roll(xi, d, 1), 0.0))
    xr = xr.reshape(tc, S5_STATES)
    xi = xi.reshape(tc, S5_STATES)
    lane_tiles = range(S5_STATES // LANES)

    def put(ref, v):
        for l in lane_tiles:
            ref[l] = v[:, l * LANES:(l + 1) * LANES]

    put(xsr_ref, xr)
    put(xsi_ref, xi)
    er =jnp.concatenate([xsr_ref[l, pl.ds(SUBLANES - 1, nb, stride=SUBLANES), :] for l in lane_tiles], axis=1)
    ei = jnp.concatenate([xsi_ref[l, pl.ds(SUBLANES - 1, nb, stride=SUBLANES), :] for l in lane_tiles], axis=1)
    blk = lax.broadcasted_iota(I32, (nb, S5_STATES), 0)
    k, d = 3, 1
    while d < nb:
        er, ei = cmul_add(er, ei, apr_ref[k:k + 1, :], api_ref[k:k + 1, :],
                          _shift_rows(er, d, blk, 0.0), _shift_rows(ei, d, blk, 0.0))
        k, d = k + 1, d * 2
    gr, gi = cmul_add(_shift_rows(er, 1, blk, 0.0), _shift_rows(ei, 1, blk, 0.0),
                      bpr_ref[...], bpi_ref[...], cr_ref[...], ci_ref[...])
    for j in range(SUBLANES):
        for l in lane_tiles:
            gr_ref[l, pl.ds(j, nb, stride=SUBLANES), :] = gr[:, l * LANES:(l + 1) * LANES]
            gi_ref[l, pl.ds(j, nb, stride=SUBLANES), :] = gi[:, l * LANES:(l + 1) * LANES]
    xr, xi = cmul_add(xr, xi, pltpu.repeat(p8r_ref[...], nb, 0), pltpu.repeat(p8i_ref[...], nb, 0),
                      jnp.concatenate([gr_ref[l] for l in lane_tiles], axis=1),
                      jnp.concatenate([gi_ref[l] for l in lane_tiles], axis=1))
    cr_ref[...] = xr[tc - 1:tc]
    ci_ref[...] = xi[tc - 1:tc]

    xcat = jnp.concatenate([xr, xi], axis=1).astype(BF16)
    y = jnp.dot(xcat, ccat_ref[...], preferred_element_type=F32) + d_ref[...] * u
    y = jax.nn.gelu(y)
    y = y * jax.nn.sigmoid(jnp.dot(y.astype(BF16), gw_ref[...], preferred_element_type=F32) + gb_ref[...])
    y_ref[0] = _rms(y, gn_ref[...]).astype(BF16)


def _s5_branch(u, bcat, apow_re, apow_im, p8_re, p8_im, bp_re, bp_im, ccat, d, glu_w, glu_b, gn, tc):
    B, T, _ = u.shape
    tile = pl.BlockSpec((1, tc, S5_W), lambda b, t: (b, t, 0))
    vec = pl.BlockSpec((1, S5_W), lambda b, t: (0, 0))

    def full(a):
        return pl.BlockSpec(a.shape, lambda b, t: (0, 0))

    return pl.pallas_call(
        functools.partial(_s5_kernel, tc=tc),
        grid=(B, T // tc),
        in_specs=[tile, full(bcat), full(apow_re), full(apow_im), full(p8_re), full(p8_im), full(bp_re),
                  full(bp_im), full(ccat), vec, full(glu_w), vec, vec],
        out_specs=tile,
        out_shape=jax.ShapeDtypeStruct((B, T, S5_W), BF16),
        scratch_shapes=[pltpu.VMEM((1, S5_STATES), F32), pltpu.VMEM((1, S5_STATES), F32)]
        + [pltpu.VMEM((S5_STATES // LANES, tc, LANES), F32)] * 4,
        compiler_params=_params(VMEM_LIMIT),
        name="s5",
    )(u, bcat, apow_re, apow_im, p8_re, p8_im, bp_re, bp_im, ccat, d, glu_w, glu_b, gn)


def _s5_tables(a_re, a_im, b_re, b_im, c_re, c_im, log_dt, tc):
    dt = jnp.exp(log_dt)[:, None]
    mag = jnp.exp(a_re * dt)
    abar_re = mag * jnp.cos(a_im * dt)
    abar_im = mag * jnp.sin(a_im * dt)
    den = a_re * a_re + a_im * a_im
    num_re = abar_re - 1.0
    k_re = (num_re * a_re + abar_im * a_im) / den
    k_im = (abar_im * a_re - num_re * a_im) / den
    bbar_re = k_re[..., None] * b_re - k_im[..., None] * b_im
    bbar_im = k_re[..., None] * b_im + k_im[..., None] * b_re
    eye = jnp.eye(S5_GROUPS, dtype=F32)
    bd_re = jnp.einsum('gpc,gh->gchp', bbar_re, eye).reshape(S5_W, S5_STATES)
    bd_im = jnp.einsum('gpc,gh->gchp', bbar_im, eye).reshape(S5_W, S5_STATES)
    bcat = jnp.concatenate([bd_re, bd_im], axis=1).astype(BF16)
    cd_re = jnp.einsum('gcp,gh->gphc', c_re, eye).reshape(S5_STATES, S5_W)
    cd_im = jnp.einsum('gcp,gh->gphc', c_im, eye).reshape(S5_STATES, S5_W)
    ccat = jnp.concatenate([cd_re, -cd_im], axis=0).astype(BF16)
    ar = abar_re.reshape(1, S5_STATES)
    ai = abar_im.reshape(1, S5_STATES)
    pows_r, pows_i = [ar], [ai]
    tab_r, tab_i = ar, ai
    n = 1
    while n < tc:
        sr, si = pows_r[-1], pows_i[-1]
        tab_r, tab_i = (jnp.concatenate([tab_r, tab_r * sr - tab_i * si], axis=0),
                        jnp.concatenate([tab_i, tab_r * si + tab_i * sr], axis=0))
        pows_r.append(sr * sr - si * si)
        pows_i.append(2.0 * sr * si)
        n *= 2
    levels = len(pows_r) - 1
    pad = (-levels) % SUBLANES
    apow_re = jnp.concatenate(pows_r[:levels] + [jnp.zeros((pad, S5_STATES), F32)], axis=0)
    apow_im = jnp.concatenate(pows_i[:levels] + [jnp.zeros((pad, S5_STATES), F32)], axis=0)
    nb = tc // SUBLANES
    bp_re = jnp.concatenate([jnp.ones((1, S5_STATES), F32), tab_r[SUBLANES - 1::SUBLANES][:nb - 1]], axis=0)
    bp_im = jnp.concatenate([jnp.zeros((1, S5_STATES), F32), tab_i[SUBLANES - 1::SUBLANES][:nb - 1]], axis=0)
    return bcat, apow_re, apow_im, tab_r[:SUBLANES], tab_i[:SUBLANES], bp_re, bp_im, ccat


def _attn_kernel(qt_ref, k_ref, vt_ref, o_ref, p_ref, *, tq):
    qi = pl.program_id(2)
    key = lax.broadcasted_iota(I32, (tq, tq), 0)
    qry = lax.broadcasted_iota(I32, (tq, tq), 1)
    heads = range(ATTN_HEADS_PER_STEP)
    qts =[qt_ref[0, hh] for hh in heads]

    def scores(hh, kj):
        k = k_ref[0, hh, pl.ds(pl.multiple_of(kj * tq, tq), tq), :]
        return jnp.dot(k, qts[hh], preferred_element_type=F32)

    def weighted_values(hh, kj, p):
        vt = vt_ref[0, hh, :, pl.ds(pl.multiple_of(kj * tq, tq), tq)]
        return jnp.dot(vt, p, preferred_element_type=F32)

    def softmax_step(s, m):
        m_new = jnp.maximum(m, jnp.max(s, axis=0, keepdims=True))
        return m_new, jnp.exp(m - m_new), jnp.exp(s - m_new).astype(BF16)

    p_ref[...] = jnp.zeros_like(p_ref)

    def body(kj, carry):
        ss = [scores(hh, kj) for hh in heads]
        pvs = [weighted_values(hh, jnp.maximum(kj - 1, 0), p_ref[hh]) for hh in heads]
        out = []
        for hh in heads:
            m, alpha, acc = carry[hh]
            acc = alpha * acc + pvs[hh]
            m, alpha, p = softmax_step(ss[hh], m)
            p_ref[hh] = p
            out.append((m, alpha, acc))
        return tuple(out)

    init = tuple((jnp.full((1, tq), NEG_BIG, F32), jnp.ones((1, tq), F32), jnp.zeros((LANES, tq), F32))
                 for _ in heads)
    carry = lax.fori_loop(0, qi, body, init)
    outs = []
    for hh in heads:
        m, alpha, acc = carry[hh]
        acc = alpha * acc + weighted_values(hh, jnp.maximum(qi - 1, 0), p_ref[hh])
        s = jnp.where(key <= qry, scores(hh, qi), NEG_BIG)
        m, alpha, p = softmax_step(s, m)
        acc = alpha * acc + weighted_values(hh, qi, p)
        out_t = acc / acc[HEAD_DIM:HEAD_DIM + 1, :]
        outs.append(out_t.T[:, :HEAD_DIM])
    o_ref[0] = jnp.concatenate(outs, axis=1)


def _attention(qa, ka, va):
    B, H, T, _ = qa.shape
    tq = min(TQ, T)
    hs = ATTN_HEADS_PER_STEP
    qt = jnp.swapaxes(qa, 2, 3)
    vt = jnp.swapaxes(va, 2, 3)
    return pl.pallas_call(
        functools.partial(_attn_kernel, tq=tq),
        grid=(B, H // hs, T // tq),
        in_specs=[pl.BlockSpec((1, hs, LANES, tq), lambda b, hp, qi: (b, hp, 0, qi)),
                  pl.BlockSpec((1, hs, T, LANES), lambda b, hp, qi: (b, hp, 0, 0)),
                  pl.BlockSpec((1, hs, LANES, T), lambda b, hp, qi: (b, hp, 0, 0))],
        out_specs=pl.BlockSpec((1, tq, hs * HEAD_DIM), lambda b, hp, qi: (b, qi, hp)),
        out_shape=jax.ShapeDtypeStruct((B, T, ATTN_W), F32),
        scratch_shapes=[pltpu.VMEM((hs, tq, tq), BF16)],
        compiler_params=_params(VMEM_LIMIT),
        name="fox_attention",
    )(qt, ka, vt)


def _out_proj_kernel(yl_ref, ys_ref, ya_ref, x_ref, mod_ref, gna_ref, wout_ref, pmg_ref, pfg_ref,
                     rwh_ref, rwl_ref, rb_ref,
                     x1_ref, h2_ref, idx_ref, gate_ref, rank_ref, cnt_ref, carry_ref, *, tm):
    @pl.when((pl.program_id(0) == 0) & (pl.program_id(1) == 0))
    def _():
        carry_ref[...] = jnp.zeros_like(carry_ref)

    mod = mod_ref[0]
    ya = _rms(ya_ref[0], gna_ref[...])
    ycat = jnp.concatenate([yl_ref[0], ys_ref[0], ya.astype(BF16)], axis=1)
    y = jnp.dot(ycat, wout_ref[...], preferred_element_type=F32)
    x1 = x_ref[0] + mod[2:3] * _rms(y, pmg_ref[...])
    x1_ref[0] = x1
    h2 = _rms(x1, pfg_ref[...]) * (1.0 + mod[4:5]) + mod[3:4]
    for c in range(ROW_CHUNKS):
        h2_ref[pl.ds(c, tm, stride=ROW_CHUNKS), :] = h2[:, c * LANES:(c + 1) * LANES]

    hh = h2.astype(BF16)
    hl = (h2 - hh.astype(F32)).astype(BF16)
    rwh = rwh_ref[...]
    logits = (jnp.dot(hh, rwh, preferred_element_type=F32) + jnp.dot(hl, rwh, preferred_element_type=F32)
              + jnp.dot(hh, rwl_ref[...], preferred_element_type=F32) + rb_ref[...])
    lane = lax.broadcasted_iota(I32, (tm, LANES), 1)
    lane_f = lane.astype(F32)
    work = jnp.where(lane < N_EXPERTS, logits, NEG_BIG)
    vals, idxs = [], []
    for _ in range(TOP_K):
        mx = jnp.max(work, axis=1, keepdims=True)
        ik = jnp.min(jnp.where(work == mx, lane_f, float(LANES)), axis=1, keepdims=True)
        vals.append(mx)
        idxs.append(ik)
        work = jnp.where(lane_f == ik, 2.0 * NEG_BIG, work)
    es = [jnp.exp(v - vals[0]) for v in vals]
    den = es[0] + es[1] + es[2] + es[3]

    onehot = jnp.zeros((tm, LANES), F32)
    for ik in idxs:
        onehot = onehot + jnp.where(lane_f == ik, 1.0, 0.0)
    r_i = lax.broadcasted_iota(I32, (tm, tm), 0)
    c_i = lax.broadcasted_iota(I32, (tm, tm), 1)
    below = jnp.where(c_i < r_i, 1.0, 0.0).astype(BF16)
    prior = jnp.dot(below, onehot.astype(BF16), preferred_element_type=F32) + carry_ref[...]
    carry = carry_ref[...] + jnp.sum(onehot, axis=0, keepdims=True)
    carry_ref[...] = carry
    cnt_ref[...] = carry.astype(I32)

    idx_full = jnp.zeros((tm, LANES), F32)
    gate_full = jnp.zeros((tm, LANES), F32)
    rank_full = jnp.zeros((tm, LANES), F32)
    for k in range(TOP_K):
        rk = jnp.sum(jnp.where(lane_f == idxs[k], prior, 0.0), axis=1, keepdims=True)
        idx_full = jnp.where(lane == k, idxs[k], idx_full)
        gate_full = jnp.where(lane == k, es[k] / den, gate_full)
        rank_full = jnp.where(lane == k, rk, rank_full)
    idx_ref[...] = idx_full[:, :TOP_K].astype(I32)
    gate_ref[...] = gate_full[:, :TOP_K]
    rank_ref[...] = rank_full[:, :TOP_K].astype(I32)


def _out_proj(yl, ys, ya, x, mod, gna, wout, pmg, pfg, rwh, rwl, rb):
    B, T, _ = x.shape
    tm = min(TM_OUT, T)
    nt = T // tm
    N = B * T

    def vec(n):
        return pl.BlockSpec((1, n), lambda b, t: (0, 0))

    tok = lambda b, t: (b * nt + t, 0)
    return pl.pallas_call(
        functools.partial(_out_proj_kernel, tm=tm),
        grid=(B, nt),
        in_specs=[pl.BlockSpec((1, tm, LRU_W), lambda b, t: (b, t, 0)),
                  pl.BlockSpec((1, tm, S5_W), lambda b, t: (b, t, 0)),
                  pl.BlockSpec((1, tm, ATTN_W), lambda b, t: (b, t, 0)),
                  pl.BlockSpec((1, tm, D), lambda b, t: (b, t, 0)),
                  pl.BlockSpec((1, N_MOD, D), lambda b, t: (b, 0, 0)),
                  vec(ATTN_W),
                  pl.BlockSpec((D, D), lambda b, t: (0, 0)),
                  vec(D), vec(D),
                  pl.BlockSpec((D, LANES), lambda b, t: (0, 0)),
                  pl.BlockSpec((D, LANES), lambda b, t: (0, 0)),
                  vec(LANES)],
        out_specs=[pl.BlockSpec((1, tm, D), lambda b, t: (b, t, 0)),
                   pl.BlockSpec((tm * ROW_CHUNKS, LANES), tok),
                   pl.BlockSpec((tm, TOP_K), tok),
                   pl.BlockSpec((tm, TOP_K), tok),
                   pl.BlockSpec((tm, TOP_K), tok),
                   pl.BlockSpec((1, LANES), lambda b, t: (0, 0))],
        out_shape=[jax.ShapeDtypeStruct((B, T, D), F32),
                   jax.ShapeDtypeStruct((N * ROW_CHUNKS, LANES), F32),
                   jax.ShapeDtypeStruct((N, TOP_K), I32),
                   jax.ShapeDtypeStruct((N, TOP_K), F32),
                   jax.ShapeDtypeStruct((N, TOP_K), I32),
                   jax.ShapeDtypeStruct((1, LANES), I32)],
        scratch_shapes=[pltpu.VMEM((1, LANES), F32)],
        compiler_params=_params(VMEM_LIMIT),
        name="out_proj_router",
    )(yl, ys, ya, x, mod, gna, wout, pmg, pfg, rwh, rwl, rb)


def _row_copy(src, src_row, dst, dst_row, sem):
    return pltpu.make_async_copy(src.at[pl.ds(pl.multiple_of(src_row * ROW_CHUNKS, ROW_CHUNKS), ROW_CHUNKS), :],
                                 dst.at[pl.ds(pl.multiple_of(dst_row * ROW_CHUNKS, ROW_CHUNKS), ROW_CHUNKS), :],
                                 sem)


def _dispatch_kernel(start_ref, cnt_ref, pad_ref, idx_ref, rank_ref, h_ref, dest_ref, xb_ref, sem, *, tn):
    i = pl.program_id(0)

    def issue(j, _):
        for k in range(TOP_K):
            a = j * TOP_K + k
            dst = start_ref[idx_ref[a]] + rank_ref[a]
            dest_ref[a] = dst
            _row_copy(h_ref, j, xb_ref, dst, sem).start()
        return 0

    lax.fori_loop(0, tn, issue, 0)

    def drain(j, _):
        for k in range(TOP_K):
            _row_copy(h_ref, 0, xb_ref, 0, sem).wait()
        return 0

    lax.fori_loop(0, tn, drain, 0)

    @pl.when(i == pl.num_programs(0) - 1)
    def _():
        def per_expert(e, _):
            lo = start_ref[e] + cnt_ref[e]
            hi = start_ref[e] + pad_ref[e]

            def fill(r, _):
                _row_copy(h_ref, 0, xb_ref, r, sem).start()
                return 0

            lax.fori_loop(lo, hi, fill, 0)

            def fill_wait(r, _):
                _row_copy(h_ref, 0, xb_ref, r, sem).wait()
                return 0

            lax.fori_loop(lo, hi, fill_wait, 0)
            return 0

        lax.fori_loop(0, N_EXPERTS, per_expert, 0)


def _dispatch(starts, counts, padded, idx_flat, rank_flat, h_rows, n_rows):
    NK = idx_flat.shape[0]
    N = NK // TOP_K
    tn = min(TN_ROWS, N)
    smem_blk = pl.BlockSpec((tn * TOP_K,), lambda i, *_: (i,), memory_space=pltpu.SMEM)
    grid_spec = pltpu.PrefetchScalarGridSpec(
        num_scalar_prefetch=3,
        grid=(N // tn,),
        in_specs=[smem_blk, smem_blk, pl.BlockSpec((tn * ROW_CHUNKS, LANES), lambda i, *_: (i, 0))],
        out_specs=[smem_blk, pl.BlockSpec(memory_space=pl.ANY)],
        scratch_shapes=[pltpu.SemaphoreType.DMA(())],
    )
    return pl.pallas_call(
        functools.partial(_dispatch_kernel, tn=tn),
        grid_spec=grid_spec,
        out_shape=[jax.ShapeDtypeStruct((NK,), I32),
                   jax.ShapeDtypeStruct((n_rows * ROW_CHUNKS, LANES), F32)],
        name="moe_dispatch",
    )(starts, counts, padded, idx_flat, rank_flat, h_rows)


def _expert_kernel(be_ref, nused_ref, x_ref, wgu_ref, bgu_ref, wdn_ref, bdn_ref, y_ref, wgu_s, wdn_s, *, rows):
    b = pl.program_id(0)
    prev = be_ref[jnp.maximum(b - 1, 0)]

    @pl.when((b == 0) | (be_ref[b] != prev))
    def _():
        wgu_s[...] = wgu_ref[0, 0].astype(BF16)
        wdn_s[...] = wdn_ref[0, 0].astype(BF16)

    @pl.when(b < nused_ref[0])
    def _():
        x = jnp.concatenate([x_ref[pl.ds(c, rows, stride=ROW_CHUNKS), :] for c in range(ROW_CHUNKS)], axis=1)
        gu = jnp.dot(x.astype(BF16), wgu_s[...], preferred_element_type=F32) + bgu_ref[0, 0]
        g = jnp.minimum(gu[:, :D_FF], SWIGLU_LIMIT)
        up = jnp.clip(gu[:, D_FF:], -SWIGLU_LIMIT, SWIGLU_LIMIT)
        act = (up + 1.0) * (g * jax.nn.sigmoid(SWIGLU_ALPHA * g))
        y = jnp.dot(act.astype(BF16), wdn_s[...], preferred_element_type=F32) + bdn_ref[0, 0]
        for c in range(ROW_CHUNKS):
            y_ref[pl.ds(c, rows, stride=ROW_CHUNKS), :] = y[:, c * LANES:(c + 1) * LANES]


def _experts(layer, block_e, n_used, xb, w_gu, b_gu, w_dn, b_dn, rows):
    n_blocks = block_e.shape[0]
    L, E = w_gu.shape[:2]
    blk = lambda b, be, nu: (jnp.minimum(b, nu[0] - 1), 0)
    wsel = lambda b, be, nu: (layer, be[b], 0, 0)
    grid_spec = pltpu.PrefetchScalarGridSpec(
        num_scalar_prefetch=2,
        grid=(n_blocks,),
        in_specs=[pl.BlockSpec((rows * ROW_CHUNKS, LANES), blk),
                  pl.BlockSpec((1, 1, D, 2 * D_FF), wsel),
                  pl.BlockSpec((1, 1, 1, 2 * D_FF), wsel),
                  pl.BlockSpec((1, 1, D_FF, D), wsel),
                  pl.BlockSpec((1, 1, 1, D), wsel)],
        out_specs=pl.BlockSpec((rows * ROW_CHUNKS, LANES), blk),
        scratch_shapes=[pltpu.VMEM((D, 2 * D_FF), BF16), pltpu.VMEM((D_FF, D), BF16)],
    )
    return pl.pallas_call(
        functools.partial(_expert_kernel, rows=rows),
        grid_spec=grid_spec,
        out_shape=jax.ShapeDtypeStruct(xb.shape, F32),
        compiler_params=_params(VMEM_LIMIT),
        name="moe_experts",
    )(block_e, n_used, xb, w_gu, b_gu.reshape(L, E, 1, 2 * D_FF), w_dn, b_dn.reshape(L, E, 1, D))


def _combine_kernel(dest_ref, yb_ref, gate_ref, x1_ref, mod_ref, g_ref, o_ref, buf_ref, sem, *, tn):
    def issue(j, _):
        for k in range(TOP_K):
            pltpu.make_async_copy(
                yb_ref.at[pl.ds(pl.multiple_of(dest_ref[j * TOP_K + k] * ROW_CHUNKS, ROW_CHUNKS), ROW_CHUNKS), :],
                buf_ref.at[k, pl.ds(pl.multiple_of(j * ROW_CHUNKS, ROW_CHUNKS), ROW_CHUNKS), :], sem).start()
        return 0

    lax.fori_loop(0, tn, issue, 0)

    def drain(j, _):
        for k in range(TOP_K):
            pltpu.make_async_copy(yb_ref.at[pl.ds(0, ROW_CHUNKS), :],
                                  buf_ref.at[k, pl.ds(0, ROW_CHUNKS), :], sem).wait()
        return 0

    lax.fori_loop(0, tn, drain, 0)

    gates = gate_ref[...]
    y = jnp.zeros((tn, D), F32)
    for k in range(TOP_K):
        yk = jnp.concatenate([buf_ref[k, pl.ds(c, tn, stride=ROW_CHUNKS), :] for c in range(ROW_CHUNKS)], axis=1)
        y = y + gates[:, k:k + 1] * yk
    mod = mod_ref[0]
    o_ref[0] = x1_ref[0] + mod[5:6] * _rms(y, g_ref[...])


def _combine(dest_flat, yb, gates, x1, mod, g):
    B, T, _ = x1.shape
    tn = min(TN_COMB, T)
    nt = T // tn
    grid_spec = pltpu.PrefetchScalarGridSpec(
        num_scalar_prefetch=0,
        grid=(B, nt),
        in_specs=[pl.BlockSpec((tn * TOP_K,), lambda b, t: (b * nt + t,), memory_space=pltpu.SMEM),
                  pl.BlockSpec(memory_space=pl.ANY),
                  pl.BlockSpec((tn, TOP_K), lambda b, t: (b * nt + t, 0)),
                  pl.BlockSpec((1, tn, D), lambda b, t: (b, t, 0)),
                  pl.BlockSpec((1, N_MOD, D), lambda b, t: (b, 0, 0)),
                  pl.BlockSpec((1, D), lambda b, t: (0, 0))],
        out_specs=pl.BlockSpec((1, tn, D), lambda b, t: (b, t, 0)),
        scratch_shapes=[pltpu.VMEM((TOP_K, tn * ROW_CHUNKS, LANES), F32), pltpu.SemaphoreType.DMA(())],
    )
    return pl.pallas_call(
        functools.partial(_combine_kernel, tn=tn),
        grid_spec=grid_spec,
        out_shape=jax.ShapeDtypeStruct((B, T, D), F32),
        name="moe_combine",
    )(dest_flat, yb, gates, x1, mod, g)


def _fused_expert_kernel(be_ref, nused_ref, src_ref, dst_ref, h_ref, wgu_ref, bgu_ref, wdn_ref, bdn_ref, y_ref,
                         xbuf, ybuf, wgu_s, wdn_s, gsem, ssem, *, rows, n_blocks, n_assign):
    b = pl.program_id(0)
    nu = nused_ref[0]
    slot = lax.rem(b, 2)
    other = 1 - slot
    xslot = lax.rem(b, 3)
    xnext = lax.rem(b + 2, 3)
    src_off = lax.rem(jnp.minimum(b + 2, n_blocks - 1) * rows, MAP_BLOCK)
    dst_off = lax.rem(jnp.maximum(b - 1, 0) * rows, MAP_BLOCK)

    def gather(i, tok, buf_slot):
        return pltpu.make_async_copy(
            h_ref.at[pl.ds(pl.multiple_of(tok * ROW_CHUNKS, ROW_CHUNKS), ROW_CHUNKS), :],
            xbuf.at[buf_slot, pl.ds(i * ROW_CHUNKS, ROW_CHUNKS), :], gsem.at[buf_slot])

    def scatter(i, dst, buf_slot):
        return pltpu.make_async_copy(
            ybuf.at[buf_slot, pl.ds(i * ROW_CHUNKS, ROW_CHUNKS), :],
            y_ref.at[pl.ds(pl.multiple_of(dst * ROW_CHUNKS, ROW_CHUNKS), ROW_CHUNKS), :], ssem.at[buf_slot])

    @pl.when(b == 0)
    def _():
        ybuf[...] = jnp.zeros(ybuf.shape, F32)
        for i in range(rows):
            gather(i, src_ref[i], 0).start()
            gather(i, src_ref[rows + i], 1).start()
            scatter(i, n_assign + 2 * rows + i, 0).start()

    @pl.when(b <= nu)
    def _():
        for i in range(rows):
            gather(i, 0, xslot).wait()

    prev = be_ref[jnp.maximum(b - 1, 0)]

    @pl.when((b == 0) | (be_ref[b] != prev))
    def _():
        wgu_s[...] = wgu_ref[0, 0].astype(BF16)
        wdn_s[...] = wdn_ref[0, 0].astype(BF16)

    @pl.when(b < nu)
    def _():
        for i in range(rows):
            gather(i, src_ref[src_off + i], xnext).start(priority=i % 2)
        for i in range(rows):
            dst = jnp.where(b == 0, n_assign + 3 * rows + i, dst_ref[dst_off + i])
            scatter(i, dst, other).start(priority=i % 2)
        x = jnp.concatenate([xbuf[xslot, pl.ds(c, rows, stride=ROW_CHUNKS), :] for c in range(ROW_CHUNKS)], axis=1)
        gu = jnp.dot(x.astype(BF16), wgu_s[...], preferred_element_type=F32) + bgu_ref[0, 0]
        g = jnp.minimum(gu[:, :D_FF], SWIGLU_LIMIT)
        up = jnp.clip(gu[:, D_FF:], -SWIGLU_LIMIT, SWIGLU_LIMIT)
        act = (up + 1.0) * (g * jax.nn.sigmoid(SWIGLU_ALPHA * g))
        y = jnp.dot(act.astype(BF16), wdn_s[...], preferred_element_type=F32) + bdn_ref[0, 0]
        for i in range(rows):
            scatter(i, 0, slot).wait()
        for c in range(ROW_CHUNKS):
            ybuf[slot, pl.ds(c, rows, stride=ROW_CHUNKS), :] = y[:, c * LANES:(c + 1) * LANES]

    @pl.when(b == nu)
    def _():
        for i in range(rows):
            gather(i, 0, lax.rem(b + 1, 3)).wait()
        for i in range(rows):
            scatter(i, 0, slot).wait()
        for i in range(rows):
            scatter(i, dst_ref[dst_off + i], other).start()
        for i in range(rows):
            scatter(i, 0, other).wait()


def _fused_experts(layer, block_e, n_used, src_tok, dst_row, h_rows, w_gu, b_gu, w_dn, b_dn, rows, n_assign):
    n_blocks = block_e.shape[0]
    L, E = w_gu.shape[:2]
    wsel = lambda b, be, nu: (layer, be[b], 0, 0)
    grid_spec = pltpu.PrefetchScalarGridSpec(
        num_scalar_prefetch=2,
        grid=(n_blocks,),
        in_specs=[pl.BlockSpec((MAP_BLOCK,), lambda b, be, nu: (jnp.minimum(b + 2, n_blocks - 1) * rows // MAP_BLOCK,),
                               memory_space=pltpu.SMEM),
                  pl.BlockSpec((MAP_BLOCK,), lambda b, be, nu: (jnp.maximum(b - 1, 0) * rows // MAP_BLOCK,),
                               memory_space=pltpu.SMEM),
                  pl.BlockSpec(memory_space=pl.ANY),
                  pl.BlockSpec((1, 1, D, 2 * D_FF), wsel),
                  pl.BlockSpec((1, 1, 1, 2 * D_FF), wsel),
                  pl.BlockSpec((1, 1, D_FF, D), wsel),
                  pl.BlockSpec((1, 1, 1, D), wsel)],
        out_specs=pl.BlockSpec(memory_space=pl.ANY),
        scratch_shapes=[pltpu.VMEM((3, rows * ROW_CHUNKS, LANES), F32),
                        pltpu.VMEM((2, rows * ROW_CHUNKS, LANES), F32),
                        pltpu.VMEM((D, 2 * D_FF), BF16), pltpu.VMEM((D_FF, D), BF16),
                        pltpu.SemaphoreType.DMA((3,)), pltpu.SemaphoreType.DMA((2,))],
    )
    return pl.pallas_call(
        functools.partial(_fused_expert_kernel, rows=rows, n_blocks=n_blocks, n_assign=n_assign),
        grid_spec=grid_spec,
        out_shape=jax.ShapeDtypeStruct(((n_assign + 4 * rows) * ROW_CHUNKS, LANES), F32),
        compiler_params=_params(VMEM_LIMIT),
        name="moe_experts",
    )(block_e, n_used, src_tok, dst_row, h_rows, w_gu, b_gu.reshape(L, E, 1, 2 * D_FF), w_dn,
      b_dn.reshape(L, E, 1, D))


def _dense_combine_kernel(y_ref, gate_ref, x1_ref, mod_ref, g_ref, o_ref, *, tn):
    gates = gate_ref[...]
    y = jnp.zeros((tn, D), F32)
    for k in range(TOP_K):
        yk = jnp.concatenate(
            [y_ref[pl.ds(k * ROW_CHUNKS + c, tn, stride=TOP_K * ROW_CHUNKS), :] for c in range(ROW_CHUNKS)], axis=1)
        y = y + gates[:, k:k + 1] * yk
    mod = mod_ref[0]
    o_ref[0] = x1_ref[0] + mod[5:6] * _rms(y, g_ref[...])


def _dense_combine(y_rows, gates, x1, mod, g):
    B, T, _ = x1.shape
    tn = min(TN_COMB, T)
    nt = T // tn
    return pl.pallas_call(
        functools.partial(_dense_combine_kernel, tn=tn),
        grid=(B, nt),
        in_specs=[pl.BlockSpec((tn * TOP_K * ROW_CHUNKS, LANES), lambda b, t: (b * nt + t, 0)),
                  pl.BlockSpec((tn, TOP_K), lambda b, t: (b * nt + t, 0)),
                  pl.BlockSpec((1, tn, D), lambda b, t: (b, t, 0)),
                  pl.BlockSpec((1, N_MOD, D), lambda b, t: (b, 0, 0)),
                  pl.BlockSpec((1, D), lambda b, t: (0, 0))],
        out_specs=pl.BlockSpec((1, tn, D), lambda b, t: (b, t, 0)),
        out_shape=jax.ShapeDtypeStruct((B, T, D), F32),
        name="moe_combine",
    )(y_rows, gates, x1, mod, g)


def _head_slabs(w):
    w = w.reshape(D, N_HEADS, HEAD_DIM)
    return jnp.pad(w, ((0, 0), (0, 0), (0, LANES - HEAD_DIM))).reshape(D, N_HEADS * LANES)


def _block_diag(w):
    nb = w.shape[0]
    return jnp.einsum('hij,hg->higj', w, jnp.eye(nb, dtype=w.dtype)).reshape(LRU_W, LRU_W)


def kernel(x, c, ada_w, ada_b, pre_mix_g, w_in, conv_w, conv_b, lru_wa, lru_ba, lru_wx, lru_bx, lru_lambda, s5_a_re, s5_a_im, s5_b_re, s5_b_im, s5_c_re, s5_c_im, s5_d, s5_log_dt, s5_glu_w, s5_glu_b, fox_fb, gn_lru, gn_s5, gn_attn, w_out, post_mix_g, pre_ffn_g, router_w, router_b, w_gu, b_gu, w_dn, b_dn, post_ffn_g):
    B, T, _ = x.shape
    N = B * T
    L = ada_w.shape[0]
    tc_s5 = min(TC_S5, T)
    n_blocks = (N * TOP_K) // EXP_ROWS + N_EXPERTS
    n_rows = n_blocks * EXP_ROWS
    mod_all = _modulation(c, ada_w, ada_b).reshape(L, B, N_MOD, D)

    for l in range(L):
        mod = mod_all[l]
        wl = w_in[l]
        w_aug = jnp.concatenate(
            [wl[:, :O_Q], _head_slabs(wl[:, O_Q:O_K]), _head_slabs(wl[:, O_K:O_V]), _head_slabs(wl[:, O_V:O_F]),
             jnp.pad(wl[:, O_F:], ((0, 0), (0, LANES - N_HEADS)))], axis=1).astype(BF16)
        fb = jnp.pad(fox_fb[l], (0, LANES - N_HEADS)).reshape(1, LANES)
        lx, lg, su, qa, ka, va = _in_proj(x, mod, pre_mix_g[l].reshape(1, D), w_aug, fb)

        yl = _lru_branch(lx, lg, conv_w[l], conv_b[l].reshape(1, LRU_W),
                         _block_diag(lru_wa[l]).astype(BF16), lru_ba[l].reshape(1, LRU_W),
                         _block_diag(lru_wx[l]).astype(BF16), lru_bx[l].reshape(1, LRU_W),
                         lru_lambda[l].reshape(1, LRU_W), gn_lru[l].reshape(1, LRU_W))

        tabs = _s5_tables(s5_a_re[l], s5_a_im[l], s5_b_re[l], s5_b_im[l], s5_c_re[l], s5_c_im[l], s5_log_dt[l], tc_s5)
        ys = _s5_branch(su, *tabs, s5_d[l].reshape(1, S5_W), s5_glu_w[l].astype(BF16),
                        s5_glu_b[l].reshape(1, S5_W), gn_s5[l].reshape(1, S5_W), tc_s5)

        ya = _attention(qa, ka, va)

        rw = jnp.pad(router_w[l], ((0, 0), (0, LANES - N_EXPERTS)))
        rwh = rw.astype(BF16)
        rwl = (rw - rwh.astype(F32)).astype(BF16)
        rb = jnp.pad(router_b[l], (0, LANES - N_EXPERTS)).reshape(1, LANES)
        x1, h_rows, idx, gates, rank, cnt = _out_proj(
            yl, ys, ya, x, mod, gn_attn[l].reshape(1, ATTN_W), w_out[l].astype(BF16),
            post_mix_g[l].reshape(1, D), pre_ffn_g[l].reshape(1, D), rwh, rwl, rb)

        counts = cnt[0, :N_EXPERTS]
        blocks_e = (counts + EXP_ROWS - 1) // EXP_ROWS
        padded = blocks_e * EXP_ROWS
        block_ends = jnp.cumsum(blocks_e)
        starts = (block_ends - blocks_e) * EXP_ROWS
        n_used = block_ends[-1:]
        bid = jnp.minimum(jnp.arange(n_blocks, dtype=I32), n_used[0] - 1)
        block_e = jnp.minimum(jnp.sum((bid[:, None] >= block_ends[None, :]).astype(I32), axis=1), N_EXPERTS - 1)

        n_assign = N * TOP_K
        dest = (jnp.take(starts.astype(I32), idx) + rank).reshape(n_assign)
        row_assign = jnp.full((n_rows,), n_assign, I32).at[dest].set(
            jnp.arange(n_assign, dtype=I32), unique_indices=True)
        is_real = row_assign < n_assign
        src_tok = jnp.where(is_real, row_assign // TOP_K, 0)
        spare = n_assign + jnp.arange(n_rows, dtype=I32) % (2 * EXP_ROWS)
        dst_row = jnp.where(is_real, row_assign, spare)

        y_rows = _fused_experts(l, block_e, n_used.astype(I32), src_tok, dst_row, h_rows,
                                w_gu, b_gu, w_dn, b_dn, EXP_ROWS, n_assign)
        x = _dense_combine(y_rows, gates, x1, mod, post_ffn_g[l].reshape(1, D))
    return x
```

```python
import functools

import jax
import jax.numpy as jnp
from jax import lax
from jax.experimental import pallas as pl
from jax.experimental.pallas import tpu as pltpu

F32 = jnp.float32
BF16 = jnp.bfloat16
I32 = jnp.int32

D = 1024
LRU_W = 256
LRU_BLOCK_W = 64
CONV_W = 4
LRU_C = 8.0
S5_W = 256
S5_GROUP_W = 16
S5_GROUPS = 16
S5_STATE = 64
S5_STATES = S5_GROUPS * S5_STATE
HEAD_DIM = 64
ATTN_W = 512
N_HEADS = 8
O_LRU_G = 256
O_S5 = 512
O_Q = 768
O_K = O_Q + ATTN_W
O_V = O_K + ATTN_W
O_F = O_V + ATTN_W
N_EXPERTS = 32
TOP_K = 4
D_FF = 1024
SWIGLU_LIMIT = 7.0
SWIGLU_ALPHA = 1.702
N_MOD = 6
RMS_EPS = 1e-6

LANES = 128
SUBLANES = 8
VMEM_LIMIT = 56 * 1024 * 1024
ROW_CHUNKS = D // LANES

C_Q = 768
C_K = C_Q + N_HEADS * LANES
C_V = C_K + N_HEADS * LANES
C_F = C_V + N_HEADS * LANES
IN_COLS_PAD = C_F + LANES
L_A0, L_A1, L_A2, L_B0, L_B1, L_B2 = 64, 65, 66, 67, 68, 69

TM_IN = 256
TC_LRU = 256
TC_S5 = 128
TQ = 512
ATTN_HEADS_PER_STEP = 4
TM_OUT = 512
TN_ROWS = 1024
TN_COMB = 256
EXP_ROWS = 256
MAP_BLOCK = 1024
NEG_BIG = -1e30


def _rms(x, g):
    return x * lax.rsqrt(jnp.mean(x * x, axis=-1, keepdims=True) + RMS_EPS) * g


def _shift_rows(v, d, row, fill):
    return jnp.where(row >= d, pltpu.roll(v, d, 0), fill)


def _params(vmem=None):
    return pltpu.CompilerParams(vmem_limit_bytes=vmem) if vmem else None


def _mod_kernel(c_ref, w_ref, b_ref, o_ref):
    c = c_ref[...]
    s = c * jax.nn.sigmoid(c)
    o_ref[0] = jnp.dot(s.astype(BF16), w_ref[0].astype(BF16), preferred_element_type=F32) + b_ref[0]


def _modulation(c, ada_w, ada_b):
    L, _, W = ada_w.shape
    B = c.shape[0]
    return pl.pallas_call(
        _mod_kernel,
        grid=(L, W // D),
        in_specs=[pl.BlockSpec((B, D), lambda l, j: (0, 0)),
                  pl.BlockSpec((1, D, D), lambda l, j: (l, 0, j)),
                  pl.BlockSpec((1, 1, D), lambda l, j: (l, 0, j))],
        out_specs=pl.BlockSpec((1, B, D), lambda l, j: (l, 0, j)),
        out_shape=jax.ShapeDtypeStruct((L, B, W), F32),
        name="adaln_mod",
    )(c, ada_w, ada_b.reshape(L, 1, W))


def _in_proj_kernel(x_ref, mod_ref, g_ref, w_ref, fb_ref,
                    lx_ref, lg_ref, su_ref, q_ref, k_ref, v_ref, fcarry_ref, *, tm):
    @pl.when(pl.program_id(1) == 0)
    def _():
        fcarry_ref[...] = jnp.zeros_like(fcarry_ref)

    mod = mod_ref[0]
    h = _rms(x_ref[0], g_ref[...]) * (1.0 + mod[1:2]) + mod[0:1]
    p = jnp.dot(h.astype(BF16), w_ref[...], preferred_element_type=F32)
    lx_ref[0] = p[:, 0:O_LRU_G]
    lg_ref[0] = p[:, O_LRU_G:O_S5]
    su_ref[0] = p[:, O_S5:O_Q]

    z = p[:, C_F:C_F + LANES] + fb_ref[...]
    logf = jnp.minimum(z, 0.0) - jnp.log1p(jnp.exp(-jnp.abs(z)))
    row = lax.broadcasted_iota(I32, (tm, LANES), 0)
    d = 1
    while d < tm:
        logf = logf + _shift_rows(logf, d, row, 0.0)
        d *= 2
    fsum = logf + fcarry_ref[...]
    fcarry_ref[...] = fsum[tm - 1:tm, :]

    fexp = jnp.concatenate([jnp.broadcast_to(fsum[:, h:h + 1], (tm, LANES)) for h in range(N_HEADS)], axis=1)
    hi = fexp.astype(BF16).astype(F32)
    r1 = fexp - hi
    mid = r1.astype(BF16).astype(F32)
    lo = r1 - mid
    lane = lax.broadcasted_iota(I32, (tm, N_HEADS * LANES), 1) & (LANES - 1)
    pq = p[:, C_Q:C_K] * (HEAD_DIM ** -0.5)
    pk = p[:, C_K:C_V]
    pv = p[:, C_V:C_F]
    ones = (lane >= L_B0) & (lane <= L_B2)
    qa = jnp.where(lane == L_A0, hi, jnp.where(lane == L_A1, mid, jnp.where(lane == L_A2, lo,
                                                                            jnp.where(ones, 1.0, pq))))
    ones = (lane >= L_A0) & (lane <= L_A2)
    ka = jnp.where(lane == L_B0, -hi, jnp.where(lane == L_B1, -mid, jnp.where(lane == L_B2, -lo,
                                                                              jnp.where(ones, 1.0, pk))))
    va = jnp.where(lane == HEAD_DIM, 1.0, pv)
    for h in range(N_HEADS):
        sl = slice(h * LANES, (h + 1) * LANES)
        q_ref[0, h] = qa[:, sl].astype(BF16)
        k_ref[0, h] = ka[:, sl].astype(BF16)
        v_ref[0, h] = va[:, sl].astype(BF16)


def _in_proj(x, mod, g, w_aug, fb):
    B, T, _ = x.shape
    tm = min(TM_IN, T)
    grp = jax.ShapeDtypeStruct((B, T, 256), F32)
    head = jax.ShapeDtypeStruct((B, N_HEADS, T, LANES), BF16)
    grp_spec = pl.BlockSpec((1, tm, 256), lambda b, t: (b, t, 0))
    head_spec = pl.BlockSpec((1, N_HEADS, tm, LANES), lambda b, t: (b, 0, t, 0))
    return pl.pallas_call(
        functools.partial(_in_proj_kernel, tm=tm),
        grid=(B, T // tm),
        in_specs=[pl.BlockSpec((1, tm, D), lambda b, t: (b, t, 0)),
                  pl.BlockSpec((1, N_MOD, D), lambda b, t: (b, 0, 0)),
                  pl.BlockSpec((1, D), lambda b, t: (0, 0)),
                  pl.BlockSpec((D, IN_COLS_PAD), lambda b, t: (0, 0)),
                  pl.BlockSpec((1, LANES), lambda b, t: (0, 0))],
        out_specs=[grp_spec, grp_spec, grp_spec, head_spec, head_spec, head_spec],
        out_shape=[grp, grp, grp, head, head, head],
        scratch_shapes=[pltpu.VMEM((1, LANES), F32)],
        compiler_params=_params(VMEM_LIMIT),
        name="in_proj",
    )(x, mod, g, w_aug, fb)


def _lru_kernel(lx_ref, lg_ref, cw_ref, cb_ref, wa_ref, ba_ref, wx_ref, bx_ref, lam_ref, gn_ref,
                y_ref, tail_ref, hcarry_ref, *, tc):
    @pl.when(pl.program_id(1) == 0)
    def _():
        tail_ref[...] = jnp.zeros_like(tail_ref)
        hcarry_ref[...] = jnp.zeros_like(hcarry_ref)

    x = lx_ref[0]
    xcat = jnp.concatenate([tail_ref[...], x], axis=0)
    cw = cw_ref[...]
    xr = cb_ref[...]
    for j in range(CONV_W):
        d = CONV_W - 1 - j
        xs = x if d == 0 else pltpu.roll(xcat, d, 0)[SUBLANES:SUBLANES + tc]
        xr = xr + xs * cw[j:j + 1]
    tail_ref[...] = x[tc - SUBLANES:tc]

    xb = xr.astype(BF16)
    r = jax.nn.sigmoid(jnp.dot(xb, wa_ref[...], preferred_element_type=F32) + ba_ref[...])
    i = jax.nn.sigmoid(jnp.dot(xb, wx_ref[...], preferred_element_type=F32) + bx_ref[...])
    nl = -lam_ref[...]
    softplus = jnp.maximum(nl, 0.0) + jnp.log1p(jnp.exp(-jnp.abs(nl)))
    log_a = -LRU_C * r * softplus
    a = jnp.exp(log_a)
    th = jnp.tanh(log_a)
    u = xr * i * jnp.sqrt(-2.0 * th / (1.0 - th))

    row = lax.broadcasted_iota(I32, (tc, LRU_W), 0)
    d = 1
    while d < tc:
        u = u + a * _shift_rows(u, d, row, 0.0)
        a = a * _shift_rows(a, d, row, 1.0)
        d *= 2
    h = u + a * hcarry_ref[...]
    hcarry_ref[...] = h[tc - 1:tc]

    y = h * jax.nn.gelu(lg_ref[0])
    y_ref[0] = _rms(y, gn_ref[...]).astype(BF16)


def _lru_branch(lx, lg, conv_w, conv_b, wa, ba, wx, bx, lam, gn):
    B, T, _ = lx.shape
    tc = min(TC_LRU, T)
    tile = pl.BlockSpec((1, tc, LRU_W), lambda b, t: (b, t, 0))
    vec = pl.BlockSpec((1, LRU_W), lambda b, t: (0, 0))
    mat = pl.BlockSpec((LRU_W, LRU_W), lambda b, t: (0, 0))
    return pl.pallas_call(
        functools.partial(_lru_kernel, tc=tc),
        grid=(B, T // tc),
        in_specs=[tile, tile, pl.BlockSpec((CONV_W, LRU_W), lambda b, t: (0, 0)), vec, mat, vec, mat, vec, vec, vec],
        out_specs=tile,
        out_shape=jax.ShapeDtypeStruct((B, T, LRU_W), BF16),
        scratch_shapes=[pltpu.VMEM((SUBLANES, LRU_W), F32), pltpu.VMEM((1, LRU_W), F32)],
        name="rg_lru",
    )(lx, lg, conv_w, conv_b, wa, ba, wx, bx, lam, gn)


def _s5_kernel(u_ref, bcat_ref, apr_ref, api_ref, p8r_ref, p8i_ref, bpr_ref, bpi_ref, ccat_ref, d_ref, gw_ref,
               gb_ref, gn_ref, y_ref, cr_ref, ci_ref, xsr_ref, xsi_ref, gr_ref, gi_ref, *, tc):
    @pl.when(pl.program_id(1) == 0)
    def _():
        cr_ref[...] = jnp.zeros_like(cr_ref)
        ci_ref[...] = jnp.zeros_like(ci_ref)

    def cmul_add(xr, xi, ar, ai, sr, si):
        return xr + (ar * sr - ai * si), xi + (ar * si + ai * sr)

    u = u_ref[0]
    bu = jnp.dot(u.astype(BF16), bcat_ref[...], preferred_element_type=F32)
    xr = bu[:, :S5_STATES]
    xi = bu[:, S5_STATES:]
    nb = tc // SUBLANES
    blocks = (nb, SUBLANES, S5_STATES)
    sub = lax.broadcasted_iota(I32, blocks, 1)
    xr = xr.reshape(blocks)
    xi = xi.reshape(blocks)
    for k in range(3):
        d = 1 << k
        ar = apr_ref[k:k + 1, :].reshape(1, 1, S5_STATES)
        ai = api_ref[k:k + 1, :].reshape(1, 1, S5_STATES)
        xr, xi = cmul_add(xr, xi, ar, ai, jnp.where(sub >= d, pltpu.roll(xr, d, 1), 0.0),
                          jnp.where(sub >= d, pltpu.roll(xi, d, 1), 0.0))
    xr = xr.reshape(tc, S5_STATES)
    xi = xi.reshape(tc, S5_STATES)
    lane_tiles = range(S5_STATES // LANES)

    def put(ref, v):
        for l in lane_tiles:
            ref[l] = v[:, l * LANES:(l + 1) * LANES]

    put(xsr_ref, xr)
    put(xsi_ref, xi)
    er =jnp.concatenate([xsr_ref[l, pl.ds(SUBLANES - 1, nb, stride=SUBLANES), :] for l in lane_tiles], axis=1)
    ei = jnp.concatenate([xsi_ref[l, pl.ds(SUBLANES - 1, nb, stride=SUBLANES), :] for l in lane_tiles], axis=1)
    blk = lax.broadcasted_iota(I32, (nb, S5_STATES), 0)
    k, d = 3, 1
    while d < nb:
        er, ei = cmul_add(er, ei, apr_ref[k:k + 1, :], api_ref[k:k + 1, :],
                          _shift_rows(er, d, blk, 0.0), _shift_rows(ei, d, blk, 0.0))
        k, d = k + 1, d * 2
    gr, gi = cmul_add(_shift_rows(er, 1, blk, 0.0), _shift_rows(ei, 1, blk, 0.0),
                      bpr_ref[...], bpi_ref[...], cr_ref[...], ci_ref[...])
    for j in range(SUBLANES):
        for l in lane_tiles:
            gr_ref[l, pl.ds(j, nb, stride=SUBLANES), :] = gr[:, l * LANES:(l + 1) * LANES]
            gi_ref[l, pl.ds(j, nb, stride=SUBLANES), :] = gi[:, l * LANES:(l + 1) * LANES]
    xr, xi = cmul_add(xr, xi, pltpu.repeat(p8r_ref[...], nb, 0), pltpu.repeat(p8i_ref[...], nb, 0),
                      jnp.concatenate([gr_ref[l] for l in lane_tiles], axis=1),
                      jnp.concatenate([gi_ref[l] for l in lane_tiles], axis=1))
    cr_ref[...] = xr[tc - 1:tc]
    ci_ref[...] = xi[tc - 1:tc]

    xcat = jnp.concatenate([xr, xi], axis=1).astype(BF16)
    y = jnp.dot(xcat, ccat_ref[...], preferred_element_type=F32) + d_ref[...] * u
    y = jax.nn.gelu(y)
    y = y * jax.nn.sigmoid(jnp.dot(y.astype(BF16), gw_ref[...], preferred_element_type=F32) + gb_ref[...])
    y_ref[0] = _rms(y, gn_ref[...]).astype(BF16)


def _s5_branch(u, bcat, apow_re, apow_im, p8_re, p8_im, bp_re, bp_im, ccat, d, glu_w, glu_b, gn, tc):
    B, T, _ = u.shape
    tile = pl.BlockSpec((1, tc, S5_W), lambda b, t: (b, t, 0))
    vec = pl.BlockSpec((1, S5_W), lambda b, t: (0, 0))

    def full(a):
        return pl.BlockSpec(a.shape, lambda b, t: (0, 0))

    return pl.pallas_call(
        functools.partial(_s5_kernel, tc=tc),
        grid=(B, T // tc),
        in_specs=[tile, full(bcat), full(apow_re), full(apow_im), full(p8_re), full(p8_im), full(bp_re),
                  full(bp_im), full(ccat), vec, full(glu_w), vec, vec],
        out_specs=tile,
        out_shape=jax.ShapeDtypeStruct((B, T, S5_W), BF16),
        scratch_shapes=[pltpu.VMEM((1, S5_STATES), F32), pltpu.VMEM((1, S5_STATES), F32)]
        + [pltpu.VMEM((S5_STATES // LANES, tc, LANES), F32)] * 4,
        compiler_params=_params(VMEM_LIMIT),
        name="s5",
    )(u, bcat, apow_re, apow_im, p8_re, p8_im, bp_re, bp_im, ccat, d, glu_w, glu_b, gn)


def _s5_tables(a_re, a_im, b_re, b_im, c_re, c_im, log_dt, tc):
    dt = jnp.exp(log_dt)[:, None]
    mag = jnp.exp(a_re * dt)
    abar_re = mag * jnp.cos(a_im * dt)
    abar_im = mag * jnp.sin(a_im * dt)
    den = a_re * a_re + a_im * a_im
    num_re = abar_re - 1.0
    k_re = (num_re * a_re + abar_im * a_im) / den
    k_im = (abar_im * a_re - num_re * a_im) / den
    bbar_re = k_re[..., None] * b_re - k_im[..., None] * b_im
    bbar_im = k_re[..., None] * b_im + k_im[..., None] * b_re
    eye = jnp.eye(S5_GROUPS, dtype=F32)
    bd_re = jnp.einsum('gpc,gh->gchp', bbar_re, eye).reshape(S5_W, S5_STATES)
    bd_im = jnp.einsum('gpc,gh->gchp', bbar_im, eye).reshape(S5_W, S5_STATES)
    bcat = jnp.concatenate([bd_re, bd_im], axis=1).astype(BF16)
    cd_re = jnp.einsum('gcp,gh->gphc', c_re, eye).reshape(S5_STATES, S5_W)
    cd_im = jnp.einsum('gcp,gh->gphc', c_im, eye).reshape(S5_STATES, S5_W)
    ccat = jnp.concatenate([cd_re, -cd_im], axis=0).astype(BF16)
    ar = abar_re.reshape(1, S5_STATES)
    ai = abar_im.reshape(1, S5_STATES)
    pows_r, pows_i = [ar], [ai]
    tab_r, tab_i = ar, ai
    n = 1
    while n < tc:
        sr, si = pows_r[-1], pows_i[-1]
        tab_r, tab_i = (jnp.concatenate([tab_r, tab_r * sr - tab_i * si], axis=0),
                        jnp.concatenate([tab_i, tab_r * si + tab_i * sr], axis=0))
        pows_r.append(sr * sr - si * si)
        pows_i.append(2.0 * sr * si)
        n *= 2
    levels = len(pows_r) - 1
    pad = (-levels) % SUBLANES
    apow_re = jnp.concatenate(pows_r[:levels] + [jnp.zeros((pad, S5_STATES), F32)], axis=0)
    apow_im = jnp.concatenate(pows_i[:levels] + [jnp.zeros((pad, S5_STATES), F32)], axis=0)
    nb = tc // SUBLANES
    bp_re = jnp.concatenate([jnp.ones((1, S5_STATES), F32), tab_r[SUBLANES - 1::SUBLANES][:nb - 1]], axis=0)
    bp_im = jnp.concatenate([jnp.zeros((1, S5_STATES), F32), tab_i[SUBLANES - 1::SUBLANES][:nb - 1]], axis=0)
    return bcat, apow_re, apow_im, tab_r[:SUBLANES], tab_i[:SUBLANES], bp_re, bp_im, ccat


def _attn_kernel(qt_ref, k_ref, vt_ref, o_ref, p_ref, *, tq):
    qi = pl.program_id(2)
    key = lax.broadcasted_iota(I32, (tq, tq), 0)
    qry = lax.broadcasted_iota(I32, (tq, tq), 1)
    heads = range(ATTN_HEADS_PER_STEP)
    qts =[qt_ref[0, hh] for hh in heads]

    def scores(hh, kj):
        k = k_ref[0, hh, pl.ds(pl.multiple_of(kj * tq, tq), tq), :]
        return jnp.dot(k, qts[hh], preferred_element_type=F32)

    def weighted_values(hh, kj, p):
        vt = vt_ref[0, hh, :, pl.ds(pl.multiple_of(kj * tq, tq), tq)]
        return jnp.dot(vt, p, preferred_element_type=F32)

    def softmax_step(s, m):
        m_new = jnp.maximum(m, jnp.max(s, axis=0, keepdims=True))
        return m_new, jnp.exp(m - m_new), jnp.exp(s - m_new).astype(BF16)

    p_ref[...] = jnp.zeros_like(p_ref)

    def body(kj, carry):
        ss = [scores(hh, kj) for hh in heads]
        pvs = [weighted_values(hh, jnp.maximum(kj - 1, 0), p_ref[hh]) for hh in heads]
        out = []
        for hh in heads:
            m, alpha, acc = carry[hh]
            acc = alpha * acc + pvs[hh]
            m, alpha, p = softmax_step(ss[hh], m)
            p_ref[hh] = p
            out.append((m, alpha, acc))
        return tuple(out)

    init = tuple((jnp.full((1, tq), NEG_BIG, F32), jnp.ones((1, tq), F32), jnp.zeros((LANES, tq), F32))
                 for _ in heads)
    carry = lax.fori_loop(0, qi, body, init)
    outs = []
    for hh in heads:
        m, alpha, acc = carry[hh]
        acc = alpha * acc + weighted_values(hh, jnp.maximum(qi - 1, 0), p_ref[hh])
        s = jnp.where(key <= qry, scores(hh, qi), NEG_BIG)
        m, alpha, p = softmax_step(s, m)
        acc = alpha * acc + weighted_values(hh, qi, p)
        out_t = acc / acc[HEAD_DIM:HEAD_DIM + 1, :]
        outs.append(out_t.T[:, :HEAD_DIM])
    o_ref[0] = jnp.concatenate(outs, axis=1)


def _attention(qa, ka, va):
    B, H, T, _ = qa.shape
    tq = min(TQ, T)
    hs = ATTN_HEADS_PER_STEP
    qt = jnp.swapaxes(qa, 2, 3)
    vt = jnp.swapaxes(va, 2, 3)
    return pl.pallas_call(
        functools.partial(_attn_kernel, tq=tq),
        grid=(B, H // hs, T // tq),
        in_specs=[pl.BlockSpec((1, hs, LANES, tq), lambda b, hp, qi: (b, hp, 0, qi)),
                  pl.BlockSpec((1, hs, T, LANES), lambda b, hp, qi: (b, hp, 0, 0)),
                  pl.BlockSpec((1, hs, LANES, T), lambda b, hp, qi: (b, hp, 0, 0))],
        out_specs=pl.BlockSpec((1, tq, hs * HEAD_DIM), lambda b, hp, qi: (b, qi, hp)),
        out_shape=jax.ShapeDtypeStruct((B, T, ATTN_W), F32),
        scratch_shapes=[pltpu.VMEM((hs, tq, tq), BF16)],
        compiler_params=_params(VMEM_LIMIT),
        name="fox_attention",
    )(qt, ka, vt)


def _out_proj_kernel(yl_ref, ys_ref, ya_ref, x_ref, mod_ref, gna_ref, wout_ref, pmg_ref, pfg_ref,
                     rwh_ref, rwl_ref, rb_ref,
                     x1_ref, h2_ref, idx_ref, gate_ref, rank_ref, cnt_ref, carry_ref, *, tm):
    @pl.when((pl.program_id(0) == 0) & (pl.program_id(1) == 0))
    def _():
        carry_ref[...] = jnp.zeros_like(carry_ref)

    mod = mod_ref[0]
    ya = _rms(ya_ref[0], gna_ref[...])
    ycat = jnp.concatenate([yl_ref[0], ys_ref[0], ya.astype(BF16)], axis=1)
    y = jnp.dot(ycat, wout_ref[...], preferred_element_type=F32)
    x1 = x_ref[0] + mod[2:3] * _rms(y, pmg_ref[...])
    x1_ref[0] = x1
    h2 = _rms(x1, pfg_ref[...]) * (1.0 + mod[4:5]) + mod[3:4]
    for c in range(ROW_CHUNKS):
        h2_ref[pl.ds(c, tm, stride=ROW_CHUNKS), :] = h2[:, c * LANES:(c + 1) * LANES]

    hh = h2.astype(BF16)
    hl = (h2 - hh.astype(F32)).astype(BF16)
    rwh = rwh_ref[...]
    logits = (jnp.dot(hh, rwh, preferred_element_type=F32) + jnp.dot(hl, rwh, preferred_element_type=F32)
              + jnp.dot(hh, rwl_ref[...], preferred_element_type=F32) + rb_ref[...])
    lane = lax.broadcasted_iota(I32, (tm, LANES), 1)
    lane_f = lane.astype(F32)
    work = jnp.where(lane < N_EXPERTS, logits, NEG_BIG)
    vals, idxs = [], []
    for _ in range(TOP_K):
        mx = jnp.max(work, axis=1, keepdims=True)
        ik = jnp.min(jnp.where(work == mx, lane_f, float(LANES)), axis=1, keepdims=True)
        vals.append(mx)
        idxs.append(ik)
        work = jnp.where(lane_f == ik, 2.0 * NEG_BIG, work)
    es = [jnp.exp(v - vals[0]) for v in vals]
    den = es[0] + es[1] + es[2] + es[3]

    onehot = jnp.zeros((tm, LANES), F32)
    for ik in idxs:
        onehot = onehot + jnp.where(lane_f == ik, 1.0, 0.0)
    r_i = lax.broadcasted_iota(I32, (tm, tm), 0)
    c_i = lax.broadcasted_iota(I32, (tm, tm), 1)
    below = jnp.where(c_i < r_i, 1.0, 0.0).astype(BF16)
    prior = jnp.dot(below, onehot.astype(BF16), preferred_element_type=F32) + carry_ref[...]
    carry = carry_ref[...] + jnp.sum(onehot, axis=0, keepdims=True)
    carry_ref[...] = carry
    cnt_ref[...] = carry.astype(I32)

    idx_full = jnp.zeros((tm, LANES), F32)
    gate_full = jnp.zeros((tm, LANES), F32)
    rank_full = jnp.zeros((tm, LANES), F32)
    for k in range(TOP_K):
        rk = jnp.sum(jnp.where(lane_f == idxs[k], prior, 0.0), axis=1, keepdims=True)
        idx_full = jnp.where(lane == k, idxs[k], idx_full)
        gate_full = jnp.where(lane == k, es[k] / den, gate_full)
        rank_full = jnp.where(lane == k, rk, rank_full)
    idx_ref[...] = idx_full[:, :TOP_K].astype(I32)
    gate_ref[...] = gate_full[:, :TOP_K]
    rank_ref[...] = rank_full[:, :TOP_K].astype(I32)


def _out_proj(yl, ys, ya, x, mod, gna, wout, pmg, pfg, rwh, rwl, rb):
    B, T, _ = x.shape
    tm = min(TM_OUT, T)
    nt = T // tm
    N = B * T

    def vec(n):
        return pl.BlockSpec((1, n), lambda b, t: (0, 0))

    tok = lambda b, t: (b * nt + t, 0)
    return pl.pallas_call(
        functools.partial(_out_proj_kernel, tm=tm),
        grid=(B, nt),
        in_specs=[pl.BlockSpec((1, tm, LRU_W), lambda b, t: (b, t, 0)),
                  pl.BlockSpec((1, tm, S5_W), lambda b, t: (b, t, 0)),
                  pl.BlockSpec((1, tm, ATTN_W), lambda b, t: (b, t, 0)),
                  pl.BlockSpec((1, tm, D), lambda b, t: (b, t, 0)),
                  pl.BlockSpec((1, N_MOD, D), lambda b, t: (b, 0, 0)),
                  vec(ATTN_W),
                  pl.BlockSpec((D, D), lambda b, t: (0, 0)),
                  vec(D), vec(D),
                  pl.BlockSpec((D, LANES), lambda b, t: (0, 0)),
                  pl.BlockSpec((D, LANES), lambda b, t: (0, 0)),
                  vec(LANES)],
        out_specs=[pl.BlockSpec((1, tm, D), lambda b, t: (b, t, 0)),
                   pl.BlockSpec((tm * ROW_CHUNKS, LANES), tok),
                   pl.BlockSpec((tm, TOP_K), tok),
                   pl.BlockSpec((tm, TOP_K), tok),
                   pl.BlockSpec((tm, TOP_K), tok),
                   pl.BlockSpec((1, LANES), lambda b, t: (0, 0))],
        out_shape=[jax.ShapeDtypeStruct((B, T, D), F32),
                   jax.ShapeDtypeStruct((N * ROW_CHUNKS, LANES), F32),
                   jax.ShapeDtypeStruct((N, TOP_K), I32),
                   jax.ShapeDtypeStruct((N, TOP_K), F32),
                   jax.ShapeDtypeStruct((N, TOP_K), I32),
                   jax.ShapeDtypeStruct((1, LANES), I32)],
        scratch_shapes=[pltpu.VMEM((1, LANES), F32)],
        compiler_params=_params(VMEM_LIMIT),
        name="out_proj_router",
    )(yl, ys, ya, x, mod, gna, wout, pmg, pfg, rwh, rwl, rb)


def _row_copy(src, src_row, dst, dst_row, sem):
    return pltpu.make_async_copy(src.at[pl.ds(pl.multiple_of(src_row * ROW_CHUNKS, ROW_CHUNKS), ROW_CHUNKS), :],
                                 dst.at[pl.ds(pl.multiple_of(dst_row * ROW_CHUNKS, ROW_CHUNKS), ROW_CHUNKS), :],
                                 sem)


def _dispatch_kernel(start_ref, cnt_ref, pad_ref, idx_ref, rank_ref, h_ref, dest_ref, xb_ref, sem, *, tn):
    i = pl.program_id(0)

    def issue(j, _):
        for k in range(TOP_K):
            a = j * TOP_K + k
            dst = start_ref[idx_ref[a]] + rank_ref[a]
            dest_ref[a] = dst
            _row_copy(h_ref, j, xb_ref, dst, sem).start()
        return 0

    lax.fori_loop(0, tn, issue, 0)

    def drain(j, _):
        for k in range(TOP_K):
            _row_copy(h_ref, 0, xb_ref, 0, sem).wait()
        return 0

    lax.fori_loop(0, tn, drain, 0)

    @pl.when(i == pl.num_programs(0) - 1)
    def _():
        def per_expert(e, _):
            lo = start_ref[e] + cnt_ref[e]
            hi = start_ref[e] + pad_ref[e]

            def fill(r, _):
                _row_copy(h_ref, 0, xb_ref, r, sem).start()
                return 0

            lax.fori_loop(lo, hi, fill, 0)

            def fill_wait(r, _):
                _row_copy(h_ref, 0, xb_ref, r, sem).wait()
                return 0

            lax.fori_loop(lo, hi, fill_wait, 0)
            return 0

        lax.fori_loop(0, N_EXPERTS, per_expert, 0)


def _dispatch(starts, counts, padded, idx_flat, rank_flat, h_rows, n_rows):
    NK = idx_flat.shape[0]
    N = NK // TOP_K
    tn = min(TN_ROWS, N)
    smem_blk = pl.BlockSpec((tn * TOP_K,), lambda i, *_: (i,), memory_space=pltpu.SMEM)
    grid_spec = pltpu.PrefetchScalarGridSpec(
        num_scalar_prefetch=3,
        grid=(N // tn,),
        in_specs=[smem_blk, smem_blk, pl.BlockSpec((tn * ROW_CHUNKS, LANES), lambda i, *_: (i, 0))],
        out_specs=[smem_blk, pl.BlockSpec(memory_space=pl.ANY)],
        scratch_shapes=[pltpu.SemaphoreType.DMA(())],
    )
    return pl.pallas_call(
        functools.partial(_dispatch_kernel, tn=tn),
        grid_spec=grid_spec,
        out_shape=[jax.ShapeDtypeStruct((NK,), I32),
                   jax.ShapeDtypeStruct((n_rows * ROW_CHUNKS, LANES), F32)],
        name="moe_dispatch",
    )(starts, counts, padded, idx_flat, rank_flat, h_rows)


def _expert_kernel(be_ref, nused_ref, x_ref, wgu_ref, bgu_ref, wdn_ref, bdn_ref, y_ref, wgu_s, wdn_s, *, rows):
    b = pl.program_id(0)
    prev = be_ref[jnp.maximum(b - 1, 0)]

    @pl.when((b == 0) | (be_ref[b] != prev))
    def _():
        wgu_s[...] = wgu_ref[0, 0].astype(BF16)
        wdn_s[...] = wdn_ref[0, 0].astype(BF16)

    @pl.when(b < nused_ref[0])
    def _():
        x = jnp.concatenate([x_ref[pl.ds(c, rows, stride=ROW_CHUNKS), :] for c in range(ROW_CHUNKS)], axis=1)
        gu = jnp.dot(x.astype(BF16), wgu_s[...], preferred_element_type=F32) + bgu_ref[0, 0]
        g = jnp.minimum(gu[:, :D_FF], SWIGLU_LIMIT)
        up = jnp.clip(gu[:, D_FF:], -SWIGLU_LIMIT, SWIGLU_LIMIT)
        act = (up + 1.0) * (g * jax.nn.sigmoid(SWIGLU_ALPHA * g))
        y = jnp.dot(act.astype(BF16), wdn_s[...], preferred_element_type=F32) + bdn_ref[0, 0]
        for c in range(ROW_CHUNKS):
            y_ref[pl.ds(c, rows, stride=ROW_CHUNKS), :] = y[:, c * LANES:(c + 1) * LANES]


def _experts(layer, block_e, n_used, xb, w_gu, b_gu, w_dn, b_dn, rows):
    n_blocks = block_e.shape[0]
    L, E = w_gu.shape[:2]
    blk = lambda b, be, nu: (jnp.minimum(b, nu[0] - 1), 0)
    wsel = lambda b, be, nu: (layer, be[b], 0, 0)
    grid_spec = pltpu.PrefetchScalarGridSpec(
        num_scalar_prefetch=2,
        grid=(n_blocks,),
        in_specs=[pl.BlockSpec((rows * ROW_CHUNKS, LANES), blk),
                  pl.BlockSpec((1, 1, D, 2 * D_FF), wsel),
                  pl.BlockSpec((1, 1, 1, 2 * D_FF), wsel),
                  pl.BlockSpec((1, 1, D_FF, D), wsel),
                  pl.BlockSpec((1, 1, 1, D), wsel)],
        out_specs=pl.BlockSpec((rows * ROW_CHUNKS, LANES), blk),
        scratch_shapes=[pltpu.VMEM((D, 2 * D_FF), BF16), pltpu.VMEM((D_FF, D), BF16)],
    )
    return pl.pallas_call(
        functools.partial(_expert_kernel, rows=rows),
        grid_spec=grid_spec,
        out_shape=jax.ShapeDtypeStruct(xb.shape, F32),
        compiler_params=_params(VMEM_LIMIT),
        name="moe_experts",
    )(block_e, n_used, xb, w_gu, b_gu.reshape(L, E, 1, 2 * D_FF), w_dn, b_dn.reshape(L, E, 1, D))


def _combine_kernel(dest_ref, yb_ref, gate_ref, x1_ref, mod_ref, g_ref, o_ref, buf_ref, sem, *, tn):
    def issue(j, _):
        for k in range(TOP_K):
            pltpu.make_async_copy(
                yb_ref.at[pl.ds(pl.multiple_of(dest_ref[j * TOP_K + k] * ROW_CHUNKS, ROW_CHUNKS), ROW_CHUNKS), :],
                buf_ref.at[k, pl.ds(pl.multiple_of(j * ROW_CHUNKS, ROW_CHUNKS), ROW_CHUNKS), :], sem).start()
        return 0

    lax.fori_loop(0, tn, issue, 0)

    def drain(j, _):
        for k in range(TOP_K):
            pltpu.make_async_copy(yb_ref.at[pl.ds(0, ROW_CHUNKS), :],
                                  buf_ref.at[k, pl.ds(0, ROW_CHUNKS), :], sem).wait()
        return 0

    lax.fori_loop(0, tn, drain, 0)

    gates = gate_ref[...]
    y = jnp.zeros((tn, D), F32)
    for k in range(TOP_K):
        yk = jnp.concatenate([buf_ref[k, pl.ds(c, tn, stride=ROW_CHUNKS), :] for c in range(ROW_CHUNKS)], axis=1)
        y = y + gates[:, k:k + 1] * yk
    mod = mod_ref[0]
    o_ref[0] = x1_ref[0] + mod[5:6] * _rms(y, g_ref[...])


def _combine(dest_flat, yb, gates, x1, mod, g):
    B, T, _ = x1.shape
    tn = min(TN_COMB, T)
    nt = T // tn
    grid_spec = pltpu.PrefetchScalarGridSpec(
        num_scalar_prefetch=0,
        grid=(B, nt),
        in_specs=[pl.BlockSpec((tn * TOP_K,), lambda b, t: (b * nt + t,), memory_space=pltpu.SMEM),
                  pl.BlockSpec(memory_space=pl.ANY),
                  pl.BlockSpec((tn, TOP_K), lambda b, t: (b * nt + t, 0)),
                  pl.BlockSpec((1, tn, D), lambda b, t: (b, t, 0)),
                  pl.BlockSpec((1, N_MOD, D), lambda b, t: (b, 0, 0)),
                  pl.BlockSpec((1, D), lambda b, t: (0, 0))],
        out_specs=pl.BlockSpec((1, tn, D), lambda b, t: (b, t, 0)),
        scratch_shapes=[pltpu.VMEM((TOP_K, tn * ROW_CHUNKS, LANES), F32), pltpu.SemaphoreType.DMA(())],
    )
    return pl.pallas_call(
        functools.partial(_combine_kernel, tn=tn),
        grid_spec=grid_spec,
        out_shape=jax.ShapeDtypeStruct((B, T, D), F32),
        name="moe_combine",
    )(dest_flat, yb, gates, x1, mod, g)


def _fused_expert_kernel(be_ref, nused_ref, src_ref, dst_ref, h_ref, wgu_ref, bgu_ref, wdn_ref, bdn_ref, y_ref,
                         xbuf, ybuf, wgu_s, wdn_s, gsem, ssem, *, rows, n_blocks, n_assign):
    b = pl.program_id(0)
    nu = nused_ref[0]
    slot = lax.rem(b, 2)
    other = 1 - slot
    xslot = lax.rem(b, 3)
    xnext = lax.rem(b + 2, 3)
    src_off = lax.rem(jnp.minimum(b + 2, n_blocks - 1) * rows, MAP_BLOCK)
    dst_off = lax.rem(jnp.maximum(b - 1, 0) * rows, MAP_BLOCK)

    def gather(i, tok, buf_slot):
        return pltpu.make_async_copy(
            h_ref.at[pl.ds(pl.multiple_of(tok * ROW_CHUNKS, ROW_CHUNKS), ROW_CHUNKS), :],
            xbuf.at[buf_slot, pl.ds(i * ROW_CHUNKS, ROW_CHUNKS), :], gsem.at[buf_slot])

    def scatter(i, dst, buf_slot):
        return pltpu.make_async_copy(
            ybuf.at[buf_slot, pl.ds(i * ROW_CHUNKS, ROW_CHUNKS), :],
            y_ref.at[pl.ds(pl.multiple_of(dst * ROW_CHUNKS, ROW_CHUNKS), ROW_CHUNKS), :], ssem.at[buf_slot])

    @pl.when(b == 0)
    def _():
        ybuf[...] = jnp.zeros(ybuf.shape, F32)
        for i in range(rows):
            gather(i, src_ref[i], 0).start()
            gather(i, src_ref[rows + i], 1).start()
            scatter(i, n_assign + 2 * rows + i, 0).start()

    @pl.when(b <= nu)
    def _():
        for i in range(rows):
            gather(i, 0, xslot).wait()

    prev = be_ref[jnp.maximum(b - 1, 0)]

    @pl.when((b == 0) | (be_ref[b] != prev))
    def _():
        wgu_s[...] = wgu_ref[0, 0].astype(BF16)
        wdn_s[...] = wdn_ref[0, 0].astype(BF16)

    @pl.when(b < nu)
    def _():
        x = jnp.concatenate([xbuf[xslot, pl.ds(c, rows, stride=ROW_CHUNKS), :] for c in range(ROW_CHUNKS)],
                            axis=1).astype(BF16)
        for i in range(rows):
            gather(i, src_ref[src_off + i], xnext).start(priority=i % 2)
        for i in range(rows):
            dst = jnp.where(b == 0, n_assign + 3 * rows + i, dst_ref[dst_off + i])
            scatter(i, dst, other).start(priority=i % 2)
        gu = jnp.dot(x, wgu_s[...], preferred_element_type=F32) + bgu_ref[0, 0]
        g = jnp.minimum(gu[:, :D_FF], SWIGLU_LIMIT)
        up = jnp.clip(gu[:, D_FF:], -SWIGLU_LIMIT, SWIGLU_LIMIT)
        act = (up + 1.0) * (g * jax.nn.sigmoid(SWIGLU_ALPHA * g))
        y = jnp.dot(act.astype(BF16), wdn_s[...], preferred_element_type=F32) + bdn_ref[0, 0]
        for i in range(rows):
            scatter(i, 0, slot).wait()
        for c in range(ROW_CHUNKS):
            ybuf[slot, pl.ds(c, rows, stride=ROW_CHUNKS), :] = y[:, c * LANES:(c + 1) * LANES]

    @pl.when(b == nu)
    def _():
        for i in range(rows):
            gather(i, 0, lax.rem(b + 1, 3)).wait()
        for i in range(rows):
            scatter(i, 0, slot).wait()
        for i in range(rows):
            scatter(i, dst_ref[dst_off + i], other).start()
        for i in range(rows):
            scatter(i, 0, other).wait()


FF_CHUNK = 256


def _expert_block_kernel(be_ref, nused_ref, src_ref, dst_ref, h_ref, wgu_ref, bgu_ref, wdn_ref, bdn_ref, y_ref,
                         xbuf, ybuf, wgu_s, wdn_s, gsem, ssem, *, rows, n_blocks, n_assign):
    b = pl.program_id(0)
    nu = nused_ref[0]
    slot = lax.rem(b, 2)
    other = 1 - slot
    src_off = lax.rem(jnp.minimum(b + 1, n_blocks - 1) * rows, MAP_BLOCK)
    dst_off = lax.rem(jnp.maximum(b - 1, 0) * rows, MAP_BLOCK)

    def gather(i, tok):
        return pltpu.make_async_copy(
            h_ref.at[pl.ds(pl.multiple_of(tok * ROW_CHUNKS, ROW_CHUNKS), ROW_CHUNKS), :],
            xbuf.at[pl.ds(i * ROW_CHUNKS, ROW_CHUNKS), :], gsem)

    def scatter(i, dst, buf_slot):
        return pltpu.make_async_copy(
            ybuf.at[buf_slot, pl.ds(i * ROW_CHUNKS, ROW_CHUNKS), :],
            y_ref.at[pl.ds(pl.multiple_of(dst * ROW_CHUNKS, ROW_CHUNKS), ROW_CHUNKS), :], ssem.at[buf_slot])

    @pl.when(b == 0)
    def _():
        ybuf[...] = jnp.zeros(ybuf.shape, F32)
        for i in range(rows):
            gather(i, src_ref[i]).start()
            scatter(i, n_assign + 2 * rows + i, 0).start()

    @pl.when(b <= nu)
    def _():
        for i in range(rows):
            gather(i, 0).wait()

    prev = be_ref[jnp.maximum(b - 1, 0)]

    @pl.when((b == 0) | (be_ref[b] != prev))
    def _():
        wgu_s[...] = wgu_ref[0, 0].astype(BF16)
        wdn_s[...] = wdn_ref[0, 0].astype(BF16)

    @pl.when(b < nu)
    def _():
        x = jnp.concatenate([xbuf[pl.ds(c, rows, stride=ROW_CHUNKS), :] for c in range(ROW_CHUNKS)],
                            axis=1).astype(BF16)
        n_chunks = D_FF // FF_CHUNK
        per_burst = rows // n_chunks

        def gather_burst(j):
            for i in range(j * per_burst, (j + 1) * per_burst):
                gather(i, src_ref[src_off + i]).start(priority=i % 2)

        def scatter_burst(j):
            for i in range(j * per_burst, (j + 1) * per_burst):
                dst = jnp.where(b == 0, n_assign + 3 * rows + i, dst_ref[dst_off + i])
                scatter(i, dst, other).start(priority=i % 2)

        bursts = [functools.partial(gather_burst, j) for j in range(n_chunks)]
        bursts += [functools.partial(scatter_burst, j) for j in range(n_chunks)]
        bgu = bgu_ref[0, 0]
        y = jnp.zeros((rows, D), F32) + bdn_ref[0, 0]
        for j in range(n_chunks):
            cols = slice(j * FF_CHUNK, (j + 1) * FF_CHUNK)
            ucols = slice(D_FF + j * FF_CHUNK, D_FF + (j + 1) * FF_CHUNK)
            g = jnp.dot(x, wgu_s[:, cols], preferred_element_type=F32) + bgu[:, cols]
            bursts[2 * j]()
            up = jnp.dot(x, wgu_s[:, ucols], preferred_element_type=F32) + bgu[:, ucols]
            g = jnp.minimum(g, SWIGLU_LIMIT)
            up = jnp.clip(up, -SWIGLU_LIMIT, SWIGLU_LIMIT)
            act = (up + 1.0) * (g * jax.nn.sigmoid(SWIGLU_ALPHA * g))
            y = y + jnp.dot(act.astype(BF16), wdn_s[cols, :], preferred_element_type=F32)
            bursts[2 * j + 1]()
        for i in range(rows):
            scatter(i, 0, slot).wait()
        for c in range(ROW_CHUNKS):
            ybuf[slot, pl.ds(c, rows, stride=ROW_CHUNKS), :] = y[:, c * LANES:(c + 1) * LANES]

    @pl.when(b == nu)
    def _():
        for i in range(rows):
            scatter(i, 0, slot).wait()
        for i in range(rows):
            scatter(i, dst_ref[dst_off + i], other).start()
        for i in range(rows):
            scatter(i, 0, other).wait()


def _fused_experts(layer, block_e, n_used, src_tok, dst_row, h_rows, w_gu, b_gu, w_dn, b_dn, rows, n_assign):
    n_blocks = block_e.shape[0]
    L, E = w_gu.shape[:2]
    wsel = lambda b, be, nu: (layer, be[b], 0, 0)
    grid_spec = pltpu.PrefetchScalarGridSpec(
        num_scalar_prefetch=2,
        grid=(n_blocks,),
        in_specs=[pl.BlockSpec((MAP_BLOCK,), lambda b, be, nu: (jnp.minimum(b + 1, n_blocks - 1) * rows // MAP_BLOCK,),
                               memory_space=pltpu.SMEM),
                  pl.BlockSpec((MAP_BLOCK,), lambda b, be, nu: (jnp.maximum(b - 1, 0) * rows // MAP_BLOCK,),
                               memory_space=pltpu.SMEM),
                  pl.BlockSpec(memory_space=pl.ANY),
                  pl.BlockSpec((1, 1, D, 2 * D_FF), wsel),
                  pl.BlockSpec((1, 1, 1, 2 * D_FF), wsel),
                  pl.BlockSpec((1, 1, D_FF, D), wsel),
                  pl.BlockSpec((1, 1, 1, D), wsel)],
        out_specs=pl.BlockSpec(memory_space=pl.ANY),
        scratch_shapes=[pltpu.VMEM((rows * ROW_CHUNKS, LANES), F32),
                        pltpu.VMEM((2, rows * ROW_CHUNKS, LANES), F32),
                        pltpu.VMEM((D, 2 * D_FF), BF16), pltpu.VMEM((D_FF, D), BF16),
                        pltpu.SemaphoreType.DMA(()), pltpu.SemaphoreType.DMA((2,))],
    )
    return pl.pallas_call(
        functools.partial(_expert_block_kernel, rows=rows, n_blocks=n_blocks, n_assign=n_assign),
        grid_spec=grid_spec,
        out_shape=jax.ShapeDtypeStruct(((n_assign + 4 * rows) * ROW_CHUNKS, LANES), F32),
        compiler_params=_params(VMEM_LIMIT),
        name="moe_experts",
    )(block_e, n_used, src_tok, dst_row, h_rows, w_gu, b_gu.reshape(L, E, 1, 2 * D_FF), w_dn,
      b_dn.reshape(L, E, 1, D))


def _dense_combine_kernel(y0_ref, y1_ref, y2_ref, y3_ref, gate_ref, x1_ref, mod_ref, g_ref, o_ref, *, tn):
    gates = gate_ref[...]
    y = jnp.zeros((tn, D), F32)
    for k, yk_ref in enumerate((y0_ref, y1_ref, y2_ref, y3_ref)):
        yk = jnp.concatenate([yk_ref[pl.ds(c, tn, stride=ROW_CHUNKS), :] for c in range(ROW_CHUNKS)], axis=1)
        y = y + gates[:, k:k + 1] * yk
    mod = mod_ref[0]
    o_ref[0] = x1_ref[0] + mod[5:6] * _rms(y, g_ref[...])


def _dense_combine(y_rows, gates, x1, mod, g):
    B, T, _ = x1.shape
    tn = min(TN_COMB, T)
    nt = T // tn

    def slot_spec(k):
        return pl.BlockSpec((tn * ROW_CHUNKS, LANES), lambda b, t: (k * B * nt + b * nt + t, 0))

    return pl.pallas_call(
        functools.partial(_dense_combine_kernel, tn=tn),
        grid=(B, nt),
        in_specs=[slot_spec(0), slot_spec(1), slot_spec(2), slot_spec(3),
                  pl.BlockSpec((tn, TOP_K), lambda b, t: (b * nt + t, 0)),
                  pl.BlockSpec((1, tn, D), lambda b, t: (b, t, 0)),
                  pl.BlockSpec((1, N_MOD, D), lambda b, t: (b, 0, 0)),
                  pl.BlockSpec((1, D), lambda b, t: (0, 0))],
        out_specs=pl.BlockSpec((1, tn, D), lambda b, t: (b, t, 0)),
        out_shape=jax.ShapeDtypeStruct((B, T, D), F32),
        name="moe_combine",
    )(y_rows, y_rows, y_rows, y_rows, gates, x1, mod, g)


def _head_slabs(w):
    w = w.reshape(D, N_HEADS, HEAD_DIM)
    return jnp.pad(w, ((0, 0), (0, 0), (0, LANES - HEAD_DIM))).reshape(D, N_HEADS * LANES)


def _block_diag(w):
    nb = w.shape[0]
    return jnp.einsum('hij,hg->higj', w, jnp.eye(nb, dtype=w.dtype)).reshape(LRU_W, LRU_W)


def kernel(x, c, ada_w, ada_b, pre_mix_g, w_in, conv_w, conv_b, lru_wa, lru_ba, lru_wx, lru_bx, lru_lambda, s5_a_re, s5_a_im, s5_b_re, s5_b_im, s5_c_re, s5_c_im, s5_d, s5_log_dt, s5_glu_w, s5_glu_b, fox_fb, gn_lru, gn_s5, gn_attn, w_out, post_mix_g, pre_ffn_g, router_w, router_b, w_gu, b_gu, w_dn, b_dn, post_ffn_g):
    B, T, _ = x.shape
    N = B * T
    L = ada_w.shape[0]
    tc_s5 = min(TC_S5, T)
    n_blocks = (N * TOP_K) // EXP_ROWS + N_EXPERTS
    n_rows = n_blocks * EXP_ROWS
    mod_all = _modulation(c, ada_w, ada_b).reshape(L, B, N_MOD, D)

    for l in range(L):
        mod = mod_all[l]
        wl = w_in[l]
        w_aug = jnp.concatenate(
            [wl[:, :O_Q], _head_slabs(wl[:, O_Q:O_K]), _head_slabs(wl[:, O_K:O_V]), _head_slabs(wl[:, O_V:O_F]),
             jnp.pad(wl[:, O_F:], ((0, 0), (0, LANES - N_HEADS)))], axis=1).astype(BF16)
        fb = jnp.pad(fox_fb[l], (0, LANES - N_HEADS)).reshape(1, LANES)
        lx, lg, su, qa, ka, va = _in_proj(x, mod, pre_mix_g[l].reshape(1, D), w_aug, fb)

        yl = _lru_branch(lx, lg, conv_w[l], conv_b[l].reshape(1, LRU_W),
                         _block_diag(lru_wa[l]).astype(BF16), lru_ba[l].reshape(1, LRU_W),
                         _block_diag(lru_wx[l]).astype(BF16), lru_bx[l].reshape(1, LRU_W),
                         lru_lambda[l].reshape(1, LRU_W), gn_lru[l].reshape(1, LRU_W))

        tabs = _s5_tables(s5_a_re[l], s5_a_im[l], s5_b_re[l], s5_b_im[l], s5_c_re[l], s5_c_im[l], s5_log_dt[l], tc_s5)
        ys = _s5_branch(su, *tabs, s5_d[l].reshape(1, S5_W), s5_glu_w[l].astype(BF16),
                        s5_glu_b[l].reshape(1, S5_W), gn_s5[l].reshape(1, S5_W), tc_s5)

        ya = _attention(qa, ka, va)

        rw = jnp.pad(router_w[l], ((0, 0), (0, LANES - N_EXPERTS)))
        rwh = rw.astype(BF16)
        rwl = (rw - rwh.astype(F32)).astype(BF16)
        rb = jnp.pad(router_b[l], (0, LANES - N_EXPERTS)).reshape(1, LANES)
        x1, h_rows, idx, gates, rank, cnt = _out_proj(
            yl, ys, ya, x, mod, gn_attn[l].reshape(1, ATTN_W), w_out[l].astype(BF16),
            post_mix_g[l].reshape(1, D), pre_ffn_g[l].reshape(1, D), rwh, rwl, rb)

        counts = cnt[0, :N_EXPERTS]
        blocks_e = (counts + EXP_ROWS - 1) // EXP_ROWS
        padded = blocks_e * EXP_ROWS
        block_ends = jnp.cumsum(blocks_e)
        starts = (block_ends - blocks_e) * EXP_ROWS
        n_used = block_ends[-1:]
        bid = jnp.minimum(jnp.arange(n_blocks, dtype=I32), n_used[0] - 1)
        block_e = jnp.minimum(jnp.sum((bid[:, None] >= block_ends[None, :]).astype(I32), axis=1), N_EXPERTS - 1)

        n_assign = N * TOP_K
        dest = (jnp.take(starts.astype(I32), idx) + rank).reshape(n_assign)
        row_assign = jnp.full((n_rows,), n_assign, I32).at[dest].set(
            jnp.arange(n_assign, dtype=I32), unique_indices=True)
        is_real = row_assign < n_assign
        src_tok = jnp.where(is_real, row_assign // TOP_K, 0)
        spare = n_assign + jnp.arange(n_rows, dtype=I32) % (2 * EXP_ROWS)
        dst_row = jnp.where(is_real, (row_assign % TOP_K) * N + row_assign // TOP_K, spare)

        y_rows = _fused_experts(l, block_e, n_used.astype(I32), src_tok, dst_row, h_rows,
                                w_gu, b_gu, w_dn, b_dn, EXP_ROWS, n_assign)
        x = _dense_combine(y_rows, gates, x1, mod, post_ffn_g[l].reshape(1, D))
    return x
```

```python
import functools

import jax
import jax.numpy as jnp
from jax import lax
from jax.experimental import pallas as pl
from jax.experimental.pallas import tpu as pltpu

F32 = jnp.float32
BF16 = jnp.bfloat16
I32 = jnp.int32

D = 1024
LRU_W = 256
LRU_BLOCK_W = 64
CONV_W = 4
LRU_C = 8.0
S5_W = 256
S5_GROUP_W = 16
S5_GROUPS = 16
S5_STATE = 64
S5_STATES = S5_GROUPS * S5_STATE
HEAD_DIM = 64
ATTN_W = 512
N_HEADS = 8
O_LRU_G = 256
O_S5 = 512
O_Q = 768
O_K = O_Q + ATTN_W
O_V = O_K + ATTN_W
O_F = O_V + ATTN_W
N_EXPERTS = 32
TOP_K = 4
D_FF = 1024
SWIGLU_LIMIT = 7.0
SWIGLU_ALPHA = 1.702
N_MOD = 6
RMS_EPS = 1e-6

LANES = 128
SUBLANES = 8
VMEM_LIMIT = 56 * 1024 * 1024
ROW_CHUNKS = D // LANES

C_Q = 768
C_K = C_Q + N_HEADS * LANES
C_V = C_K + N_HEADS * LANES
C_F = C_V + N_HEADS * LANES
IN_COLS_PAD = C_F + LANES
L_A0, L_A1, L_A2, L_B0, L_B1, L_B2 = 64, 65, 66, 67, 68, 69

TM_IN = 512
SUBTILES = 2
OUT_SUBTILES = 1
TC_LRU = 256
TC_S5 = 256
TQ = 512
ATTN_HEADS_PER_STEP = 4
TM_OUT = 512
TN_ROWS = 1024
TN_COMB = 256
EXP_ROWS = 256
MAP_BLOCK = 1024
NEG_BIG = -1e30


def _rms(x, g):
    return x * lax.rsqrt(jnp.mean(x * x, axis=-1, keepdims=True) + RMS_EPS) * g


def _shift_rows(v, d, row, fill):
    return jnp.where(row >= d, pltpu.roll(v, d, 0), fill)


def _params(vmem=None):
    return pltpu.CompilerParams(vmem_limit_bytes=vmem) if vmem else None


def _mod_kernel(c_ref, w_ref, b_ref, o_ref):
    c = c_ref[...]
    s = c * jax.nn.sigmoid(c)
    o_ref[0] = jnp.dot(s.astype(BF16), w_ref[0].astype(BF16), preferred_element_type=F32) + b_ref[0]


def _modulation(c, ada_w, ada_b):
    L, _, W = ada_w.shape
    B = c.shape[0]
    return pl.pallas_call(
        _mod_kernel,
        grid=(L, W // D),
        in_specs=[pl.BlockSpec((B, D), lambda l, j: (0, 0)),
                  pl.BlockSpec((1, D, D), lambda l, j: (l, 0, j)),
                  pl.BlockSpec((1, 1, D), lambda l, j: (l, 0, j))],
        out_specs=pl.BlockSpec((1, B, D), lambda l, j: (l, 0, j)),
        out_shape=jax.ShapeDtypeStruct((L, B, W), F32),
        name="adaln_mod",
    )(c, ada_w, ada_b.reshape(L, 1, W))


def _in_proj_kernel(x_ref, mod_ref, g_ref, w_ref, fb_ref,
                    lx_ref, lg_ref, su_ref, q_ref, k_ref, v_ref, fcarry_ref, *, tm):
    @pl.when(pl.program_id(1) == 0)
    def _():
        fcarry_ref[...] = jnp.zeros_like(fcarry_ref)

    mod = mod_ref[0]
    ts = tm // SUBTILES
    row = lax.broadcasted_iota(I32, (ts, LANES), 0)
    lane = lax.broadcasted_iota(I32, (ts, N_HEADS * LANES), 1) & (LANES - 1)
    carry = fcarry_ref[...]
    for sub in range(SUBTILES):
        rows = slice(sub * ts, (sub + 1) * ts)
        h = _rms(x_ref[0, rows, :], g_ref[...]) * (1.0 + mod[1:2]) + mod[0:1]
        p = jnp.dot(h.astype(BF16), w_ref[...], preferred_element_type=F32)
        lx_ref[0, rows, :] = p[:, 0:O_LRU_G]
        lg_ref[0, rows, :] = p[:, O_LRU_G:O_S5]
        su_ref[0, rows, :] = p[:, O_S5:O_Q]

        z = p[:, C_F:C_F + LANES] + fb_ref[...]
        logf = jnp.minimum(z, 0.0) - jnp.log1p(jnp.exp(-jnp.abs(z)))
        d = 1
        while d < ts:
            logf = logf + _shift_rows(logf, d, row, 0.0)
            d *= 2
        fsum = logf + carry
        carry = fsum[ts - 1:ts, :]

        fexp = jnp.concatenate([jnp.broadcast_to(fsum[:, hd:hd + 1], (ts, LANES)) for hd in range(N_HEADS)],
                               axis=1)
        hi = fexp.astype(BF16).astype(F32)
        r1 = fexp - hi
        mid = r1.astype(BF16).astype(F32)
        lo = r1 - mid
        pq = p[:, C_Q:C_K] * (HEAD_DIM ** -0.5)
        pk = p[:, C_K:C_V]
        pv = p[:, C_V:C_F]
        ones = (lane >= L_B0) & (lane <= L_B2)
        qa = jnp.where(lane == L_A0, hi, jnp.where(lane == L_A1, mid, jnp.where(lane == L_A2, lo,
                                                                                jnp.where(ones, 1.0, pq))))
        ones = (lane >= L_A0) & (lane <= L_A2)
        ka = jnp.where(lane == L_B0, -hi, jnp.where(lane == L_B1, -mid, jnp.where(lane == L_B2, -lo,
                                                                                  jnp.where(ones, 1.0, pk))))
        va = jnp.where(lane == HEAD_DIM, 1.0, pv)
        for hd in range(N_HEADS):
            sl = slice(hd * LANES, (hd + 1) * LANES)
            q_ref[0, hd, rows, :] = qa[:, sl].astype(BF16)
            k_ref[0, hd, rows, :] = ka[:, sl].astype(BF16)
            v_ref[0, hd, rows, :] = va[:, sl].astype(BF16)
    fcarry_ref[...] = carry


def _in_proj(x, mod, g, w_aug, fb):
    B, T, _ = x.shape
    tm = min(TM_IN, T)
    grp = jax.ShapeDtypeStruct((B, T, 256), F32)
    head = jax.ShapeDtypeStruct((B, N_HEADS, T, LANES), BF16)
    grp_spec = pl.BlockSpec((1, tm, 256), lambda b, t: (b, t, 0))
    head_spec = pl.BlockSpec((1, N_HEADS, tm, LANES), lambda b, t: (b, 0, t, 0))
    return pl.pallas_call(
        functools.partial(_in_proj_kernel, tm=tm),
        grid=(B, T // tm),
        in_specs=[pl.BlockSpec((1, tm, D), lambda b, t: (b, t, 0)),
                  pl.BlockSpec((1, N_MOD, D), lambda b, t: (b, 0, 0)),
                  pl.BlockSpec((1, D), lambda b, t: (0, 0)),
                  pl.BlockSpec((D, IN_COLS_PAD), lambda b, t: (0, 0)),
                  pl.BlockSpec((1, LANES), lambda b, t: (0, 0))],
        out_specs=[grp_spec, grp_spec, grp_spec, head_spec, head_spec, head_spec],
        out_shape=[grp, grp, grp, head, head, head],
        scratch_shapes=[pltpu.VMEM((1, LANES), F32)],
        compiler_params=_params(VMEM_LIMIT),
        name="in_proj",
    )(x, mod, g, w_aug, fb)


def _lru_kernel(lx_ref, lg_ref, cw_ref, cb_ref, wa_ref, ba_ref, wx_ref, bx_ref, lam_ref, gn_ref,
                y_ref, tail_ref, hcarry_ref, *, tc):
    @pl.when(pl.program_id(1) == 0)
    def _():
        tail_ref[...] = jnp.zeros_like(tail_ref)
        hcarry_ref[...] = jnp.zeros_like(hcarry_ref)

    x = lx_ref[0]
    xcat = jnp.concatenate([tail_ref[...], x], axis=0)
    cw = cw_ref[...]
    xr = cb_ref[...]
    for j in range(CONV_W):
        d = CONV_W - 1 - j
        xs = x if d == 0 else pltpu.roll(xcat, d, 0)[SUBLANES:SUBLANES + tc]
        xr = xr + xs * cw[j:j + 1]
    tail_ref[...] = x[tc - SUBLANES:tc]

    xb = xr.astype(BF16)
    r = jax.nn.sigmoid(jnp.dot(xb, wa_ref[...], preferred_element_type=F32) + ba_ref[...])
    i = jax.nn.sigmoid(jnp.dot(xb, wx_ref[...], preferred_element_type=F32) + bx_ref[...])
    nl = -lam_ref[...]
    softplus = jnp.maximum(nl, 0.0) + jnp.log1p(jnp.exp(-jnp.abs(nl)))
    log_a = -LRU_C * r * softplus
    a = jnp.exp(log_a)
    th = jnp.tanh(log_a)
    u = xr * i * jnp.sqrt(-2.0 * th / (1.0 - th))

    row = lax.broadcasted_iota(I32, (tc, LRU_W), 0)
    d = 1
    while d < tc:
        u = u + a * _shift_rows(u, d, row, 0.0)
        a = a * _shift_rows(a, d, row, 1.0)
        d *= 2
    h = u + a * hcarry_ref[...]
    hcarry_ref[...] = h[tc - 1:tc]

    y = h * jax.nn.gelu(lg_ref[0])
    y_ref[0] = _rms(y, gn_ref[...]).astype(BF16)


def _lru_branch(lx, lg, conv_w, conv_b, wa, ba, wx, bx, lam, gn):
    B, T, _ = lx.shape
    tc = min(TC_LRU, T)
    tile = pl.BlockSpec((1, tc, LRU_W), lambda b, t: (b, t, 0))
    vec = pl.BlockSpec((1, LRU_W), lambda b, t: (0, 0))
    mat = pl.BlockSpec((LRU_W, LRU_W), lambda b, t: (0, 0))
    return pl.pallas_call(
        functools.partial(_lru_kernel, tc=tc),
        grid=(B, T // tc),
        in_specs=[tile, tile, pl.BlockSpec((CONV_W, LRU_W), lambda b, t: (0, 0)), vec, mat, vec, mat, vec, vec, vec],
        out_specs=tile,
        out_shape=jax.ShapeDtypeStruct((B, T, LRU_W), BF16),
        scratch_shapes=[pltpu.VMEM((SUBLANES, LRU_W), F32), pltpu.VMEM((1, LRU_W), F32)],
        name="rg_lru",
    )(lx, lg, conv_w, conv_b, wa, ba, wx, bx, lam, gn)


def _s5_kernel(u_ref, bcat_ref, apr_ref, api_ref, p8r_ref, p8i_ref, bpr_ref, bpi_ref, ccat_ref, d_ref, gw_ref,
               gb_ref, gn_ref, y_ref, cr_ref, ci_ref, xsr_ref, xsi_ref, gr_ref, gi_ref, *, tc):
    @pl.when(pl.program_id(1) == 0)
    def _():
        cr_ref[...] = jnp.zeros_like(cr_ref)
        ci_ref[...] = jnp.zeros_like(ci_ref)

    def cmul_add(xr, xi, ar, ai, sr, si):
        return xr + (ar * sr - ai * si), xi + (ar * si + ai * sr)

    u = u_ref[0]
    bu = jnp.dot(u.astype(BF16), bcat_ref[...], preferred_element_type=F32)
    xr = bu[:, :S5_STATES]
    xi = bu[:, S5_STATES:]
    nb = tc // SUBLANES
    blocks = (nb, SUBLANES, S5_STATES)
    sub = lax.broadcasted_iota(I32, blocks, 1)
    xr = xr.reshape(blocks)
    xi = xi.reshape(blocks)
    for k in range(3):
        d = 1 << k
        ar = apr_ref[k:k + 1, :].reshape(1, 1, S5_STATES)
        ai = api_ref[k:k + 1, :].reshape(1, 1, S5_STATES)
        xr, xi = cmul_add(xr, xi, ar, ai, jnp.where(sub >= d, pltpu.roll(xr, d, 1), 0.0),
                          jnp.where(sub >= d, pltpu.roll(xi, d, 1), 0.0))
    xr = xr.reshape(tc, S5_STATES)
    xi = xi.reshape(tc, S5_STATES)
    lane_tiles = range(S5_STATES // LANES)

    def put(ref, v):
        for l in lane_tiles:
            ref[l] = v[:, l * LANES:(l + 1) * LANES]

    put(xsr_ref, xr)
    put(xsi_ref, xi)
    er =jnp.concatenate([xsr_ref[l, pl.ds(SUBLANES - 1, nb, stride=SUBLANES), :] for l in lane_tiles], axis=1)
    ei = jnp.concatenate([xsi_ref[l, pl.ds(SUBLANES - 1, nb, stride=SUBLANES), :] for l in lane_tiles], axis=1)
    blk = lax.broadcasted_iota(I32, (nb, S5_STATES), 0)
    k, d = 3, 1
    while d < nb:
        er, ei = cmul_add(er, ei, apr_ref[k:k + 1, :], api_ref[k:k + 1, :],
                          _shift_rows(er, d, blk, 0.0), _shift_rows(ei, d, blk, 0.0))
        k, d = k + 1, d * 2
    gr, gi = cmul_add(_shift_rows(er, 1, blk, 0.0), _shift_rows(ei, 1, blk, 0.0),
                      bpr_ref[...], bpi_ref[...], cr_ref[...], ci_ref[...])
    for j in range(SUBLANES):
        for l in lane_tiles:
            gr_ref[l, pl.ds(j, nb, stride=SUBLANES), :] = gr[:, l * LANES:(l + 1) * LANES]
            gi_ref[l, pl.ds(j, nb, stride=SUBLANES), :] = gi[:, l * LANES:(l + 1) * LANES]
    xr, xi = cmul_add(xr, xi, pltpu.repeat(p8r_ref[...], nb, 0), pltpu.repeat(p8i_ref[...], nb, 0),
                      jnp.concatenate([gr_ref[l] for l in lane_tiles], axis=1),
                      jnp.concatenate([gi_ref[l] for l in lane_tiles], axis=1))
    cr_ref[...] = xr[tc - 1:tc]
    ci_ref[...] = xi[tc - 1:tc]

    xcat = jnp.concatenate([xr, xi], axis=1).astype(BF16)
    y = jnp.dot(xcat, ccat_ref[...], preferred_element_type=F32) + d_ref[...] * u
    y = jax.nn.gelu(y)
    y = y * jax.nn.sigmoid(jnp.dot(y.astype(BF16), gw_ref[...], preferred_element_type=F32) + gb_ref[...])
    y_ref[0] = _rms(y, gn_ref[...]).astype(BF16)


def _s5_branch(u, bcat, apow_re, apow_im, p8_re, p8_im, bp_re, bp_im, ccat, d, glu_w, glu_b, gn, tc):
    B, T, _ = u.shape
    tile = pl.BlockSpec((1, tc, S5_W), lambda b, t: (b, t, 0))
    vec = pl.BlockSpec((1, S5_W), lambda b, t: (0, 0))

    def full(a):
        return pl.BlockSpec(a.shape, lambda b, t: (0, 0))

    return pl.pallas_call(
        functools.partial(_s5_kernel, tc=tc),
        grid=(B, T // tc),
        in_specs=[tile, full(bcat), full(apow_re), full(apow_im), full(p8_re), full(p8_im), full(bp_re),
                  full(bp_im), full(ccat), vec, full(glu_w), vec, vec],
        out_specs=tile,
        out_shape=jax.ShapeDtypeStruct((B, T, S5_W), BF16),
        scratch_shapes=[pltpu.VMEM((1, S5_STATES), F32), pltpu.VMEM((1, S5_STATES), F32)]
        + [pltpu.VMEM((S5_STATES // LANES, tc, LANES), F32)] * 4,
        compiler_params=_params(VMEM_LIMIT),
        name="s5",
    )(u, bcat, apow_re, apow_im, p8_re, p8_im, bp_re, bp_im, ccat, d, glu_w, glu_b, gn)


def _s5_tables(a_re, a_im, b_re, b_im, c_re, c_im, log_dt, tc):
    dt = jnp.exp(log_dt)[:, None]
    mag = jnp.exp(a_re * dt)
    abar_re = mag * jnp.cos(a_im * dt)
    abar_im = mag * jnp.sin(a_im * dt)
    den = a_re * a_re + a_im * a_im
    num_re = abar_re - 1.0
    k_re = (num_re * a_re + abar_im * a_im) / den
    k_im = (abar_im * a_re - num_re * a_im) / den
    bbar_re = k_re[..., None] * b_re - k_im[..., None] * b_im
    bbar_im = k_re[..., None] * b_im + k_im[..., None] * b_re
    eye = jnp.eye(S5_GROUPS, dtype=F32)
    bd_re = jnp.einsum('gpc,gh->gchp', bbar_re, eye).reshape(S5_W, S5_STATES)
    bd_im = jnp.einsum('gpc,gh->gchp', bbar_im, eye).reshape(S5_W, S5_STATES)
    bcat = jnp.concatenate([bd_re, bd_im], axis=1).astype(BF16)
    cd_re = jnp.einsum('gcp,gh->gphc', c_re, eye).reshape(S5_STATES, S5_W)
    cd_im = jnp.einsum('gcp,gh->gphc', c_im, eye).reshape(S5_STATES, S5_W)
    ccat = jnp.concatenate([cd_re, -cd_im], axis=0).astype(BF16)
    ar = abar_re.reshape(1, S5_STATES)
    ai = abar_im.reshape(1, S5_STATES)
    pows_r, pows_i = [ar], [ai]
    tab_r, tab_i = ar, ai
    n = 1
    while n < tc:
        sr, si = pows_r[-1], pows_i[-1]
        tab_r, tab_i = (jnp.concatenate([tab_r, tab_r * sr - tab_i * si], axis=0),
                        jnp.concatenate([tab_i, tab_r * si + tab_i * sr], axis=0))
        pows_r.append(sr * sr - si * si)
        pows_i.append(2.0 * sr * si)
        n *= 2
    levels = len(pows_r) - 1
    pad = (-levels) % SUBLANES
    apow_re = jnp.concatenate(pows_r[:levels] + [jnp.zeros((pad, S5_STATES), F32)], axis=0)
    apow_im = jnp.concatenate(pows_i[:levels] + [jnp.zeros((pad, S5_STATES), F32)], axis=0)
    nb = tc // SUBLANES
    bp_re = jnp.concatenate([jnp.ones((1, S5_STATES), F32), tab_r[SUBLANES - 1::SUBLANES][:nb - 1]], axis=0)
    bp_im = jnp.concatenate([jnp.zeros((1, S5_STATES), F32), tab_i[SUBLANES - 1::SUBLANES][:nb - 1]], axis=0)
    return bcat, apow_re, apow_im, tab_r[:SUBLANES], tab_i[:SUBLANES], bp_re, bp_im, ccat


def _attn_kernel(qt_ref, k_ref, vt_ref, o_ref, p_ref, *, tq):
    qi = pl.program_id(2)
    key = lax.broadcasted_iota(I32, (tq, tq), 0)
    qry = lax.broadcasted_iota(I32, (tq, tq), 1)
    heads = range(ATTN_HEADS_PER_STEP)
    qts =[qt_ref[0, hh] for hh in heads]

    def scores(hh, kj):
        k = k_ref[0, hh, pl.ds(pl.multiple_of(kj * tq, tq), tq), :]
        return jnp.dot(k, qts[hh], preferred_element_type=F32)

    def weighted_values(hh, kj, p):
        vt = vt_ref[0, hh, :, pl.ds(pl.multiple_of(kj * tq, tq), tq)]
        return jnp.dot(vt, p, preferred_element_type=F32)

    def softmax_step(s, m):
        m_new = jnp.maximum(m, jnp.max(s, axis=0, keepdims=True))
        return m_new, jnp.exp(m - m_new), jnp.exp(s - m_new).astype(BF16)

    p_ref[...] = jnp.zeros_like(p_ref)

    def body(kj, carry):
        ss = [scores(hh, kj) for hh in heads]
        pvs = [weighted_values(hh, jnp.maximum(kj - 1, 0), p_ref[hh]) for hh in heads]
        out = []
        for hh in heads:
            m, alpha, acc = carry[hh]
            acc = alpha * acc + pvs[hh]
            m, alpha, p = softmax_step(ss[hh], m)
            p_ref[hh] = p
            out.append((m, alpha, acc))
        return tuple(out)

    init = tuple((jnp.full((1, tq), NEG_BIG, F32), jnp.ones((1, tq), F32), jnp.zeros((LANES, tq), F32))
                 for _ in heads)
    carry = lax.fori_loop(0, qi, body, init)
    outs = []
    for hh in heads:
        m, alpha, acc = carry[hh]
        acc = alpha * acc + weighted_values(hh, jnp.maximum(qi - 1, 0), p_ref[hh])
        s = jnp.where(key <= qry, scores(hh, qi), NEG_BIG)
        m, alpha, p = softmax_step(s, m)
        acc = alpha * acc + weighted_values(hh, qi, p)
        out_t = acc / acc[HEAD_DIM:HEAD_DIM + 1, :]
        outs.append(out_t.T[:, :HEAD_DIM])
    o_ref[0] = jnp.concatenate(outs, axis=1)


def _attention(qa, ka, va):
    B, H, T, _ = qa.shape
    tq = min(TQ, T)
    hs = ATTN_HEADS_PER_STEP
    qt = jnp.swapaxes(qa, 2, 3)
    vt = jnp.swapaxes(va, 2, 3)
    return pl.pallas_call(
        functools.partial(_attn_kernel, tq=tq),
        grid=(B, H // hs, T // tq),
        in_specs=[pl.BlockSpec((1, hs, LANES, tq), lambda b, hp, qi: (b, hp, 0, qi)),
                  pl.BlockSpec((1, hs, T, LANES), lambda b, hp, qi: (b, hp, 0, 0)),
                  pl.BlockSpec((1, hs, LANES, T), lambda b, hp, qi: (b, hp, 0, 0))],
        out_specs=pl.BlockSpec((1, tq, hs * HEAD_DIM), lambda b, hp, qi: (b, qi, hp)),
        out_shape=jax.ShapeDtypeStruct((B, T, ATTN_W), F32),
        scratch_shapes=[pltpu.VMEM((hs, tq, tq), BF16)],
        compiler_params=_params(VMEM_LIMIT),
        name="fox_attention",
    )(qt, ka, vt)


def _out_proj_kernel(yl_ref, ys_ref, ya_ref, x_ref, mod_ref, gna_ref, wout_ref, pmg_ref, pfg_ref,
                     rwh_ref, rwl_ref, rb_ref,
                     x1_ref, h2_ref, idx_ref, gate_ref, rank_ref, cnt_ref, carry_ref, *, tm):
    @pl.when((pl.program_id(0) == 0) & (pl.program_id(1) == 0))
    def _():
        carry_ref[...] = jnp.zeros_like(carry_ref)

    mod = mod_ref[0]
    ts = tm // OUT_SUBTILES
    lane = lax.broadcasted_iota(I32, (ts, LANES), 1)
    lane_f = lane.astype(F32)
    r_i = lax.broadcasted_iota(I32, (ts, ts), 0)
    c_i = lax.broadcasted_iota(I32, (ts, ts), 1)
    below = jnp.where(c_i < r_i, 1.0, 0.0).astype(BF16)
    carry = carry_ref[...]
    for sub in range(OUT_SUBTILES):
        rows = slice(sub * ts, (sub + 1) * ts)
        ya = _rms(ya_ref[0, rows, :], gna_ref[...])
        ycat = jnp.concatenate([yl_ref[0, rows, :], ys_ref[0, rows, :], ya.astype(BF16)], axis=1)
        y = jnp.dot(ycat, wout_ref[...], preferred_element_type=F32)
        x1 = x_ref[0, rows, :] + mod[2:3] * _rms(y, pmg_ref[...])
        x1_ref[0, rows, :] = x1
        h2 = _rms(x1, pfg_ref[...]) * (1.0 + mod[4:5]) + mod[3:4]
        for c in range(ROW_CHUNKS):
            h2_ref[pl.ds(sub * ts * ROW_CHUNKS + c, ts, stride=ROW_CHUNKS), :] = h2[:, c * LANES:(c + 1) * LANES]

        hh = h2.astype(BF16)
        hl = (h2 - hh.astype(F32)).astype(BF16)
        rwh = rwh_ref[...]
        logits = (jnp.dot(hh, rwh, preferred_element_type=F32) + jnp.dot(hl, rwh, preferred_element_type=F32)
                  + jnp.dot(hh, rwl_ref[...], preferred_element_type=F32) + rb_ref[...])
        work = jnp.where(lane < N_EXPERTS, logits, NEG_BIG)
        vals, idxs = [], []
        for _ in range(TOP_K):
            mx = jnp.max(work, axis=1, keepdims=True)
            ik = jnp.min(jnp.where(work == mx, lane_f, float(LANES)), axis=1, keepdims=True)
            vals.append(mx)
            idxs.append(ik)
            work = jnp.where(lane_f == ik, 2.0 * NEG_BIG, work)
        es = [jnp.exp(v - vals[0]) for v in vals]
        den = es[0] + es[1] + es[2] + es[3]

        onehot = jnp.zeros((ts, LANES), F32)
        for ik in idxs:
            onehot = onehot + jnp.where(lane_f == ik, 1.0, 0.0)
        prior = jnp.dot(below, onehot.astype(BF16), preferred_element_type=F32) + carry
        carry = carry + jnp.sum(onehot, axis=0, keepdims=True)

        idx_full = jnp.zeros((ts, LANES), F32)
        gate_full = jnp.zeros((ts, LANES), F32)
        rank_full = jnp.zeros((ts, LANES), F32)
        for k in range(TOP_K):
            rk = jnp.sum(jnp.where(lane_f == idxs[k], prior, 0.0), axis=1, keepdims=True)
            idx_full = jnp.where(lane == k, idxs[k], idx_full)
            gate_full = jnp.where(lane == k, es[k] / den, gate_full)
            rank_full = jnp.where(lane == k, rk, rank_full)
        idx_ref[rows, :] = idx_full[:, :TOP_K].astype(I32)
        gate_ref[rows, :] = gate_full[:, :TOP_K]
        rank_ref[rows, :] = rank_full[:, :TOP_K].astype(I32)
    carry_ref[...] = carry
    cnt_ref[...] = carry.astype(I32)


def _out_proj(yl, ys, ya, x, mod, gna, wout, pmg, pfg, rwh, rwl, rb):
    B, T, _ = x.shape
    tm = min(TM_OUT, T)
    nt = T // tm
    N = B * T

    def vec(n):
        return pl.BlockSpec((1, n), lambda b, t: (0, 0))

    tok = lambda b, t: (b * nt + t, 0)
    return pl.pallas_call(
        functools.partial(_out_proj_kernel, tm=tm),
        grid=(B, nt),
        in_specs=[pl.BlockSpec((1, tm, LRU_W), lambda b, t: (b, t, 0)),
                  pl.BlockSpec((1, tm, S5_W), lambda b, t: (b, t, 0)),
                  pl.BlockSpec((1, tm, ATTN_W), lambda b, t: (b, t, 0)),
                  pl.BlockSpec((1, tm, D), lambda b, t: (b, t, 0)),
                  pl.BlockSpec((1, N_MOD, D), lambda b, t: (b, 0, 0)),
                  vec(ATTN_W),
                  pl.BlockSpec((D, D), lambda b, t: (0, 0)),
                  vec(D), vec(D),
                  pl.BlockSpec((D, LANES), lambda b, t: (0, 0)),
                  pl.BlockSpec((D, LANES), lambda b, t: (0, 0)),
                  vec(LANES)],
        out_specs=[pl.BlockSpec((1, tm, D), lambda b, t: (b, t, 0)),
                   pl.BlockSpec((tm * ROW_CHUNKS, LANES), tok),
                   pl.BlockSpec((tm, TOP_K), tok),
                   pl.BlockSpec((tm, TOP_K), tok),
                   pl.BlockSpec((tm, TOP_K), tok),
                   pl.BlockSpec((1, LANES), lambda b, t: (0, 0))],
        out_shape=[jax.ShapeDtypeStruct((B, T, D), F32),
                   jax.ShapeDtypeStruct((N * ROW_CHUNKS, LANES), F32),
                   jax.ShapeDtypeStruct((N, TOP_K), I32),
                   jax.ShapeDtypeStruct((N, TOP_K), F32),
                   jax.ShapeDtypeStruct((N, TOP_K), I32),
                   jax.ShapeDtypeStruct((1, LANES), I32)],
        scratch_shapes=[pltpu.VMEM((1, LANES), F32)],
        compiler_params=_params(VMEM_LIMIT),
        name="out_proj_router",
    )(yl, ys, ya, x, mod, gna, wout, pmg, pfg, rwh, rwl, rb)


def _row_copy(src, src_row, dst, dst_row, sem):
    return pltpu.make_async_copy(src.at[pl.ds(pl.multiple_of(src_row * ROW_CHUNKS, ROW_CHUNKS), ROW_CHUNKS), :],
                                 dst.at[pl.ds(pl.multiple_of(dst_row * ROW_CHUNKS, ROW_CHUNKS), ROW_CHUNKS), :],
                                 sem)


def _dispatch_kernel(start_ref, cnt_ref, pad_ref, idx_ref, rank_ref, h_ref, dest_ref, xb_ref, sem, *, tn):
    i = pl.program_id(0)

    def issue(j, _):
        for k in range(TOP_K):
            a = j * TOP_K + k
            dst = start_ref[idx_ref[a]] + rank_ref[a]
            dest_ref[a] = dst
            _row_copy(h_ref, j, xb_ref, dst, sem).start()
        return 0

    lax.fori_loop(0, tn, issue, 0)

    def drain(j, _):
        for k in range(TOP_K):
            _row_copy(h_ref, 0, xb_ref, 0, sem).wait()
        return 0

    lax.fori_loop(0, tn, drain, 0)

    @pl.when(i == pl.num_programs(0) - 1)
    def _():
        def per_expert(e, _):
            lo = start_ref[e] + cnt_ref[e]
            hi = start_ref[e] + pad_ref[e]

            def fill(r, _):
                _row_copy(h_ref, 0, xb_ref, r, sem).start()
                return 0

            lax.fori_loop(lo, hi, fill, 0)

            def fill_wait(r, _):
                _row_copy(h_ref, 0, xb_ref, r, sem).wait()
                return 0

            lax.fori_loop(lo, hi, fill_wait, 0)
            return 0

        lax.fori_loop(0, N_EXPERTS, per_expert, 0)


def _dispatch(starts, counts, padded, idx_flat, rank_flat, h_rows, n_rows):
    NK = idx_flat.shape[0]
    N = NK // TOP_K
    tn = min(TN_ROWS, N)
    smem_blk = pl.BlockSpec((tn * TOP_K,), lambda i, *_: (i,), memory_space=pltpu.SMEM)
    grid_spec = pltpu.PrefetchScalarGridSpec(
        num_scalar_prefetch=3,
        grid=(N // tn,),
        in_specs=[smem_blk, smem_blk, pl.BlockSpec((tn * ROW_CHUNKS, LANES), lambda i, *_: (i, 0))],
        out_specs=[smem_blk, pl.BlockSpec(memory_space=pl.ANY)],
        scratch_shapes=[pltpu.SemaphoreType.DMA(())],
    )
    return pl.pallas_call(
        functools.partial(_dispatch_kernel, tn=tn),
        grid_spec=grid_spec,
        out_shape=[jax.ShapeDtypeStruct((NK,), I32),
                   jax.ShapeDtypeStruct((n_rows * ROW_CHUNKS, LANES), F32)],
        name="moe_dispatch",
    )(starts, counts, padded, idx_flat, rank_flat, h_rows)


def _expert_kernel(be_ref, nused_ref, x_ref, wgu_ref, bgu_ref, wdn_ref, bdn_ref, y_ref, wgu_s, wdn_s, *, rows):
    b = pl.program_id(0)
    prev = be_ref[jnp.maximum(b - 1, 0)]

    @pl.when((b == 0) | (be_ref[b] != prev))
    def _():
        wgu_s[...] = wgu_ref[0, 0].astype(BF16)
        wdn_s[...] = wdn_ref[0, 0].astype(BF16)

    @pl.when(b < nused_ref[0])
    def _():
        x = jnp.concatenate([x_ref[pl.ds(c, rows, stride=ROW_CHUNKS), :] for c in range(ROW_CHUNKS)], axis=1)
        gu = jnp.dot(x.astype(BF16), wgu_s[...], preferred_element_type=F32) + bgu_ref[0, 0]
        g = jnp.minimum(gu[:, :D_FF], SWIGLU_LIMIT)
        up = jnp.clip(gu[:, D_FF:], -SWIGLU_LIMIT, SWIGLU_LIMIT)
        act = (up + 1.0) * (g * jax.nn.sigmoid(SWIGLU_ALPHA * g))
        y = jnp.dot(act.astype(BF16), wdn_s[...], preferred_element_type=F32) + bdn_ref[0, 0]
        for c in range(ROW_CHUNKS):
            y_ref[pl.ds(c, rows, stride=ROW_CHUNKS), :] = y[:, c * LANES:(c + 1) * LANES]


def _experts(layer, block_e, n_used, xb, w_gu, b_gu, w_dn, b_dn, rows):
    n_blocks = block_e.shape[0]
    L, E = w_gu.shape[:2]
    blk = lambda b, be, nu: (jnp.minimum(b, nu[0] - 1), 0)
    wsel = lambda b, be, nu: (layer, be[b], 0, 0)
    grid_spec = pltpu.PrefetchScalarGridSpec(
        num_scalar_prefetch=2,
        grid=(n_blocks,),
        in_specs=[pl.BlockSpec((rows * ROW_CHUNKS, LANES), blk),
                  pl.BlockSpec((1, 1, D, 2 * D_FF), wsel),
                  pl.BlockSpec((1, 1, 1, 2 * D_FF), wsel),
                  pl.BlockSpec((1, 1, D_FF, D), wsel),
                  pl.BlockSpec((1, 1, 1, D), wsel)],
        out_specs=pl.BlockSpec((rows * ROW_CHUNKS, LANES), blk),
        scratch_shapes=[pltpu.VMEM((D, 2 * D_FF), BF16), pltpu.VMEM((D_FF, D), BF16)],
    )
    return pl.pallas_call(
        functools.partial(_expert_kernel, rows=rows),
        grid_spec=grid_spec,
        out_shape=jax.ShapeDtypeStruct(xb.shape, F32),
        compiler_params=_params(VMEM_LIMIT),
        name="moe_experts",
    )(block_e, n_used, xb, w_gu, b_gu.reshape(L, E, 1, 2 * D_FF), w_dn, b_dn.reshape(L, E, 1, D))


def _combine_kernel(dest_ref, yb_ref, gate_ref, x1_ref, mod_ref, g_ref, o_ref, buf_ref, sem, *, tn):
    def issue(j, _):
        for k in range(TOP_K):
            pltpu.make_async_copy(
                yb_ref.at[pl.ds(pl.multiple_of(dest_ref[j * TOP_K + k] * ROW_CHUNKS, ROW_CHUNKS), ROW_CHUNKS), :],
                buf_ref.at[k, pl.ds(pl.multiple_of(j * ROW_CHUNKS, ROW_CHUNKS), ROW_CHUNKS), :], sem).start()
        return 0

    lax.fori_loop(0, tn, issue, 0)

    def drain(j, _):
        for k in range(TOP_K):
            pltpu.make_async_copy(yb_ref.at[pl.ds(0, ROW_CHUNKS), :],
                                  buf_ref.at[k, pl.ds(0, ROW_CHUNKS), :], sem).wait()
        return 0

    lax.fori_loop(0, tn, drain, 0)

    gates = gate_ref[...]
    y = jnp.zeros((tn, D), F32)
    for k in range(TOP_K):
        yk = jnp.concatenate([buf_ref[k, pl.ds(c, tn, stride=ROW_CHUNKS), :] for c in range(ROW_CHUNKS)], axis=1)
        y = y + gates[:, k:k + 1] * yk
    mod = mod_ref[0]
    o_ref[0] = x1_ref[0] + mod[5:6] * _rms(y, g_ref[...])


def _combine(dest_flat, yb, gates, x1, mod, g):
    B, T, _ = x1.shape
    tn = min(TN_COMB, T)
    nt = T // tn
    grid_spec = pltpu.PrefetchScalarGridSpec(
        num_scalar_prefetch=0,
        grid=(B, nt),
        in_specs=[pl.BlockSpec((tn * TOP_K,), lambda b, t: (b * nt + t,), memory_space=pltpu.SMEM),
                  pl.BlockSpec(memory_space=pl.ANY),
                  pl.BlockSpec((tn, TOP_K), lambda b, t: (b * nt + t, 0)),
                  pl.BlockSpec((1, tn, D), lambda b, t: (b, t, 0)),
                  pl.BlockSpec((1, N_MOD, D), lambda b, t: (b, 0, 0)),
                  pl.BlockSpec((1, D), lambda b, t: (0, 0))],
        out_specs=pl.BlockSpec((1, tn, D), lambda b, t: (b, t, 0)),
        scratch_shapes=[pltpu.VMEM((TOP_K, tn * ROW_CHUNKS, LANES), F32), pltpu.SemaphoreType.DMA(())],
    )
    return pl.pallas_call(
        functools.partial(_combine_kernel, tn=tn),
        grid_spec=grid_spec,
        out_shape=jax.ShapeDtypeStruct((B, T, D), F32),
        name="moe_combine",
    )(dest_flat, yb, gates, x1, mod, g)


def _fused_expert_kernel(be_ref, nused_ref, src_ref, dst_ref, h_ref, wgu_ref, bgu_ref, wdn_ref, bdn_ref, y_ref,
                         xbuf, ybuf, wgu_s, wdn_s, gsem, ssem, *, rows, n_blocks, n_assign):
    b = pl.program_id(0)
    nu = nused_ref[0]
    slot = lax.rem(b, 2)
    other = 1 - slot
    xslot = lax.rem(b, 3)
    xnext = lax.rem(b + 2, 3)
    src_off = lax.rem(jnp.minimum(b + 2, n_blocks - 1) * rows, MAP_BLOCK)
    dst_off = lax.rem(jnp.maximum(b - 1, 0) * rows, MAP_BLOCK)

    def gather(i, tok, buf_slot):
        return pltpu.make_async_copy(
            h_ref.at[pl.ds(pl.multiple_of(tok * ROW_CHUNKS, ROW_CHUNKS), ROW_CHUNKS), :],
            xbuf.at[buf_slot, pl.ds(i * ROW_CHUNKS, ROW_CHUNKS), :], gsem.at[buf_slot])

    def scatter(i, dst, buf_slot):
        return pltpu.make_async_copy(
            ybuf.at[buf_slot, pl.ds(i * ROW_CHUNKS, ROW_CHUNKS), :],
            y_ref.at[pl.ds(pl.multiple_of(dst * ROW_CHUNKS, ROW_CHUNKS), ROW_CHUNKS), :], ssem.at[buf_slot])

    @pl.when(b == 0)
    def _():
        ybuf[...] = jnp.zeros(ybuf.shape, F32)
        for i in range(rows):
            gather(i, src_ref[i], 0).start()
            gather(i, src_ref[rows + i], 1).start()
            scatter(i, n_assign + 2 * rows + i, 0).start()

    @pl.when(b <= nu)
    def _():
        for i in range(rows):
            gather(i, 0, xslot).wait()

    prev = be_ref[jnp.maximum(b - 1, 0)]

    @pl.when((b == 0) | (be_ref[b] != prev))
    def _():
        wgu_s[...] = wgu_ref[0, 0].astype(BF16)
        wdn_s[...] = wdn_ref[0, 0].astype(BF16)

    @pl.when(b < nu)
    def _():
        x = jnp.concatenate([xbuf[xslot, pl.ds(c, rows, stride=ROW_CHUNKS), :] for c in range(ROW_CHUNKS)],
                            axis=1).astype(BF16)
        for i in range(rows):
            gather(i, src_ref[src_off + i], xnext).start(priority=i % 2)
        for i in range(rows):
            dst = jnp.where(b == 0, n_assign + 3 * rows + i, dst_ref[dst_off + i])
            scatter(i, dst, other).start(priority=i % 2)
        gu = jnp.dot(x, wgu_s[...], preferred_element_type=F32) + bgu_ref[0, 0]
        g = jnp.minimum(gu[:, :D_FF], SWIGLU_LIMIT)
        up = jnp.clip(gu[:, D_FF:], -SWIGLU_LIMIT, SWIGLU_LIMIT)
        act = (up + 1.0) * (g * jax.nn.sigmoid(SWIGLU_ALPHA * g))
        y = jnp.dot(act.astype(BF16), wdn_s[...], preferred_element_type=F32) + bdn_ref[0, 0]
        for i in range(rows):
            scatter(i, 0, slot).wait()
        for c in range(ROW_CHUNKS):
            ybuf[slot, pl.ds(c, rows, stride=ROW_CHUNKS), :] = y[:, c * LANES:(c + 1) * LANES]

    @pl.when(b == nu)
    def _():
        for i in range(rows):
            gather(i, 0, lax.rem(b + 1, 3)).wait()
        for i in range(rows):
            scatter(i, 0, slot).wait()
        for i in range(rows):
            scatter(i, dst_ref[dst_off + i], other).start()
        for i in range(rows):
            scatter(i, 0, other).wait()


FF_CHUNK = 256


def _expert_block_kernel(be_ref, nused_ref, src_ref, dst_ref, h_ref, wgu_ref, bgu_ref, wdn_ref, bdn_ref, y_ref,
                         xbuf, ybuf, wgu_s, wdn_s, gsem, ssem, *, rows, n_blocks, n_assign):
    b = pl.program_id(0)
    nu = nused_ref[0]
    slot = lax.rem(b, 2)
    other = 1 - slot
    src_off = lax.rem(jnp.minimum(b + 1, n_blocks - 1) * rows, MAP_BLOCK)
    dst_off = lax.rem(jnp.maximum(b - 1, 0) * rows, MAP_BLOCK)

    def gather(i, tok):
        return pltpu.make_async_copy(
            h_ref.at[pl.ds(pl.multiple_of(tok * ROW_CHUNKS, ROW_CHUNKS), ROW_CHUNKS), :],
            xbuf.at[pl.ds(i * ROW_CHUNKS, ROW_CHUNKS), :], gsem)

    def scatter(i, dst, buf_slot):
        return pltpu.make_async_copy(
            ybuf.at[buf_slot, pl.ds(i * ROW_CHUNKS, ROW_CHUNKS), :],
            y_ref.at[pl.ds(pl.multiple_of(dst * ROW_CHUNKS, ROW_CHUNKS), ROW_CHUNKS), :], ssem.at[buf_slot])

    @pl.when(b == 0)
    def _():
        ybuf[...] = jnp.zeros(ybuf.shape, F32)
        for i in range(rows):
            gather(i, src_ref[i]).start()
            scatter(i, n_assign + 2 * rows + i, 0).start()

    @pl.when(b <= nu)
    def _():
        for i in range(rows):
            gather(i, 0).wait()

    prev = be_ref[jnp.maximum(b - 1, 0)]

    @pl.when((b == 0) | (be_ref[b] != prev))
    def _():
        wgu_s[...] = wgu_ref[0, 0].astype(BF16)
        wdn_s[...] = wdn_ref[0, 0].astype(BF16)

    @pl.when(b < nu)
    def _():
        x = jnp.concatenate([xbuf[pl.ds(c, rows, stride=ROW_CHUNKS), :] for c in range(ROW_CHUNKS)],
                            axis=1).astype(BF16)
        n_chunks = D_FF // FF_CHUNK
        per_burst = rows // n_chunks

        def gather_burst(j):
            for i in range(j * per_burst, (j + 1) * per_burst):
                gather(i, src_ref[src_off + i]).start(priority=i % 2)

        def scatter_burst(j):
            for i in range(j * per_burst, (j + 1) * per_burst):
                dst = jnp.where(b == 0, n_assign + 3 * rows + i, dst_ref[dst_off + i])
                scatter(i, dst, other).start(priority=i % 2)

        bursts = [functools.partial(gather_burst, j) for j in range(n_chunks)]
        bursts += [functools.partial(scatter_burst, j) for j in range(n_chunks)]
        bgu = bgu_ref[0, 0]
        y = jnp.zeros((rows, D), F32) + bdn_ref[0, 0]
        for j in range(n_chunks):
            cols = slice(j * FF_CHUNK, (j + 1) * FF_CHUNK)
            ucols = slice(D_FF + j * FF_CHUNK, D_FF + (j + 1) * FF_CHUNK)
            g = jnp.dot(x, wgu_s[:, cols], preferred_element_type=F32) + bgu[:, cols]
            bursts[2 * j]()
            up = jnp.dot(x, wgu_s[:, ucols], preferred_element_type=F32) + bgu[:, ucols]
            g = jnp.minimum(g, SWIGLU_LIMIT)
            up = jnp.clip(up, -SWIGLU_LIMIT, SWIGLU_LIMIT)
            act = (up + 1.0) * (g * jax.nn.sigmoid(SWIGLU_ALPHA * g))
            y = y + jnp.dot(act.astype(BF16), wdn_s[cols, :], preferred_element_type=F32)
            bursts[2 * j + 1]()
        for i in range(rows):
            scatter(i, 0, slot).wait()
        for c in range(ROW_CHUNKS):
            ybuf[slot, pl.ds(c, rows, stride=ROW_CHUNKS), :] = y[:, c * LANES:(c + 1) * LANES]

    @pl.when(b == nu)
    def _():
        for i in range(rows):
            scatter(i, 0, slot).wait()
        for i in range(rows):
            scatter(i, dst_ref[dst_off + i], other).start()
        for i in range(rows):
            scatter(i, 0, other).wait()


def _fused_experts(layer, block_e, n_used, src_tok, dst_row, h_rows, w_gu, b_gu, w_dn, b_dn, rows, n_assign):
    n_blocks = block_e.shape[0]
    L, E = w_gu.shape[:2]
    wsel = lambda b, be, nu: (layer, be[b], 0, 0)
    grid_spec = pltpu.PrefetchScalarGridSpec(
        num_scalar_prefetch=2,
        grid=(n_blocks,),
        in_specs=[pl.BlockSpec((MAP_BLOCK,), lambda b, be, nu: (jnp.minimum(b + 2, n_blocks - 1) * rows // MAP_BLOCK,),
                               memory_space=pltpu.SMEM),
                  pl.BlockSpec((MAP_BLOCK,), lambda b, be, nu: (jnp.maximum(b - 1, 0) * rows // MAP_BLOCK,),
                               memory_space=pltpu.SMEM),
                  pl.BlockSpec(memory_space=pl.ANY),
                  pl.BlockSpec((1, 1, D, 2 * D_FF), wsel),
                  pl.BlockSpec((1, 1, 1, 2 * D_FF), wsel),
                  pl.BlockSpec((1, 1, D_FF, D), wsel),
                  pl.BlockSpec((1, 1, 1, D), wsel)],
        out_specs=pl.BlockSpec(memory_space=pl.ANY),
        scratch_shapes=[pltpu.VMEM((3, rows * ROW_CHUNKS, LANES), F32),
                        pltpu.VMEM((2, rows * ROW_CHUNKS, LANES), F32),
                        pltpu.VMEM((D, 2 * D_FF), BF16), pltpu.VMEM((D_FF, D), BF16),
                        pltpu.SemaphoreType.DMA((3,)), pltpu.SemaphoreType.DMA((2,))],
    )
    return pl.pallas_call(
        functools.partial(_fused_expert_kernel, rows=rows, n_blocks=n_blocks, n_assign=n_assign),
        grid_spec=grid_spec,
        out_shape=jax.ShapeDtypeStruct(((n_assign + 4 * rows) * ROW_CHUNKS, LANES), F32),
        compiler_params=_params(VMEM_LIMIT),
        name="moe_experts",
    )(block_e, n_used, src_tok, dst_row, h_rows, w_gu, b_gu.reshape(L, E, 1, 2 * D_FF), w_dn,
      b_dn.reshape(L, E, 1, D))


def _dense_combine_kernel(y0_ref, y1_ref, y2_ref, y3_ref, gate_ref, x1_ref, mod_ref, g_ref, o_ref, *, tn):
    gates = gate_ref[...]
    y = jnp.zeros((tn, D), F32)
    for k, yk_ref in enumerate((y0_ref, y1_ref, y2_ref, y3_ref)):
        yk = jnp.concatenate([yk_ref[pl.ds(c, tn, stride=ROW_CHUNKS), :] for c in range(ROW_CHUNKS)], axis=1)
        y = y + gates[:, k:k + 1] * yk
    mod = mod_ref[0]
    o_ref[0] = x1_ref[0] + mod[5:6] * _rms(y, g_ref[...])


def _dense_combine(y_rows, gates, x1, mod, g):
    B, T, _ = x1.shape
    tn = min(TN_COMB, T)
    nt = T // tn

    def slot_spec(k):
        return pl.BlockSpec((tn * ROW_CHUNKS, LANES), lambda b, t: (k * B * nt + b * nt + t, 0))

    return pl.pallas_call(
        functools.partial(_dense_combine_kernel, tn=tn),
        grid=(B, nt),
        in_specs=[slot_spec(0), slot_spec(1), slot_spec(2), slot_spec(3),
                  pl.BlockSpec((tn, TOP_K), lambda b, t: (b * nt + t, 0)),
                  pl.BlockSpec((1, tn, D), lambda b, t: (b, t, 0)),
                  pl.BlockSpec((1, N_MOD, D), lambda b, t: (b, 0, 0)),
                  pl.BlockSpec((1, D), lambda b, t: (0, 0))],
        out_specs=pl.BlockSpec((1, tn, D), lambda b, t: (b, t, 0)),
        out_shape=jax.ShapeDtypeStruct((B, T, D), F32),
        name="moe_combine",
    )(y_rows, y_rows, y_rows, y_rows, gates, x1, mod, g)


def _head_slabs(w):
    w = w.reshape(D, N_HEADS, HEAD_DIM)
    return jnp.pad(w, ((0, 0), (0, 0), (0, LANES - HEAD_DIM))).reshape(D, N_HEADS * LANES)


def _block_diag(w):
    nb = w.shape[0]
    return jnp.einsum('hij,hg->higj', w, jnp.eye(nb, dtype=w.dtype)).reshape(LRU_W, LRU_W)


def kernel(x, c, ada_w, ada_b, pre_mix_g, w_in, conv_w, conv_b, lru_wa, lru_ba, lru_wx, lru_bx, lru_lambda, s5_a_re, s5_a_im, s5_b_re, s5_b_im, s5_c_re, s5_c_im, s5_d, s5_log_dt, s5_glu_w, s5_glu_b, fox_fb, gn_lru, gn_s5, gn_attn, w_out, post_mix_g, pre_ffn_g, router_w, router_b, w_gu, b_gu, w_dn, b_dn, post_ffn_g):
    B, T, _ = x.shape
    N = B * T
    L = ada_w.shape[0]
    tc_s5 = min(TC_S5, T)
    n_blocks = (N * TOP_K) // EXP_ROWS + N_EXPERTS
    n_rows = n_blocks * EXP_ROWS
    mod_all = _modulation(c, ada_w, ada_b).reshape(L, B, N_MOD, D)

    for l in range(L):
        mod = mod_all[l]
        wl = w_in[l]
        w_aug = jnp.concatenate(
            [wl[:, :O_Q], _head_slabs(wl[:, O_Q:O_K]), _head_slabs(wl[:, O_K:O_V]), _head_slabs(wl[:, O_V:O_F]),
             jnp.pad(wl[:, O_F:], ((0, 0), (0, LANES - N_HEADS)))], axis=1).astype(BF16)
        fb = jnp.pad(fox_fb[l], (0, LANES - N_HEADS)).reshape(1, LANES)
        lx, lg, su, qa, ka, va = _in_proj(x, mod, pre_mix_g[l].reshape(1, D), w_aug, fb)

        yl = _lru_branch(lx, lg, conv_w[l], conv_b[l].reshape(1, LRU_W),
                         _block_diag(lru_wa[l]).astype(BF16), lru_ba[l].reshape(1, LRU_W),
                         _block_diag(lru_wx[l]).astype(BF16), lru_bx[l].reshape(1, LRU_W),
                         lru_lambda[l].reshape(1, LRU_W), gn_lru[l].reshape(1, LRU_W))

        tabs = _s5_tables(s5_a_re[l], s5_a_im[l], s5_b_re[l], s5_b_im[l], s5_c_re[l], s5_c_im[l], s5_log_dt[l], tc_s5)
        ys = _s5_branch(su, *tabs, s5_d[l].reshape(1, S5_W), s5_glu_w[l].astype(BF16),
                        s5_glu_b[l].reshape(1, S5_W), gn_s5[l].reshape(1, S5_W), tc_s5)

        ya = _attention(qa, ka, va)

        rw = jnp.pad(router_w[l], ((0, 0), (0, LANES - N_EXPERTS)))
        rwh = rw.astype(BF16)
        rwl = (rw - rwh.astype(F32)).astype(BF16)
        rb = jnp.pad(router_b[l], (0, LANES - N_EXPERTS)).reshape(1, LANES)
        x1, h_rows, idx, gates, rank, cnt = _out_proj(
            yl, ys, ya, x, mod, gn_attn[l].reshape(1, ATTN_W), w_out[l].astype(BF16),
            post_mix_g[l].reshape(1, D), pre_ffn_g[l].reshape(1, D), rwh, rwl, rb)

        counts = cnt[0, :N_EXPERTS]
        blocks_e = (counts + EXP_ROWS - 1) // EXP_ROWS
        padded = blocks_e * EXP_ROWS
        block_ends = jnp.cumsum(blocks_e)
        starts = (block_ends - blocks_e) * EXP_ROWS
        n_used = block_ends[-1:]
        bid = jnp.minimum(jnp.arange(n_blocks, dtype=I32), n_used[0] - 1)
        block_e = jnp.minimum(jnp.sum((bid[:, None] >= block_ends[None, :]).astype(I32), axis=1), N_EXPERTS - 1)

        n_assign = N * TOP_K
        dest = (jnp.take(starts.astype(I32), idx) + rank).reshape(n_assign)
        row_assign = jnp.full((n_rows,), n_assign, I32).at[dest].set(
            jnp.arange(n_assign, dtype=I32), unique_indices=True)
        is_real = row_assign < n_assign
        src_tok = jnp.where(is_real, row_assign // TOP_K, 0)
        spare = n_assign + jnp.arange(n_rows, dtype=I32) % (2 * EXP_ROWS)
        dst_row = jnp.where(is_real, (row_assign % TOP_K) * N + row_assign // TOP_K, spare)

        y_rows = _fused_experts(l, block_e, n_used.astype(I32), src_tok, dst_row, h_rows,
                                w_gu, b_gu, w_dn, b_dn, EXP_ROWS, n_assign)
        x = _dense_combine(y_rows, gates, x1, mod, post_ffn_g[l].reshape(1, D))
    return x
```

```python
import functools

import jax
import jax.numpy as jnp
from jax import lax
from jax.experimental import pallas as pl
from jax.experimental.pallas import tpu as pltpu

F32 = jnp.float32
BF16 = jnp.bfloat16
I32 = jnp.int32

D = 1024
LRU_W = 256
LRU_BLOCK_W = 64
CONV_W = 4
LRU_C = 8.0
S5_W = 256
S5_GROUP_W = 16
S5_GROUPS = 16
S5_STATE = 64
S5_STATES = S5_GROUPS * S5_STATE
HEAD_DIM = 64
ATTN_W = 512
N_HEADS = 8
O_LRU_G = 256
O_S5 = 512
O_Q = 768
O_K = O_Q + ATTN_W
O_V = O_K + ATTN_W
O_F = O_V + ATTN_W
N_EXPERTS = 32
TOP_K = 4
D_FF = 1024
SWIGLU_LIMIT = 7.0
SWIGLU_ALPHA = 1.702
N_MOD = 6
RMS_EPS = 1e-6

LANES = 128
SUBLANES = 8
VMEM_LIMIT = 56 * 1024 * 1024
ROW_CHUNKS = D // LANES

C_Q = O_Q
C_K = O_K
C_V = O_V
C_F = O_F
IN_COLS_PAD = C_F + LANES
L_A0, L_A1, L_A2, L_B0, L_B1, L_B2 = 64, 65, 66, 67, 68, 69

TM_IN = 512
SUBTILES = 2
OUT_SUBTILES = 1
TC_LRU = 256
TC_S5 = 256
TQ = 512
ATTN_HEADS_PER_STEP = 4
TM_OUT = 512
TN_COMB = 256
EXP_ROWS = 256
MAP_BLOCK = 1024
NEG_BIG = -1e30


def _rms(x, g):
    return x * lax.rsqrt(jnp.mean(x * x, axis=-1, keepdims=True) + RMS_EPS) * g


def _shift_rows(v, d, row, fill):
    return jnp.where(row >= d, pltpu.roll(v, d, 0), fill)


def _params(vmem=None):
    return pltpu.CompilerParams(vmem_limit_bytes=vmem) if vmem else None


def _mod_kernel(c_ref, w_ref, b_ref, o_ref):
    c = c_ref[...]
    s = c * jax.nn.sigmoid(c)
    o_ref[0] = jnp.dot(s.astype(BF16), w_ref[0].astype(BF16), preferred_element_type=F32) + b_ref[0]


def _modulation(c, ada_w, ada_b):
    L, _, W = ada_w.shape
    B = c.shape[0]
    return pl.pallas_call(
        _mod_kernel,
        grid=(L, W // D),
        in_specs=[pl.BlockSpec((B, D), lambda l, j: (0, 0)),
                  pl.BlockSpec((1, D, D), lambda l, j: (l, 0, j)),
                  pl.BlockSpec((1, 1, D), lambda l, j: (l, 0, j))],
        out_specs=pl.BlockSpec((1, B, D), lambda l, j: (l, 0, j)),
        out_shape=jax.ShapeDtypeStruct((L, B, W), F32),
        name="adaln_mod",
    )(c, ada_w, ada_b.reshape(L, 1, W))


def _in_proj_kernel(x_ref, mod_ref, g_ref, w_ref, fb_ref,
                    lx_ref, lg_ref, su_ref, q_ref, k_ref, v_ref, fcarry_ref, *, tm):
    @pl.when(pl.program_id(1) == 0)
    def _():
        fcarry_ref[...] = jnp.zeros_like(fcarry_ref)

    mod = mod_ref[0]
    ts = tm // SUBTILES
    row = lax.broadcasted_iota(I32, (ts, LANES), 0)
    lane = lax.broadcasted_iota(I32, (ts, LANES), 1)
    carry = fcarry_ref[...]
    for sub in range(SUBTILES):
        rows = slice(sub * ts, (sub + 1) * ts)
        h = _rms(x_ref[0, rows, :], g_ref[...]) * (1.0 + mod[1:2]) + mod[0:1]
        p = jnp.dot(h.astype(BF16), w_ref[...], preferred_element_type=F32)
        lx_ref[0, rows, :] = p[:, 0:O_LRU_G]
        lg_ref[0, rows, :] = p[:, O_LRU_G:O_S5]
        su_ref[0, rows, :] = p[:, O_S5:O_Q]

        z = p[:, C_F:C_F + LANES] + fb_ref[...]
        logf = jnp.minimum(z, 0.0) - jnp.log1p(jnp.exp(-jnp.abs(z)))
        d = 1
        while d < ts:
            logf = logf + _shift_rows(logf, d, row, 0.0)
            d *= 2
        fsum = logf + carry
        carry = fsum[ts - 1:ts, :]

        pq = p[:, C_Q:C_K] * (HEAD_DIM ** -0.5)
        pk = p[:, C_K:C_V]
        pv = p[:, C_V:C_F]
        for hd in range(N_HEADS):
            pair = slice((hd // 2) * LANES, (hd // 2 + 1) * LANES)

            def head_low(a):
                s = a[:, pair]
                return s if hd % 2 == 0 else pltpu.roll(s, HEAD_DIM, 1)

            f = jnp.broadcast_to(fsum[:, hd:hd + 1], (ts, LANES))
            hi = f.astype(BF16).astype(F32)
            r1 = f - hi
            mid = r1.astype(BF16).astype(F32)
            lo = r1 - mid
            q_aug = jnp.where(lane == L_A0, hi, jnp.where(lane == L_A1, mid, jnp.where(lane == L_A2, lo,
                              jnp.where((lane >= L_B0) & (lane <= L_B2), 1.0, 0.0))))
            k_aug = jnp.where(lane == L_B0, -hi, jnp.where(lane == L_B1, -mid, jnp.where(lane == L_B2, -lo,
                              jnp.where((lane >= L_A0) & (lane <= L_A2), 1.0, 0.0))))
            v_aug = jnp.where(lane == HEAD_DIM, 1.0, 0.0)
            q_ref[0, hd, rows, :] = jnp.where(lane < HEAD_DIM, head_low(pq), q_aug).astype(BF16)
            k_ref[0, hd, rows, :] = jnp.where(lane < HEAD_DIM, head_low(pk), k_aug).astype(BF16)
            v_ref[0, hd, rows, :] = jnp.where(lane < HEAD_DIM, head_low(pv), v_aug).astype(BF16)
    fcarry_ref[...] = carry


def _in_proj(x, mod, g, w_aug, fb):
    B, T, _ = x.shape
    tm = min(TM_IN, T)
    grp = jax.ShapeDtypeStruct((B, T, 256), F32)
    head = jax.ShapeDtypeStruct((B, N_HEADS, T, LANES), BF16)
    grp_spec = pl.BlockSpec((1, tm, 256), lambda b, t: (b, t, 0))
    head_spec = pl.BlockSpec((1, N_HEADS, tm, LANES), lambda b, t: (b, 0, t, 0))
    return pl.pallas_call(
        functools.partial(_in_proj_kernel, tm=tm),
        grid=(B, T // tm),
        in_specs=[pl.BlockSpec((1, tm, D), lambda b, t: (b, t, 0)),
                  pl.BlockSpec((1, N_MOD, D), lambda b, t: (b, 0, 0)),
                  pl.BlockSpec((1, D), lambda b, t: (0, 0)),
                  pl.BlockSpec((D, IN_COLS_PAD), lambda b, t: (0, 0)),
                  pl.BlockSpec((1, LANES), lambda b, t: (0, 0))],
        out_specs=[grp_spec, grp_spec, grp_spec, head_spec, head_spec, head_spec],
        out_shape=[grp, grp, grp, head, head, head],
        scratch_shapes=[pltpu.VMEM((1, LANES), F32)],
        compiler_params=_params(VMEM_LIMIT),
        name="in_proj",
    )(x, mod, g, w_aug, fb)


def _lru_kernel(lx_ref, lg_ref, cw_ref, cb_ref, wa_ref, ba_ref, wx_ref, bx_ref, lam_ref, gn_ref,
                y_ref, tail_ref, hcarry_ref, *, tc):
    @pl.when(pl.program_id(1) == 0)
    def _():
        tail_ref[...] = jnp.zeros_like(tail_ref)
        hcarry_ref[...] = jnp.zeros_like(hcarry_ref)

    x = lx_ref[0]
    xcat = jnp.concatenate([tail_ref[...], x], axis=0)
    cw = cw_ref[...]
    xr = cb_ref[...]
    for j in range(CONV_W):
        d = CONV_W - 1 - j
        xs = x if d == 0 else pltpu.roll(xcat, d, 0)[SUBLANES:SUBLANES + tc]
        xr = xr + xs * cw[j:j + 1]
    tail_ref[...] = x[tc - SUBLANES:tc]

    xb = xr.astype(BF16)
    r = jax.nn.sigmoid(jnp.dot(xb, wa_ref[...], preferred_element_type=F32) + ba_ref[...])
    i = jax.nn.sigmoid(jnp.dot(xb, wx_ref[...], preferred_element_type=F32) + bx_ref[...])
    nl = -lam_ref[...]
    softplus = jnp.maximum(nl, 0.0) + jnp.log1p(jnp.exp(-jnp.abs(nl)))
    log_a = -LRU_C * r * softplus
    a = jnp.exp(log_a)
    th = jnp.tanh(log_a)
    u = xr * i * jnp.sqrt(-2.0 * th / (1.0 - th))

    row = lax.broadcasted_iota(I32, (tc, LRU_W), 0)
    d = 1
    while d < tc:
        u = u + a * _shift_rows(u, d, row, 0.0)
        a = a * _shift_rows(a, d, row, 1.0)
        d *= 2
    h = u + a * hcarry_ref[...]
    hcarry_ref[...] = h[tc - 1:tc]

    y = h * jax.nn.gelu(lg_ref[0])
    y_ref[0] = _rms(y, gn_ref[...]).astype(BF16)


def _lru_branch(lx, lg, conv_w, conv_b, wa, ba, wx, bx, lam, gn):
    B, T, _ = lx.shape
    tc = min(TC_LRU, T)
    tile = pl.BlockSpec((1, tc, LRU_W), lambda b, t: (b, t, 0))
    vec = pl.BlockSpec((1, LRU_W), lambda b, t: (0, 0))
    mat = pl.BlockSpec((LRU_W, LRU_W), lambda b, t: (0, 0))
    return pl.pallas_call(
        functools.partial(_lru_kernel, tc=tc),
        grid=(B, T // tc),
        in_specs=[tile, tile, pl.BlockSpec((CONV_W, LRU_W), lambda b, t: (0, 0)), vec, mat, vec, mat, vec, vec, vec],
        out_specs=tile,
        out_shape=jax.ShapeDtypeStruct((B, T, LRU_W), BF16),
        scratch_shapes=[pltpu.VMEM((SUBLANES, LRU_W), F32), pltpu.VMEM((1, LRU_W), F32)],
        name="rg_lru",
    )(lx, lg, conv_w, conv_b, wa, ba, wx, bx, lam, gn)


def _s5_kernel(u_ref, bcat_ref, apr_ref, api_ref, p8r_ref, p8i_ref, bpr_ref, bpi_ref, ccat_ref, d_ref, gw_ref,
               gb_ref, gn_ref, y_ref, cr_ref, ci_ref, xsr_ref, xsi_ref, gr_ref, gi_ref, *, tc):
    @pl.when(pl.program_id(1) == 0)
    def _():
        cr_ref[...] = jnp.zeros_like(cr_ref)
        ci_ref[...] = jnp.zeros_like(ci_ref)

    def cmul_add(xr, xi, ar, ai, sr, si):
        return xr + (ar * sr - ai * si), xi + (ar * si + ai * sr)

    u = u_ref[0]
    bu = jnp.dot(u.astype(BF16), bcat_ref[...], preferred_element_type=F32)
    xr = bu[:, :S5_STATES]
    xi = bu[:, S5_STATES:]
    nb = tc // SUBLANES
    blocks = (nb, SUBLANES, S5_STATES)
    sub = lax.broadcasted_iota(I32, blocks, 1)
    xr = xr.reshape(blocks)
    xi = xi.reshape(blocks)
    for k in range(3):
        d = 1 << k
        ar = apr_ref[k:k + 1, :].reshape(1, 1, S5_STATES)
        ai = api_ref[k:k + 1, :].reshape(1, 1, S5_STATES)
        xr, xi = cmul_add(xr, xi, ar, ai, jnp.where(sub >= d, pltpu.roll(xr, d, 1), 0.0),
                          jnp.where(sub >= d, pltpu.roll(xi, d, 1), 0.0))
    xr = xr.reshape(tc, S5_STATES)
    xi = xi.reshape(tc, S5_STATES)
    lane_tiles = range(S5_STATES // LANES)

    def put(ref, v):
        for l in lane_tiles:
            ref[l] = v[:, l * LANES:(l + 1) * LANES]

    put(xsr_ref, xr)
    put(xsi_ref, xi)
    er =jnp.concatenate([xsr_ref[l, pl.ds(SUBLANES - 1, nb, stride=SUBLANES), :] for l in lane_tiles], axis=1)
    ei = jnp.concatenate([xsi_ref[l, pl.ds(SUBLANES - 1, nb, stride=SUBLANES), :] for l in lane_tiles], axis=1)
    blk = lax.broadcasted_iota(I32, (nb, S5_STATES), 0)
    k, d = 3, 1
    while d < nb:
        er, ei = cmul_add(er, ei, apr_ref[k:k + 1, :], api_ref[k:k + 1, :],
                          _shift_rows(er, d, blk, 0.0), _shift_rows(ei, d, blk, 0.0))
        k, d = k + 1, d * 2
    gr, gi = cmul_add(_shift_rows(er, 1, blk, 0.0), _shift_rows(ei, 1, blk, 0.0),
                      bpr_ref[...], bpi_ref[...], cr_ref[...], ci_ref[...])
    for j in range(SUBLANES):
        for l in lane_tiles:
            gr_ref[l, pl.ds(j, nb, stride=SUBLANES), :] = gr[:, l * LANES:(l + 1) * LANES]
            gi_ref[l, pl.ds(j, nb, stride=SUBLANES), :] = gi[:, l * LANES:(l + 1) * LANES]
    xr, xi = cmul_add(xr, xi, pltpu.repeat(p8r_ref[...], nb, 0), pltpu.repeat(p8i_ref[...], nb, 0),
                      jnp.concatenate([gr_ref[l] for l in lane_tiles], axis=1),
                      jnp.concatenate([gi_ref[l] for l in lane_tiles], axis=1))
    cr_ref[...] = xr[tc - 1:tc]
    ci_ref[...] = xi[tc - 1:tc]

    xcat = jnp.concatenate([xr, xi], axis=1).astype(BF16)
    y = jnp.dot(xcat, ccat_ref[...], preferred_element_type=F32) + d_ref[...] * u
    y = jax.nn.gelu(y)
    y = y * jax.nn.sigmoid(jnp.dot(y.astype(BF16), gw_ref[...], preferred_element_type=F32) + gb_ref[...])
    y_ref[0] = _rms(y, gn_ref[...]).astype(BF16)


def _s5_branch(u, bcat, apow_re, apow_im, p8_re, p8_im, bp_re, bp_im, ccat, d, glu_w, glu_b, gn, tc):
    B, T, _ = u.shape
    tile = pl.BlockSpec((1, tc, S5_W), lambda b, t: (b, t, 0))
    vec = pl.BlockSpec((1, S5_W), lambda b, t: (0, 0))

    def full(a):
        return pl.BlockSpec(a.shape, lambda b, t: (0, 0))

    return pl.pallas_call(
        functools.partial(_s5_kernel, tc=tc),
        grid=(B, T // tc),
        in_specs=[tile, full(bcat), full(apow_re), full(apow_im), full(p8_re), full(p8_im), full(bp_re),
                  full(bp_im), full(ccat), vec, full(glu_w), vec, vec],
        out_specs=tile,
        out_shape=jax.ShapeDtypeStruct((B, T, S5_W), BF16),
        scratch_shapes=[pltpu.VMEM((1, S5_STATES), F32), pltpu.VMEM((1, S5_STATES), F32)]
        + [pltpu.VMEM((S5_STATES // LANES, tc, LANES), F32)] * 4,
        compiler_params=_params(VMEM_LIMIT),
        name="s5",
    )(u, bcat, apow_re, apow_im, p8_re, p8_im, bp_re, bp_im, ccat, d, glu_w, glu_b, gn)


def _s5_tables(a_re, a_im, b_re, b_im, c_re, c_im, log_dt, tc):
    dt = jnp.exp(log_dt)[:, None]
    mag = jnp.exp(a_re * dt)
    abar_re = mag * jnp.cos(a_im * dt)
    abar_im = mag * jnp.sin(a_im * dt)
    den = a_re * a_re + a_im * a_im
    num_re = abar_re - 1.0
    k_re = (num_re * a_re + abar_im * a_im) / den
    k_im = (abar_im * a_re - num_re * a_im) / den
    bbar_re = k_re[..., None] * b_re - k_im[..., None] * b_im
    bbar_im = k_re[..., None] * b_im + k_im[..., None] * b_re
    eye = jnp.eye(S5_GROUPS, dtype=F32)
    bd_re = jnp.einsum('gpc,gh->gchp', bbar_re, eye).reshape(S5_W, S5_STATES)
    bd_im = jnp.einsum('gpc,gh->gchp', bbar_im, eye).reshape(S5_W, S5_STATES)
    bcat = jnp.concatenate([bd_re, bd_im], axis=1).astype(BF16)
    cd_re = jnp.einsum('gcp,gh->gphc', c_re, eye).reshape(S5_STATES, S5_W)
    cd_im = jnp.einsum('gcp,gh->gphc', c_im, eye).reshape(S5_STATES, S5_W)
    ccat = jnp.concatenate([cd_re, -cd_im], axis=0).astype(BF16)
    ar = abar_re.reshape(1, S5_STATES)
    ai = abar_im.reshape(1, S5_STATES)
    pows_r, pows_i = [ar], [ai]
    tab_r, tab_i = ar, ai
    n = 1
    while n < tc:
        sr, si = pows_r[-1], pows_i[-1]
        tab_r, tab_i = (jnp.concatenate([tab_r, tab_r * sr - tab_i * si], axis=0),
                        jnp.concatenate([tab_i, tab_r * si + tab_i * sr], axis=0))
        pows_r.append(sr * sr - si * si)
        pows_i.append(2.0 * sr * si)
        n *= 2
    levels = len(pows_r) - 1
    pad = (-levels) % SUBLANES
    apow_re = jnp.concatenate(pows_r[:levels] + [jnp.zeros((pad, S5_STATES), F32)], axis=0)
    apow_im = jnp.concatenate(pows_i[:levels] + [jnp.zeros((pad, S5_STATES), F32)], axis=0)
    nb = tc // SUBLANES
    bp_re = jnp.concatenate([jnp.ones((1, S5_STATES), F32), tab_r[SUBLANES - 1::SUBLANES][:nb - 1]], axis=0)
    bp_im = jnp.concatenate([jnp.zeros((1, S5_STATES), F32), tab_i[SUBLANES - 1::SUBLANES][:nb - 1]], axis=0)
    return bcat, apow_re, apow_im, tab_r[:SUBLANES], tab_i[:SUBLANES], bp_re, bp_im, ccat


def _attn_kernel(qt_ref, k_ref, vt_ref, o_ref, p_ref, *, tq):
    qi = pl.program_id(2)
    key = lax.broadcasted_iota(I32, (tq, tq), 0)
    qry = lax.broadcasted_iota(I32, (tq, tq), 1)
    heads = range(ATTN_HEADS_PER_STEP)
    qts =[qt_ref[0, hh] for hh in heads]

    def scores(hh, kj):
        k = k_ref[0, hh, pl.ds(pl.multiple_of(kj * tq, tq), tq), :]
        return jnp.dot(k, qts[hh], preferred_element_type=F32)

    def weighted_values(hh, kj, p):
        vt = vt_ref[0, hh, :, pl.ds(pl.multiple_of(kj * tq, tq), tq)]
        return jnp.dot(vt, p, preferred_element_type=F32)

    def softmax_step(s, m):
        m_new = jnp.maximum(m, jnp.max(s, axis=0, keepdims=True))
        return m_new, jnp.exp(m - m_new), jnp.exp(s - m_new).astype(BF16)

    p_ref[...] = jnp.zeros_like(p_ref)

    def body(kj, carry):
        ss = [scores(hh, kj) for hh in heads]
        pvs = [weighted_values(hh, jnp.maximum(kj - 1, 0), p_ref[hh]) for hh in heads]
        out = []
        for hh in heads:
            m, alpha, acc = carry[hh]
            acc = alpha * acc + pvs[hh]
            m, alpha, p = softmax_step(ss[hh], m)
            p_ref[hh] = p
            out.append((m, alpha, acc))
        return tuple(out)

    init = tuple((jnp.full((1, tq), NEG_BIG, F32), jnp.ones((1, tq), F32), jnp.zeros((LANES, tq), F32))
                 for _ in heads)
    carry = lax.fori_loop(0, qi, body, init)
    outs = []
    for hh in heads:
        m, alpha, acc = carry[hh]
        acc = alpha * acc + weighted_values(hh, jnp.maximum(qi - 1, 0), p_ref[hh])
        s = jnp.where(key <= qry, scores(hh, qi), NEG_BIG)
        m, alpha, p = softmax_step(s, m)
        acc = alpha * acc + weighted_values(hh, qi, p)
        out_t = acc / acc[HEAD_DIM:HEAD_DIM + 1, :]
        outs.append(out_t.T[:, :HEAD_DIM])
    o_ref[0] = jnp.concatenate(outs, axis=1)


def _attention(qa, ka, va):
    B, H, T, _ = qa.shape
    tq = min(TQ, T)
    hs = ATTN_HEADS_PER_STEP
    qt = jnp.swapaxes(qa, 2, 3)
    vt = jnp.swapaxes(va, 2, 3)
    return pl.pallas_call(
        functools.partial(_attn_kernel, tq=tq),
        grid=(B, H // hs, T // tq),
        in_specs=[pl.BlockSpec((1, hs, LANES, tq), lambda b, hp, qi: (b, hp, 0, qi)),
                  pl.BlockSpec((1, hs, T, LANES), lambda b, hp, qi: (b, hp, 0, 0)),
                  pl.BlockSpec((1, hs, LANES, T), lambda b, hp, qi: (b, hp, 0, 0))],
        out_specs=pl.BlockSpec((1, tq, hs * HEAD_DIM), lambda b, hp, qi: (b, qi, hp)),
        out_shape=jax.ShapeDtypeStruct((B, T, ATTN_W), F32),
        scratch_shapes=[pltpu.VMEM((hs, tq, tq), BF16)],
        compiler_params=_params(VMEM_LIMIT),
        name="fox_attention",
    )(qt, ka, vt)


def _out_proj_kernel(yl_ref, ys_ref, ya_ref, x_ref, mod_ref, gna_ref, wout_ref, pmg_ref, pfg_ref,
                     rwh_ref, rwl_ref, rb_ref,
                     x1_ref, h2_ref, idx_ref, gate_ref, rank_ref, cnt_ref, carry_ref, *, tm):
    @pl.when((pl.program_id(0) == 0) & (pl.program_id(1) == 0))
    def _():
        carry_ref[...] = jnp.zeros_like(carry_ref)

    mod = mod_ref[0]
    ts = tm // OUT_SUBTILES
    lane = lax.broadcasted_iota(I32, (ts, LANES), 1)
    lane_f = lane.astype(F32)
    r_i = lax.broadcasted_iota(I32, (ts, ts), 0)
    c_i = lax.broadcasted_iota(I32, (ts, ts), 1)
    below = jnp.where(c_i < r_i, 1.0, 0.0).astype(BF16)
    carry = carry_ref[...]
    for sub in range(OUT_SUBTILES):
        rows = slice(sub * ts, (sub + 1) * ts)
        ya = _rms(ya_ref[0, rows, :], gna_ref[...])
        ycat = jnp.concatenate([yl_ref[0, rows, :], ys_ref[0, rows, :], ya.astype(BF16)], axis=1)
        y = jnp.dot(ycat, wout_ref[...], preferred_element_type=F32)
        x1 = x_ref[0, rows, :] + mod[2:3] * _rms(y, pmg_ref[...])
        x1_ref[0, rows, :] = x1
        h2 = _rms(x1, pfg_ref[...]) * (1.0 + mod[4:5]) + mod[3:4]
        for c in range(ROW_CHUNKS):
            h2_ref[pl.ds(sub * ts * ROW_CHUNKS + c, ts, stride=ROW_CHUNKS), :] = h2[:, c * LANES:(c + 1) * LANES]

        hh = h2.astype(BF16)
        hl = (h2 - hh.astype(F32)).astype(BF16)
        rwh = rwh_ref[...]
        logits = (jnp.dot(hh, rwh, preferred_element_type=F32) + jnp.dot(hl, rwh, preferred_element_type=F32)
                  + jnp.dot(hh, rwl_ref[...], preferred_element_type=F32) + rb_ref[...])
        work = jnp.where(lane < N_EXPERTS, logits, NEG_BIG)
        vals, idxs = [], []
        for _ in range(TOP_K):
            mx = jnp.max(work, axis=1, keepdims=True)
            ik = jnp.min(jnp.where(work == mx, lane_f, float(LANES)), axis=1, keepdims=True)
            vals.append(mx)
            idxs.append(ik)
            work = jnp.where(lane_f == ik, 2.0 * NEG_BIG, work)
        es = [jnp.exp(v - vals[0]) for v in vals]
        den = es[0] + es[1] + es[2] + es[3]

        onehot = jnp.zeros((ts, LANES), F32)
        for ik in idxs:
            onehot = onehot + jnp.where(lane_f == ik, 1.0, 0.0)
        prior = jnp.dot(below, onehot.astype(BF16), preferred_element_type=F32) + carry
        carry = carry + jnp.sum(onehot, axis=0, keepdims=True)

        idx_full = jnp.zeros((ts, LANES), F32)
        gate_full = jnp.zeros((ts, LANES), F32)
        rank_full = jnp.zeros((ts, LANES), F32)
        for k in range(TOP_K):
            rk = jnp.sum(jnp.where(lane_f == idxs[k], prior, 0.0), axis=1, keepdims=True)
            idx_full = jnp.where(lane == k, idxs[k], idx_full)
            gate_full = jnp.where(lane == k, es[k] / den, gate_full)
            rank_full = jnp.where(lane == k, rk, rank_full)
        idx_ref[rows, :] = idx_full[:, :TOP_K].astype(I32)
        gate_ref[rows, :] = gate_full[:, :TOP_K]
        rank_ref[rows, :] = rank_full[:, :TOP_K].astype(I32)
    carry_ref[...] = carry
    cnt_ref[...] = carry.astype(I32)


def _out_proj(yl, ys, ya, x, mod, gna, wout, pmg, pfg, rwh, rwl, rb):
    B, T, _ = x.shape
    tm = min(TM_OUT, T)
    nt = T // tm
    N = B * T

    def vec(n):
        return pl.BlockSpec((1, n), lambda b, t: (0, 0))

    tok = lambda b, t: (b * nt + t, 0)
    return pl.pallas_call(
        functools.partial(_out_proj_kernel, tm=tm),
        grid=(B, nt),
        in_specs=[pl.BlockSpec((1, tm, LRU_W), lambda b, t: (b, t, 0)),
                  pl.BlockSpec((1, tm, S5_W), lambda b, t: (b, t, 0)),
                  pl.BlockSpec((1, tm, ATTN_W), lambda b, t: (b, t, 0)),
                  pl.BlockSpec((1, tm, D), lambda b, t: (b, t, 0)),
                  pl.BlockSpec((1, N_MOD, D), lambda b, t: (b, 0, 0)),
                  vec(ATTN_W),
                  pl.BlockSpec((D, D), lambda b, t: (0, 0)),
                  vec(D), vec(D),
                  pl.BlockSpec((D, LANES), lambda b, t: (0, 0)),
                  pl.BlockSpec((D, LANES), lambda b, t: (0, 0)),
                  vec(LANES)],
        out_specs=[pl.BlockSpec((1, tm, D), lambda b, t: (b, t, 0)),
                   pl.BlockSpec((tm * ROW_CHUNKS, LANES), tok),
                   pl.BlockSpec((tm, TOP_K), tok),
                   pl.BlockSpec((tm, TOP_K), tok),
                   pl.BlockSpec((tm, TOP_K), tok),
                   pl.BlockSpec((1, LANES), lambda b, t: (0, 0))],
        out_shape=[jax.ShapeDtypeStruct((B, T, D), F32),
                   jax.ShapeDtypeStruct((N * ROW_CHUNKS, LANES), F32),
                   jax.ShapeDtypeStruct((N, TOP_K), I32),
                   jax.ShapeDtypeStruct((N, TOP_K), F32),
                   jax.ShapeDtypeStruct((N, TOP_K), I32),
                   jax.ShapeDtypeStruct((1, LANES), I32)],
        scratch_shapes=[pltpu.VMEM((1, LANES), F32)],
        compiler_params=_params(VMEM_LIMIT),
        name="out_proj_router",
    )(yl, ys, ya, x, mod, gna, wout, pmg, pfg, rwh, rwl, rb)


def _fused_expert_kernel(be_ref, nused_ref, src_ref, dst_ref, h_ref, wgu_ref, bgu_ref, wdn_ref, bdn_ref, y_ref,
                         xbuf, ybuf, wgu_s, wdn_s, gsem, ssem, *, rows, n_blocks, n_assign):
    b = pl.program_id(0)
    nu = nused_ref[0]
    slot = lax.rem(b, 2)
    other = 1 - slot
    xslot = lax.rem(b, 3)
    xnext = lax.rem(b + 2, 3)
    src_off = lax.rem(jnp.minimum(b + 2, n_blocks - 1) * rows, MAP_BLOCK)
    dst_off = lax.rem(jnp.maximum(b - 1, 0) * rows, MAP_BLOCK)

    def gather(i, tok, buf_slot):
        return pltpu.make_async_copy(
            h_ref.at[pl.ds(pl.multiple_of(tok * ROW_CHUNKS, ROW_CHUNKS), ROW_CHUNKS), :],
            xbuf.at[buf_slot, pl.ds(i * ROW_CHUNKS, ROW_CHUNKS), :], gsem.at[buf_slot])

    def scatter(i, dst, buf_slot):
        return pltpu.make_async_copy(
            ybuf.at[buf_slot, pl.ds(i * ROW_CHUNKS, ROW_CHUNKS), :],
            y_ref.at[pl.ds(pl.multiple_of(dst * ROW_CHUNKS, ROW_CHUNKS), ROW_CHUNKS), :], ssem.at[buf_slot])

    @pl.when(b == 0)
    def _():
        ybuf[...] = jnp.zeros(ybuf.shape, F32)
        for i in range(rows):
            gather(i, src_ref[i], 0).start()
            gather(i, src_ref[rows + i], 1).start()
            scatter(i, n_assign + 2 * rows + i, 0).start()

    @pl.when(b <= nu)
    def _():
        for i in range(rows):
            gather(i, 0, xslot).wait()

    prev = be_ref[jnp.maximum(b - 1, 0)]

    @pl.when((b == 0) | (be_ref[b] != prev))
    def _():
        wgu_s[...] = wgu_ref[0, 0].astype(BF16)
        wdn_s[...] = wdn_ref[0, 0].astype(BF16)

    @pl.when(b < nu)
    def _():
        for i in range(rows):
            gather(i, src_ref[src_off + i], xnext).start(priority=i % 2)
        for i in range(rows):
            dst = jnp.where(b == 0, n_assign + 3 * rows + i, dst_ref[dst_off + i])
            scatter(i, dst, other).start(priority=i % 2)
        x = jnp.concatenate([xbuf[xslot, pl.ds(c, rows, stride=ROW_CHUNKS), :] for c in range(ROW_CHUNKS)], axis=1)
        gu = jnp.dot(x.astype(BF16), wgu_s[...], preferred_element_type=F32) + bgu_ref[0, 0]
        g = jnp.minimum(gu[:, :D_FF], SWIGLU_LIMIT)
        up = jnp.clip(gu[:, D_FF:], -SWIGLU_LIMIT, SWIGLU_LIMIT)
        act = (up + 1.0) * (g * jax.nn.sigmoid(SWIGLU_ALPHA * g))
        y = jnp.dot(act.astype(BF16), wdn_s[...], preferred_element_type=F32) + bdn_ref[0, 0]
        for i in range(rows):
            scatter(i, 0, slot).wait()
        for c in range(ROW_CHUNKS):
            ybuf[slot, pl.ds(c, rows, stride=ROW_CHUNKS), :] = y[:, c * LANES:(c + 1) * LANES]

    @pl.when(b == nu)
    def _():
        for i in range(rows):
            gather(i, 0, lax.rem(b + 1, 3)).wait()
        for i in range(rows):
            scatter(i, 0, slot).wait()
        for i in range(rows):
            scatter(i, dst_ref[dst_off + i], other).start()
        for i in range(rows):
            scatter(i, 0, other).wait()


def _fused_experts(layer, block_e, n_used, src_tok, dst_row, h_rows, w_gu, b_gu, w_dn, b_dn, rows, n_assign):
    n_blocks = block_e.shape[0]
    L, E = w_gu.shape[:2]
    wsel = lambda b, be, nu: (layer, be[b], 0, 0)
    grid_spec = pltpu.PrefetchScalarGridSpec(
        num_scalar_prefetch=2,
        grid=(n_blocks,),
        in_specs=[pl.BlockSpec((MAP_BLOCK,), lambda b, be, nu: (jnp.minimum(b + 2, n_blocks - 1) * rows // MAP_BLOCK,),
                               memory_space=pltpu.SMEM),
                  pl.BlockSpec((MAP_BLOCK,), lambda b, be, nu: (jnp.maximum(b - 1, 0) * rows // MAP_BLOCK,),
                               memory_space=pltpu.SMEM),
                  pl.BlockSpec(memory_space=pl.ANY),
                  pl.BlockSpec((1, 1, D, 2 * D_FF), wsel),
                  pl.BlockSpec((1, 1, 1, 2 * D_FF), wsel),
                  pl.BlockSpec((1, 1, D_FF, D), wsel),
                  pl.BlockSpec((1, 1, 1, D), wsel)],
        out_specs=pl.BlockSpec(memory_space=pl.ANY),
        scratch_shapes=[pltpu.VMEM((3, rows * ROW_CHUNKS, LANES), F32),
                        pltpu.VMEM((2, rows * ROW_CHUNKS, LANES), F32),
                        pltpu.VMEM((D, 2 * D_FF), BF16), pltpu.VMEM((D_FF, D), BF16),
                        pltpu.SemaphoreType.DMA((3,)), pltpu.SemaphoreType.DMA((2,))],
    )
    return pl.pallas_call(
        functools.partial(_fused_expert_kernel, rows=rows, n_blocks=n_blocks, n_assign=n_assign),
        grid_spec=grid_spec,
        out_shape=jax.ShapeDtypeStruct(((n_assign + 4 * rows) * ROW_CHUNKS, LANES), F32),
        compiler_params=_params(VMEM_LIMIT),
        name="moe_experts",
    )(block_e, n_used, src_tok, dst_row, h_rows, w_gu, b_gu.reshape(L, E, 1, 2 * D_FF), w_dn,
      b_dn.reshape(L, E, 1, D))


def _dense_combine_kernel(y0_ref, y1_ref, y2_ref, y3_ref, gate_ref, x1_ref, mod_ref, g_ref, o_ref, *, tn):
    gates = gate_ref[...]
    y = jnp.zeros((tn, D), F32)
    for k, yk_ref in enumerate((y0_ref, y1_ref, y2_ref, y3_ref)):
        yk = jnp.concatenate([yk_ref[pl.ds(c, tn, stride=ROW_CHUNKS), :] for c in range(ROW_CHUNKS)], axis=1)
        y = y + gates[:, k:k + 1] * yk
    mod = mod_ref[0]
    o_ref[0] = x1_ref[0] + mod[5:6] * _rms(y, g_ref[...])


def _dense_combine(y_rows, gates, x1, mod, g):
    B, T, _ = x1.shape
    tn = min(TN_COMB, T)
    nt = T // tn

    def slot_spec(k):
        return pl.BlockSpec((tn * ROW_CHUNKS, LANES), lambda b, t: (k * B * nt + b * nt + t, 0))

    return pl.pallas_call(
        functools.partial(_dense_combine_kernel, tn=tn),
        grid=(B, nt),
        in_specs=[slot_spec(0), slot_spec(1), slot_spec(2), slot_spec(3),
                  pl.BlockSpec((tn, TOP_K), lambda b, t: (b * nt + t, 0)),
                  pl.BlockSpec((1, tn, D), lambda b, t: (b, t, 0)),
                  pl.BlockSpec((1, N_MOD, D), lambda b, t: (b, 0, 0)),
                  pl.BlockSpec((1, D), lambda b, t: (0, 0))],
        out_specs=pl.BlockSpec((1, tn, D), lambda b, t: (b, t, 0)),
        out_shape=jax.ShapeDtypeStruct((B, T, D), F32),
        name="moe_combine",
    )(y_rows, y_rows, y_rows, y_rows, gates, x1, mod, g)


def _block_diag(w):
    nb = w.shape[0]
    return jnp.einsum('hij,hg->higj', w, jnp.eye(nb, dtype=w.dtype)).reshape(LRU_W, LRU_W)


def kernel(x, c, ada_w, ada_b, pre_mix_g, w_in, conv_w, conv_b, lru_wa, lru_ba, lru_wx, lru_bx, lru_lambda, s5_a_re, s5_a_im, s5_b_re, s5_b_im, s5_c_re, s5_c_im, s5_d, s5_log_dt, s5_glu_w, s5_glu_b, fox_fb, gn_lru, gn_s5, gn_attn, w_out, post_mix_g, pre_ffn_g, router_w, router_b, w_gu, b_gu, w_dn, b_dn, post_ffn_g):
    B, T, _ = x.shape
    N = B * T
    L = ada_w.shape[0]
    tc_s5 = min(TC_S5, T)
    n_blocks = (N * TOP_K) // EXP_ROWS + N_EXPERTS
    n_rows = n_blocks * EXP_ROWS
    mod_all = _modulation(c, ada_w, ada_b).reshape(L, B, N_MOD, D)

    for l in range(L):
        mod = mod_all[l]
        wl = w_in[l]
        w_aug = jnp.pad(wl, ((0, 0), (0, LANES - N_HEADS))).astype(BF16)
        fb = jnp.pad(fox_fb[l], (0, LANES - N_HEADS)).reshape(1, LANES)
        lx, lg, su, qa, ka, va = _in_proj(x, mod, pre_mix_g[l].reshape(1, D), w_aug, fb)

        yl = _lru_branch(lx, lg, conv_w[l], conv_b[l].reshape(1, LRU_W),
                         _block_diag(lru_wa[l]).astype(BF16), lru_ba[l].reshape(1, LRU_W),
                         _block_diag(lru_wx[l]).astype(BF16), lru_bx[l].reshape(1, LRU_W),
                         lru_lambda[l].reshape(1, LRU_W), gn_lru[l].reshape(1, LRU_W))

        tabs = _s5_tables(s5_a_re[l], s5_a_im[l], s5_b_re[l], s5_b_im[l], s5_c_re[l], s5_c_im[l], s5_log_dt[l], tc_s5)
        ys = _s5_branch(su, *tabs, s5_d[l].reshape(1, S5_W), s5_glu_w[l].astype(BF16),
                        s5_glu_b[l].reshape(1, S5_W), gn_s5[l].reshape(1, S5_W), tc_s5)

        ya = _attention(qa, ka, va)

        rw = jnp.pad(router_w[l], ((0, 0), (0, LANES - N_EXPERTS)))
        rwh = rw.astype(BF16)
        rwl = (rw - rwh.astype(F32)).astype(BF16)
        rb = jnp.pad(router_b[l], (0, LANES - N_EXPERTS)).reshape(1, LANES)
        x1, h_rows, idx, gates, rank, cnt = _out_proj(
            yl, ys, ya, x, mod, gn_attn[l].reshape(1, ATTN_W), w_out[l].astype(BF16),
            post_mix_g[l].reshape(1, D), pre_ffn_g[l].reshape(1, D), rwh, rwl, rb)

        counts = cnt[0, :N_EXPERTS]
        blocks_e = (counts + EXP_ROWS - 1) // EXP_ROWS
        padded = blocks_e * EXP_ROWS
        block_ends = jnp.cumsum(blocks_e)
        starts = (block_ends - blocks_e) * EXP_ROWS
        n_used = block_ends[-1:]
        bid = jnp.minimum(jnp.arange(n_blocks, dtype=I32), n_used[0] - 1)
        block_e = jnp.minimum(jnp.sum((bid[:, None] >= block_ends[None, :]).astype(I32), axis=1), N_EXPERTS - 1)

        n_assign = N * TOP_K
        dest = (jnp.take(starts.astype(I32), idx) + rank).reshape(n_assign)
        row_assign = jnp.full((n_rows,), n_assign, I32).at[dest].set(
            jnp.arange(n_assign, dtype=I32), unique_indices=True)
        is_real = row_assign < n_assign
        src_tok = jnp.where(is_real, row_assign // TOP_K, 0)
        spare = n_assign + jnp.arange(n_rows, dtype=I32) % (2 * EXP_ROWS)
        dst_row = jnp.where(is_real, (row_assign % TOP_K) * N + row_assign // TOP_K, spare)

        y_rows = _fused_experts(l, block_e, n_used.astype(I32), src_tok, dst_row, h_rows,
                                w_gu, b_gu, w_dn, b_dn, EXP_ROWS, n_assign)
        x = _dense_combine(y_rows, gates, x1, mod, post_ffn_g[l].reshape(1, D))
    return x
```

```python
import functools

import jax
import jax.numpy as jnp
from jax import lax
from jax.experimental import pallas as pl
from jax.experimental.pallas import tpu as pltpu
from jax.experimental.pallas import tpu_sc as plsc

F32 = jnp.float32
BF16 = jnp.bfloat16
I32 = jnp.int32

D = 1024
LRU_W = 256
LRU_BLOCK_W = 64
CONV_W = 4
LRU_C = 8.0
S5_W = 256
S5_GROUP_W = 16
S5_GROUPS = 16
S5_STATE = 64
S5_STATES = S5_GROUPS * S5_STATE
HEAD_DIM = 64
ATTN_W = 512
N_HEADS = 8
O_LRU_G = 256
O_S5 = 512
O_Q = 768
O_K = O_Q + ATTN_W
O_V = O_K + ATTN_W
O_F = O_V + ATTN_W
N_EXPERTS = 32
TOP_K = 4
D_FF = 1024
SWIGLU_LIMIT = 7.0
SWIGLU_ALPHA = 1.702
N_MOD = 6
RMS_EPS = 1e-6

LANES = 128
SUBLANES = 8
VMEM_LIMIT = 56 * 1024 * 1024
ROW_CHUNKS = D // LANES

C_Q = O_Q
C_K = O_K
C_V = O_V
C_F = O_F
IN_COLS_PAD = C_F + LANES
L_A0, L_A1, L_A2, L_B0, L_B1, L_B2 = 64, 65, 66, 67, 68, 69

TM_IN = 512
SUBTILES = 2
OUT_SUBTILES = 1
TC_LRU = 256
TC_S5 = 256
TQ = 512
ATTN_HEADS_PER_STEP = 4
TM_OUT = 512
TN_COMB = 256
EXP_ROWS = 512
MAP_BLOCK = 1024
SC_WINDOW = 128
SC_ROW_WORDS = 128
NEG_BIG = -1e30


def _rms(x, g):
    return x * lax.rsqrt(jnp.mean(x * x, axis=-1, keepdims=True) + RMS_EPS) * g


def _shift_rows(v, d, row, fill):
    return jnp.where(row >= d, pltpu.roll(v, d, 0), fill)


def _params(vmem=None):
    return pltpu.CompilerParams(vmem_limit_bytes=vmem) if vmem else None


def _mod_kernel(c_ref, w_ref, b_ref, o_ref):
    c = c_ref[...]
    s = c * jax.nn.sigmoid(c)
    o_ref[0] = jnp.dot(s.astype(BF16), w_ref[0].astype(BF16), preferred_element_type=F32) + b_ref[0]


def _modulation(c, ada_w, ada_b):
    L, _, W = ada_w.shape
    B = c.shape[0]
    return pl.pallas_call(
        _mod_kernel,
        grid=(L, W // D),
        in_specs=[pl.BlockSpec((B, D), lambda l, j: (0, 0)),
                  pl.BlockSpec((1, D, D), lambda l, j: (l, 0, j)),
                  pl.BlockSpec((1, 1, D), lambda l, j: (l, 0, j))],
        out_specs=pl.BlockSpec((1, B, D), lambda l, j: (l, 0, j)),
        out_shape=jax.ShapeDtypeStruct((L, B, W), F32),
        name="adaln_mod",
    )(c, ada_w, ada_b.reshape(L, 1, W))


def _in_proj_kernel(x_ref, mod_ref, g_ref, w_ref, fb_ref,
                    lx_ref, lg_ref, su_ref, q_ref, k_ref, v_ref, fcarry_ref, *, tm):
    @pl.when(pl.program_id(1) == 0)
    def _():
        fcarry_ref[...] = jnp.zeros_like(fcarry_ref)

    mod = mod_ref[0]
    ts = tm // SUBTILES
    row = lax.broadcasted_iota(I32, (ts, LANES), 0)
    lane = lax.broadcasted_iota(I32, (ts, LANES), 1)
    carry = fcarry_ref[...]
    for sub in range(SUBTILES):
        rows = slice(sub * ts, (sub + 1) * ts)
        h = _rms(x_ref[0, rows, :], g_ref[...]) * (1.0 + mod[1:2]) + mod[0:1]
        p = jnp.dot(h.astype(BF16), w_ref[...], preferred_element_type=F32)
        lx_ref[0, rows, :] = p[:, 0:O_LRU_G]
        lg_ref[0, rows, :] = p[:, O_LRU_G:O_S5]
        su_ref[0, rows, :] = p[:, O_S5:O_Q]

        z = p[:, C_F:C_F + LANES] + fb_ref[...]
        logf = jnp.minimum(z, 0.0) - jnp.log1p(jnp.exp(-jnp.abs(z)))
        d = 1
        while d < ts:
            logf = logf + _shift_rows(logf, d, row, 0.0)
            d *= 2
        fsum = logf + carry
        carry = fsum[ts - 1:ts, :]

        pq = p[:, C_Q:C_K] * (HEAD_DIM ** -0.5)
        pk = p[:, C_K:C_V]
        pv = p[:, C_V:C_F]
        for hd in range(N_HEADS):
            pair = slice((hd // 2) * LANES, (hd // 2 + 1) * LANES)

            def head_low(a):
                s = a[:, pair]
                return s if hd % 2 == 0 else pltpu.roll(s, HEAD_DIM, 1)

            f = jnp.broadcast_to(fsum[:, hd:hd + 1], (ts, LANES))
            hi = f.astype(BF16).astype(F32)
            r1 = f - hi
            mid = r1.astype(BF16).astype(F32)
            lo = r1 - mid
            q_aug = jnp.where(lane == L_A0, hi, jnp.where(lane == L_A1, mid, jnp.where(lane == L_A2, lo,
                              jnp.where((lane >= L_B0) & (lane <= L_B2), 1.0, 0.0))))
            k_aug = jnp.where(lane == L_B0, -hi, jnp.where(lane == L_B1, -mid, jnp.where(lane == L_B2, -lo,
                              jnp.where((lane >= L_A0) & (lane <= L_A2), 1.0, 0.0))))
            v_aug = jnp.where(lane == HEAD_DIM, 1.0, 0.0)
            q_ref[0, hd, rows, :] = jnp.where(lane < HEAD_DIM, head_low(pq), q_aug).astype(BF16)
            k_ref[0, hd, rows, :] = jnp.where(lane < HEAD_DIM, head_low(pk), k_aug).astype(BF16)
            v_ref[0, hd, rows, :] = jnp.where(lane < HEAD_DIM, head_low(pv), v_aug).astype(BF16)
    fcarry_ref[...] = carry


def _in_proj(x, mod, g, w_aug, fb):
    B, T, _ = x.shape
    tm = min(TM_IN, T)
    grp = jax.ShapeDtypeStruct((B, T, 256), F32)
    head = jax.ShapeDtypeStruct((B, N_HEADS, T, LANES), BF16)
    grp_spec = pl.BlockSpec((1, tm, 256), lambda b, t: (b, t, 0))
    head_spec = pl.BlockSpec((1, N_HEADS, tm, LANES), lambda b, t: (b, 0, t, 0))
    return pl.pallas_call(
        functools.partial(_in_proj_kernel, tm=tm),
        grid=(B, T // tm),
        in_specs=[pl.BlockSpec((1, tm, D), lambda b, t: (b, t, 0)),
                  pl.BlockSpec((1, N_MOD, D), lambda b, t: (b, 0, 0)),
                  pl.BlockSpec((1, D), lambda b, t: (0, 0)),
                  pl.BlockSpec((D, IN_COLS_PAD), lambda b, t: (0, 0)),
                  pl.BlockSpec((1, LANES), lambda b, t: (0, 0))],
        out_specs=[grp_spec, grp_spec, grp_spec, head_spec, head_spec, head_spec],
        out_shape=[grp, grp, grp, head, head, head],
        scratch_shapes=[pltpu.VMEM((1, LANES), F32)],
        compiler_params=_params(VMEM_LIMIT),
        name="in_proj",
    )(x, mod, g, w_aug, fb)


def _lru_kernel(lx_ref, lg_ref, cw_ref, cb_ref, wa_ref, ba_ref, wx_ref, bx_ref, lam_ref, gn_ref,
                y_ref, tail_ref, hcarry_ref, *, tc):
    @pl.when(pl.program_id(1) == 0)
    def _():
        tail_ref[...] = jnp.zeros_like(tail_ref)
        hcarry_ref[...] = jnp.zeros_like(hcarry_ref)

    x = lx_ref[0]
    xcat = jnp.concatenate([tail_ref[...], x], axis=0)
    cw = cw_ref[...]
    xr = cb_ref[...]
    for j in range(CONV_W):
        d = CONV_W - 1 - j
        xs = x if d == 0 else pltpu.roll(xcat, d, 0)[SUBLANES:SUBLANES + tc]
        xr = xr + xs * cw[j:j + 1]
    tail_ref[...] = x[tc - SUBLANES:tc]

    xb = xr.astype(BF16)
    r = jax.nn.sigmoid(jnp.dot(xb, wa_ref[...], preferred_element_type=F32) + ba_ref[...])
    i = jax.nn.sigmoid(jnp.dot(xb, wx_ref[...], preferred_element_type=F32) + bx_ref[...])
    nl = -lam_ref[...]
    softplus = jnp.maximum(nl, 0.0) + jnp.log1p(jnp.exp(-jnp.abs(nl)))
    log_a = -LRU_C * r * softplus
    a = jnp.exp(log_a)
    th = jnp.tanh(log_a)
    u = xr * i * jnp.sqrt(-2.0 * th / (1.0 - th))

    row = lax.broadcasted_iota(I32, (tc, LRU_W), 0)
    d = 1
    while d < tc:
        u = u + a * _shift_rows(u, d, row, 0.0)
        a = a * _shift_rows(a, d, row, 1.0)
        d *= 2
    h = u + a * hcarry_ref[...]
    hcarry_ref[...] = h[tc - 1:tc]

    y = h * jax.nn.gelu(lg_ref[0])
    y_ref[0] = _rms(y, gn_ref[...]).astype(BF16)


def _lru_branch(lx, lg, conv_w, conv_b, wa, ba, wx, bx, lam, gn):
    B, T, _ = lx.shape
    tc = min(TC_LRU, T)
    tile = pl.BlockSpec((1, tc, LRU_W), lambda b, t: (b, t, 0))
    vec = pl.BlockSpec((1, LRU_W), lambda b, t: (0, 0))
    mat = pl.BlockSpec((LRU_W, LRU_W), lambda b, t: (0, 0))
    return pl.pallas_call(
        functools.partial(_lru_kernel, tc=tc),
        grid=(B, T // tc),
        in_specs=[tile, tile, pl.BlockSpec((CONV_W, LRU_W), lambda b, t: (0, 0)), vec, mat, vec, mat, vec, vec, vec],
        out_specs=tile,
        out_shape=jax.ShapeDtypeStruct((B, T, LRU_W), BF16),
        scratch_shapes=[pltpu.VMEM((SUBLANES, LRU_W), F32), pltpu.VMEM((1, LRU_W), F32)],
        name="rg_lru",
    )(lx, lg, conv_w, conv_b, wa, ba, wx, bx, lam, gn)


def _s5_kernel(u_ref, bcat_ref, apr_ref, api_ref, p8r_ref, p8i_ref, bpr_ref, bpi_ref, ccat_ref, d_ref, gw_ref,
               gb_ref, gn_ref, y_ref, cr_ref, ci_ref, xsr_ref, xsi_ref, gr_ref, gi_ref, *, tc):
    @pl.when(pl.program_id(1) == 0)
    def _():
        cr_ref[...] = jnp.zeros_like(cr_ref)
        ci_ref[...] = jnp.zeros_like(ci_ref)

    def cmul_add(xr, xi, ar, ai, sr, si):
        return xr + (ar * sr - ai * si), xi + (ar * si + ai * sr)

    u = u_ref[0]
    bu = jnp.dot(u.astype(BF16), bcat_ref[...], preferred_element_type=F32)
    xr = bu[:, :S5_STATES]
    xi = bu[:, S5_STATES:]
    nb = tc // SUBLANES
    blocks = (nb, SUBLANES, S5_STATES)
    sub = lax.broadcasted_iota(I32, blocks, 1)
    xr = xr.reshape(blocks)
    xi = xi.reshape(blocks)
    for k in range(3):
        d = 1 << k
        ar = apr_ref[k:k + 1, :].reshape(1, 1, S5_STATES)
        ai = api_ref[k:k + 1, :].reshape(1, 1, S5_STATES)
        xr, xi = cmul_add(xr, xi, ar, ai, jnp.where(sub >= d, pltpu.roll(xr, d, 1), 0.0),
                          jnp.where(sub >= d, pltpu.roll(xi, d, 1), 0.0))
    xr = xr.reshape(tc, S5_STATES)
    xi = xi.reshape(tc, S5_STATES)
    lane_tiles = range(S5_STATES // LANES)

    def put(ref, v):
        for l in lane_tiles:
            ref[l] = v[:, l * LANES:(l + 1) * LANES]

    put(xsr_ref, xr)
    put(xsi_ref, xi)
    er =jnp.concatenate([xsr_ref[l, pl.ds(SUBLANES - 1, nb, stride=SUBLANES), :] for l in lane_tiles], axis=1)
    ei = jnp.concatenate([xsi_ref[l, pl.ds(SUBLANES - 1, nb, stride=SUBLANES), :] for l in lane_tiles], axis=1)
    blk = lax.broadcasted_iota(I32, (nb, S5_STATES), 0)
    k, d = 3, 1
    while d < nb:
        er, ei = cmul_add(er, ei, apr_ref[k:k + 1, :], api_ref[k:k + 1, :],
                          _shift_rows(er, d, blk, 0.0), _shift_rows(ei, d, blk, 0.0))
        k, d = k + 1, d * 2
    gr, gi = cmul_add(_shift_rows(er, 1, blk, 0.0), _shift_rows(ei, 1, blk, 0.0),
                      bpr_ref[...], bpi_ref[...], cr_ref[...], ci_ref[...])
    for j in range(SUBLANES):
        for l in lane_tiles:
            gr_ref[l, pl.ds(j, nb, stride=SUBLANES), :] = gr[:, l * LANES:(l + 1) * LANES]
            gi_ref[l, pl.ds(j, nb, stride=SUBLANES), :] = gi[:, l * LANES:(l + 1) * LANES]
    xr, xi = cmul_add(xr, xi, jnp.tile(p8r_ref[...], (nb, 1)), jnp.tile(p8i_ref[...], (nb, 1)),
                      jnp.concatenate([gr_ref[l] for l in lane_tiles], axis=1),
                      jnp.concatenate([gi_ref[l] for l in lane_tiles], axis=1))
    cr_ref[...] = xr[tc - 1:tc]
    ci_ref[...] = xi[tc - 1:tc]

    xcat = jnp.concatenate([xr, xi], axis=1).astype(BF16)
    y = jnp.dot(xcat, ccat_ref[...], preferred_element_type=F32) + d_ref[...] * u
    y = jax.nn.gelu(y)
    y = y * jax.nn.sigmoid(jnp.dot(y.astype(BF16), gw_ref[...], preferred_element_type=F32) + gb_ref[...])
    y_ref[0] = _rms(y, gn_ref[...]).astype(BF16)


def _s5_branch(u, bcat, apow_re, apow_im, p8_re, p8_im, bp_re, bp_im, ccat, d, glu_w, glu_b, gn, tc):
    B, T, _ = u.shape
    tile = pl.BlockSpec((1, tc, S5_W), lambda b, t: (b, t, 0))
    vec = pl.BlockSpec((1, S5_W), lambda b, t: (0, 0))

    def full(a):
        return pl.BlockSpec(a.shape, lambda b, t: (0, 0))

    return pl.pallas_call(
        functools.partial(_s5_kernel, tc=tc),
        grid=(B, T // tc),
        in_specs=[tile, full(bcat), full(apow_re), full(apow_im), full(p8_re), full(p8_im), full(bp_re),
                  full(bp_im), full(ccat), vec, full(glu_w), vec, vec],
        out_specs=tile,
        out_shape=jax.ShapeDtypeStruct((B, T, S5_W), BF16),
        scratch_shapes=[pltpu.VMEM((1, S5_STATES), F32), pltpu.VMEM((1, S5_STATES), F32)]
        + [pltpu.VMEM((S5_STATES // LANES, tc, LANES), F32)] * 4,
        compiler_params=_params(VMEM_LIMIT),
        name="s5",
    )(u, bcat, apow_re, apow_im, p8_re, p8_im, bp_re, bp_im, ccat, d, glu_w, glu_b, gn)


def _s5_tables(a_re, a_im, b_re, b_im, c_re, c_im, log_dt, tc):
    dt = jnp.exp(log_dt)[:, None]
    mag = jnp.exp(a_re * dt)
    abar_re = mag * jnp.cos(a_im * dt)
    abar_im = mag * jnp.sin(a_im * dt)
    den = a_re * a_re + a_im * a_im
    num_re = abar_re - 1.0
    k_re = (num_re * a_re + abar_im * a_im) / den
    k_im = (abar_im * a_re - num_re * a_im) / den
    bbar_re = k_re[..., None] * b_re - k_im[..., None] * b_im
    bbar_im = k_re[..., None] * b_im + k_im[..., None] * b_re
    eye = jnp.eye(S5_GROUPS, dtype=F32)
    bd_re = jnp.einsum('gpc,gh->gchp', bbar_re, eye).reshape(S5_W, S5_STATES)
    bd_im = jnp.einsum('gpc,gh->gchp', bbar_im, eye).reshape(S5_W, S5_STATES)
    bcat = jnp.concatenate([bd_re, bd_im], axis=1).astype(BF16)
    cd_re = jnp.einsum('gcp,gh->gphc', c_re, eye).reshape(S5_STATES, S5_W)
    cd_im = jnp.einsum('gcp,gh->gphc', c_im, eye).reshape(S5_STATES, S5_W)
    ccat = jnp.concatenate([cd_re, -cd_im], axis=0).astype(BF16)
    ar = abar_re.reshape(1, S5_STATES)
    ai = abar_im.reshape(1, S5_STATES)
    pows_r, pows_i = [ar], [ai]
    tab_r, tab_i = ar, ai
    n = 1
    while n < tc:
        sr, si = pows_r[-1], pows_i[-1]
        tab_r, tab_i = (jnp.concatenate([tab_r, tab_r * sr - tab_i * si], axis=0),
                        jnp.concatenate([tab_i, tab_r * si + tab_i * sr], axis=0))
        pows_r.append(sr * sr - si * si)
        pows_i.append(2.0 * sr * si)
        n *= 2
    levels = len(pows_r) - 1
    pad = (-levels) % SUBLANES
    apow_re = jnp.concatenate(pows_r[:levels] + [jnp.zeros((pad, S5_STATES), F32)], axis=0)
    apow_im = jnp.concatenate(pows_i[:levels] + [jnp.zeros((pad, S5_STATES), F32)], axis=0)
    nb = tc // SUBLANES
    bp_re = jnp.concatenate([jnp.ones((1, S5_STATES), F32), tab_r[SUBLANES - 1::SUBLANES][:nb - 1]], axis=0)
    bp_im = jnp.concatenate([jnp.zeros((1, S5_STATES), F32), tab_i[SUBLANES - 1::SUBLANES][:nb - 1]], axis=0)
    return bcat, apow_re, apow_im, tab_r[:SUBLANES], tab_i[:SUBLANES], bp_re, bp_im, ccat


def _attn_kernel(qt_ref, k_ref, vt_ref, o_ref, p_ref, *, tq):
    qi = pl.program_id(2)
    key = lax.broadcasted_iota(I32, (tq, tq), 0)
    qry = lax.broadcasted_iota(I32, (tq, tq), 1)
    heads = range(ATTN_HEADS_PER_STEP)
    qts =[qt_ref[0, hh] for hh in heads]

    def scores(hh, kj):
        k = k_ref[0, hh, pl.ds(pl.multiple_of(kj * tq, tq), tq), :]
        return jnp.dot(k, qts[hh], preferred_element_type=F32)

    def weighted_values(hh, kj, p):
        vt = vt_ref[0, hh, :, pl.ds(pl.multiple_of(kj * tq, tq), tq)]
        return jnp.dot(vt, p, preferred_element_type=F32)

    def softmax_step(s, m):
        m_new = jnp.maximum(m, jnp.max(s, axis=0, keepdims=True))
        return m_new, jnp.exp(m - m_new), jnp.exp(s - m_new).astype(BF16)

    p_ref[...] = jnp.zeros_like(p_ref)

    def body(kj, carry):
        ss = [scores(hh, kj) for hh in heads]
        pvs = [weighted_values(hh, jnp.maximum(kj - 1, 0), p_ref[hh]) for hh in heads]
        out = []
        for hh in heads:
            m, alpha, acc = carry[hh]
            acc = alpha * acc + pvs[hh]
            m, alpha, p = softmax_step(ss[hh], m)
            p_ref[hh] = p
            out.append((m, alpha, acc))
        return tuple(out)

    init = tuple((jnp.full((1, tq), NEG_BIG, F32), jnp.ones((1, tq), F32), jnp.zeros((LANES, tq), F32))
                 for _ in heads)
    carry = lax.fori_loop(0, qi, body, init)
    outs = []
    for hh in heads:
        m, alpha, acc = carry[hh]
        acc = alpha * acc + weighted_values(hh, jnp.maximum(qi - 1, 0), p_ref[hh])
        s = jnp.where(key <= qry, scores(hh, qi), NEG_BIG)
        m, alpha, p = softmax_step(s, m)
        acc = alpha * acc + weighted_values(hh, qi, p)
        out_t = acc / acc[HEAD_DIM:HEAD_DIM + 1, :]
        outs.append(out_t.T[:, :HEAD_DIM])
    o_ref[0] = jnp.concatenate(outs, axis=1)


def _attention(qa, ka, va):
    B, H, T, _ = qa.shape
    tq = min(TQ, T)
    hs = ATTN_HEADS_PER_STEP
    qt = jnp.swapaxes(qa, 2, 3)
    vt = jnp.swapaxes(va, 2, 3)
    return pl.pallas_call(
        functools.partial(_attn_kernel, tq=tq),
        grid=(B, H // hs, T // tq),
        in_specs=[pl.BlockSpec((1, hs, LANES, tq), lambda b, hp, qi: (b, hp, 0, qi)),
                  pl.BlockSpec((1, hs, T, LANES), lambda b, hp, qi: (b, hp, 0, 0)),
                  pl.BlockSpec((1, hs, LANES, T), lambda b, hp, qi: (b, hp, 0, 0))],
        out_specs=pl.BlockSpec((1, tq, hs * HEAD_DIM), lambda b, hp, qi: (b, qi, hp)),
        out_shape=jax.ShapeDtypeStruct((B, T, ATTN_W), F32),
        scratch_shapes=[pltpu.VMEM((hs, tq, tq), BF16)],
        compiler_params=_params(VMEM_LIMIT),
        name="fox_attention",
    )(qt, ka, vt)


def _out_proj_kernel(yl_ref, ys_ref, ya_ref, x_ref, mod_ref, gna_ref, wout_ref, pmg_ref, pfg_ref,
                     rwh_ref, rwl_ref, rb_ref,
                     x1_ref, h2_ref, idx_ref, gate_ref, rank_ref, cnt_ref, carry_ref, *, tm):
    @pl.when((pl.program_id(0) == 0) & (pl.program_id(1) == 0))
    def _():
        carry_ref[...] = jnp.zeros_like(carry_ref)

    mod = mod_ref[0]
    ts = tm // OUT_SUBTILES
    lane = lax.broadcasted_iota(I32, (ts, LANES), 1)
    lane_f = lane.astype(F32)
    r_i = lax.broadcasted_iota(I32, (ts, ts), 0)
    c_i = lax.broadcasted_iota(I32, (ts, ts), 1)
    below = jnp.where(c_i < r_i, 1.0, 0.0).astype(BF16)
    carry = carry_ref[...]
    for sub in range(OUT_SUBTILES):
        rows = slice(sub * ts, (sub + 1) * ts)
        ya = _rms(ya_ref[0, rows, :], gna_ref[...])
        ycat = jnp.concatenate([yl_ref[0, rows, :], ys_ref[0, rows, :], ya.astype(BF16)], axis=1)
        y = jnp.dot(ycat, wout_ref[...], preferred_element_type=F32)
        x1 = x_ref[0, rows, :] + mod[2:3] * _rms(y, pmg_ref[...])
        x1_ref[0, rows, :] = x1
        h2 = _rms(x1, pfg_ref[...]) * (1.0 + mod[4:5]) + mod[3:4]
        for c in range(ROW_CHUNKS):
            h2_ref[pl.ds(sub * ts * ROW_CHUNKS + c, ts, stride=ROW_CHUNKS), :] = h2[:, c * LANES:(c + 1) * LANES]

        hh = h2.astype(BF16)
        hl = (h2 - hh.astype(F32)).astype(BF16)
        rwh = rwh_ref[...]
        logits = (jnp.dot(hh, rwh, preferred_element_type=F32) + jnp.dot(hl, rwh, preferred_element_type=F32)
                  + jnp.dot(hh, rwl_ref[...], preferred_element_type=F32) + rb_ref[...])
        work = jnp.where(lane < N_EXPERTS, logits, NEG_BIG)
        vals, idxs = [], []
        for _ in range(TOP_K):
            mx = jnp.max(work, axis=1, keepdims=True)
            ik = jnp.min(jnp.where(work == mx, lane_f, float(LANES)), axis=1, keepdims=True)
            vals.append(mx)
            idxs.append(ik)
            work = jnp.where(lane_f == ik, 2.0 * NEG_BIG, work)
        es = [jnp.exp(v - vals[0]) for v in vals]
        den = es[0] + es[1] + es[2] + es[3]

        onehot = jnp.zeros((ts, LANES), F32)
        for ik in idxs:
            onehot = onehot + jnp.where(lane_f == ik, 1.0, 0.0)
        prior = jnp.dot(below, onehot.astype(BF16), preferred_element_type=F32) + carry
        carry = carry + jnp.sum(onehot, axis=0, keepdims=True)

        idx_full = jnp.zeros((ts, LANES), F32)
        gate_full = jnp.zeros((ts, LANES), F32)
        rank_full = jnp.zeros((ts, LANES), F32)
        for k in range(TOP_K):
            rk = jnp.sum(jnp.where(lane_f == idxs[k], prior, 0.0), axis=1, keepdims=True)
            idx_full = jnp.where(lane == k, idxs[k], idx_full)
            gate_full = jnp.where(lane == k, es[k] / den, gate_full)
            rank_full = jnp.where(lane == k, rk, rank_full)
        idx_ref[:, rows] = idx_full.T[:TOP_K, :].astype(I32)
        gate_ref[rows, :] = gate_full[:, :TOP_K]
        rank_ref[:, rows] = rank_full.T[:TOP_K, :].astype(I32)
    carry_ref[...] = carry
    cnt_ref[...] = carry.astype(I32)


def _out_proj(yl, ys, ya, x, mod, gna, wout, pmg, pfg, rwh, rwl, rb):
    B, T, _ = x.shape
    tm = min(TM_OUT, T)
    nt = T // tm
    N = B * T

    def vec(n):
        return pl.BlockSpec((1, n), lambda b, t: (0, 0))

    tok = lambda b, t: (b * nt + t, 0)
    return pl.pallas_call(
        functools.partial(_out_proj_kernel, tm=tm),
        grid=(B, nt),
        in_specs=[pl.BlockSpec((1, tm, LRU_W), lambda b, t: (b, t, 0)),
                  pl.BlockSpec((1, tm, S5_W), lambda b, t: (b, t, 0)),
                  pl.BlockSpec((1, tm, ATTN_W), lambda b, t: (b, t, 0)),
                  pl.BlockSpec((1, tm, D), lambda b, t: (b, t, 0)),
                  pl.BlockSpec((1, N_MOD, D), lambda b, t: (b, 0, 0)),
                  vec(ATTN_W),
                  pl.BlockSpec((D, D), lambda b, t: (0, 0)),
                  vec(D), vec(D),
                  pl.BlockSpec((D, LANES), lambda b, t: (0, 0)),
                  pl.BlockSpec((D, LANES), lambda b, t: (0, 0)),
                  vec(LANES)],
        out_specs=[pl.BlockSpec((1, tm, D), lambda b, t: (b, t, 0)),
                   pl.BlockSpec((tm * ROW_CHUNKS, LANES), tok),
                   pl.BlockSpec((TOP_K, tm), lambda b, t: (0, b * nt + t)),
                   pl.BlockSpec((tm, TOP_K), tok),
                   pl.BlockSpec((TOP_K, tm), lambda b, t: (0, b * nt + t)),
                   pl.BlockSpec((1, LANES), lambda b, t: (0, 0))],
        out_shape=[jax.ShapeDtypeStruct((B, T, D), F32),
                   jax.ShapeDtypeStruct((N * ROW_CHUNKS, LANES), F32),
                   jax.ShapeDtypeStruct((TOP_K, N), I32),
                   jax.ShapeDtypeStruct((N, TOP_K), F32),
                   jax.ShapeDtypeStruct((TOP_K, N), I32),
                   jax.ShapeDtypeStruct((1, LANES), I32)],
        scratch_shapes=[pltpu.VMEM((1, LANES), F32)],
        compiler_params=_params(VMEM_LIMIT),
        name="out_proj_router",
    )(yl, ys, ya, x, mod, gna, wout, pmg, pfg, rwh, rwl, rb)


def _fused_expert_kernel(be_ref, nused_ref, src0_ref, src_ref, dst_ref, h_ref, wgu_ref, bgu_ref, wdn_ref, bdn_ref,
                         y_ref, xbuf, ybuf, wgu_s, wdn_s, gsem, ssem, *, rows, n_blocks, n_assign):
    b = pl.program_id(0)
    nu = nused_ref[0]
    slot = lax.rem(b, 2)
    other = 1 - slot
    xslot = lax.rem(b, 3)
    xnext = lax.rem(b + 2, 3)
    src_off = lax.rem(jnp.minimum(b + 2, n_blocks - 1) * rows, MAP_BLOCK)
    dst_off = lax.rem(jnp.maximum(b - 1, 0) * rows, MAP_BLOCK)

    def gather(i, tok, buf_slot):
        return pltpu.make_async_copy(
            h_ref.at[pl.ds(pl.multiple_of(tok * ROW_CHUNKS, ROW_CHUNKS), ROW_CHUNKS), :],
            xbuf.at[buf_slot, pl.ds(i * ROW_CHUNKS, ROW_CHUNKS), :], gsem.at[buf_slot])

    def scatter(i, dst, buf_slot):
        return pltpu.make_async_copy(
            ybuf.at[buf_slot, pl.ds(i * ROW_CHUNKS, ROW_CHUNKS), :],
            y_ref.at[pl.ds(pl.multiple_of(dst * ROW_CHUNKS, ROW_CHUNKS), ROW_CHUNKS), :], ssem.at[buf_slot])

    @pl.when(b == 0)
    def _():
        ybuf[...] = jnp.zeros(ybuf.shape, F32)
        for i in range(rows):
            gather(i, src0_ref[i], 0).start()
            gather(i, src0_ref[rows + i], 1).start()
            scatter(i, n_assign + 2 * rows + i, 0).start()

    @pl.when(b <= nu)
    def _():
        for i in range(rows):
            gather(i, 0, xslot).wait()

    prev = be_ref[jnp.maximum(b - 1, 0)]

    @pl.when((b == 0) | (be_ref[b] != prev))
    def _():
        wgu_s[...] = wgu_ref[0, 0].astype(BF16)
        wdn_s[...] = wdn_ref[0, 0].astype(BF16)

    @pl.when(b < nu)
    def _():
        for i in range(rows):
            gather(i, src_ref[src_off + i], xnext).start(priority=i % 2)
        for i in range(rows):
            dst = jnp.where(b == 0, n_assign + 3 * rows + i, dst_ref[dst_off + i])
            scatter(i, dst, other).start(priority=i % 2)
        x = jnp.concatenate([xbuf[xslot, pl.ds(c, rows, stride=ROW_CHUNKS), :] for c in range(ROW_CHUNKS)], axis=1)
        gu = jnp.dot(x.astype(BF16), wgu_s[...], preferred_element_type=F32) + bgu_ref[0, 0]
        g = jnp.minimum(gu[:, :D_FF], SWIGLU_LIMIT)
        up = jnp.clip(gu[:, D_FF:], -SWIGLU_LIMIT, SWIGLU_LIMIT)
        act = (up + 1.0) * (g * jax.nn.sigmoid(SWIGLU_ALPHA * g))
        y = jnp.dot(act.astype(BF16), wdn_s[...], preferred_element_type=F32) + bdn_ref[0, 0]
        for i in range(rows):
            scatter(i, 0, slot).wait()
        for c in range(ROW_CHUNKS):
            ybuf[slot, pl.ds(c, rows, stride=ROW_CHUNKS), :] = y[:, c * LANES:(c + 1) * LANES]

    @pl.when(b == nu)
    def _():
        for i in range(rows):
            gather(i, 0, lax.rem(b + 1, 3)).wait()
        for i in range(rows):
            scatter(i, 0, slot).wait()
        for i in range(rows):
            scatter(i, dst_ref[dst_off + i], other).start()
        for i in range(rows):
            scatter(i, 0, other).wait()


def _fused_experts(layer, block_e, n_used, src_tok, dst_row, h_rows, w_gu, b_gu, w_dn, b_dn, rows, n_assign):
    n_blocks = block_e.shape[0]
    L, E = w_gu.shape[:2]
    wsel = lambda b, be, nu: (layer, be[b], 0, 0)
    grid_spec = pltpu.PrefetchScalarGridSpec(
        num_scalar_prefetch=2,
        grid=(n_blocks,),
        in_specs=[pl.BlockSpec((MAP_BLOCK,), lambda b, be, nu: (0,), memory_space=pltpu.SMEM),
                  pl.BlockSpec((MAP_BLOCK,), lambda b, be, nu: (jnp.minimum(b + 2, n_blocks - 1) * rows // MAP_BLOCK,),
                               memory_space=pltpu.SMEM),
                  pl.BlockSpec((MAP_BLOCK,), lambda b, be, nu: (jnp.maximum(b - 1, 0) * rows // MAP_BLOCK,),
                               memory_space=pltpu.SMEM),
                  pl.BlockSpec(memory_space=pl.ANY),
                  pl.BlockSpec((1, 1, D, 2 * D_FF), wsel),
                  pl.BlockSpec((1, 1, 1, 2 * D_FF), wsel),
                  pl.BlockSpec((1, 1, D_FF, D), wsel),
                  pl.BlockSpec((1, 1, 1, D), wsel)],
        out_specs=pl.BlockSpec(memory_space=pl.ANY),
        scratch_shapes=[pltpu.VMEM((3, rows * ROW_CHUNKS, LANES), F32),
                        pltpu.VMEM((2, rows * ROW_CHUNKS, LANES), F32),
                        pltpu.VMEM((D, 2 * D_FF), BF16), pltpu.VMEM((D_FF, D), BF16),
                        pltpu.SemaphoreType.DMA((3,)), pltpu.SemaphoreType.DMA((2,))],
    )
    return pl.pallas_call(
        functools.partial(_fused_expert_kernel, rows=rows, n_blocks=n_blocks, n_assign=n_assign),
        grid_spec=grid_spec,
        out_shape=jax.ShapeDtypeStruct(((n_assign + 4 * rows) * ROW_CHUNKS, LANES), F32),
        compiler_params=_params(VMEM_LIMIT),
        name="moe_experts",
    )(block_e, n_used, src_tok, src_tok, dst_row, h_rows, w_gu, b_gu.reshape(L, E, 1, 2 * D_FF), w_dn,
      b_dn.reshape(L, E, 1, D))


def _dense_combine_kernel(y0_ref, y1_ref, y2_ref, y3_ref, gate_ref, x1_ref, mod_ref, g_ref, o_ref, *, tn):
    gates = gate_ref[...]
    y = jnp.zeros((tn, D), F32)
    for k, yk_ref in enumerate((y0_ref, y1_ref, y2_ref, y3_ref)):
        yk = jnp.concatenate([yk_ref[pl.ds(c, tn, stride=ROW_CHUNKS), :] for c in range(ROW_CHUNKS)], axis=1)
        y = y + gates[:, k:k + 1] * yk
    mod = mod_ref[0]
    o_ref[0] = x1_ref[0] + mod[5:6] * _rms(y, g_ref[...])


def _dense_combine(y_rows, gates, x1, mod, g):
    B, T, _ = x1.shape
    tn = min(TN_COMB, T)
    nt = T // tn

    def slot_spec(k):
        return pl.BlockSpec((tn * ROW_CHUNKS, LANES), lambda b, t: (k * B * nt + b * nt + t, 0))

    return pl.pallas_call(
        functools.partial(_dense_combine_kernel, tn=tn),
        grid=(B, nt),
        in_specs=[slot_spec(0), slot_spec(1), slot_spec(2), slot_spec(3),
                  pl.BlockSpec((tn, TOP_K), lambda b, t: (b * nt + t, 0)),
                  pl.BlockSpec((1, tn, D), lambda b, t: (b, t, 0)),
                  pl.BlockSpec((1, N_MOD, D), lambda b, t: (b, 0, 0)),
                  pl.BlockSpec((1, D), lambda b, t: (0, 0))],
        out_specs=pl.BlockSpec((1, tn, D), lambda b, t: (b, t, 0)),
        out_shape=jax.ShapeDtypeStruct((B, T, D), F32),
        name="moe_combine",
    )(y_rows, y_rows, y_rows, y_rows, gates, x1, mod, g)


def _inverse_row_map(dest, n_rows):
    n_assign = dest.shape[0]
    vals = jnp.broadcast_to(jnp.arange(n_assign, dtype=I32)[:, None], (n_assign, SC_ROW_WORDS))
    mesh = plsc.VectorSubcoreMesh(core_axis_name="core", subcore_axis_name="subcore")

    @functools.partial(pl.kernel, out_type=jax.ShapeDtypeStruct((n_rows, SC_ROW_WORDS), I32), mesh=mesh,
                       scratch_types=[])
    def scatter_rows(x_hbm, i_hbm, o_hbm):
        def body(x_vmem, i_vmem):
            pltpu.sync_copy(x_vmem, o_hbm.at[i_vmem.at[0]])

        pltpu.emit_pipeline(
            body,
            grid=(n_assign // SC_WINDOW,),
            in_specs=[pl.BlockSpec((SC_WINDOW, SC_ROW_WORDS), lambda i: (i, 0)),
                      pl.BlockSpec((1, SC_WINDOW), lambda i: (0, i))],
            out_specs=[],
            core_axis_name=("core", "subcore"),
            dimension_semantics=(pltpu.PARALLEL,),
        )(x_hbm, i_hbm)

    return scatter_rows(vals, dest.reshape(1, n_assign))[:, 0]


def _block_diag(w):
    nb = w.shape[0]
    return jnp.einsum('hij,hg->higj', w, jnp.eye(nb, dtype=w.dtype)).reshape(LRU_W, LRU_W)


def kernel(x, c, ada_w, ada_b, pre_mix_g, w_in, conv_w, conv_b, lru_wa, lru_ba, lru_wx, lru_bx, lru_lambda, s5_a_re, s5_a_im, s5_b_re, s5_b_im, s5_c_re, s5_c_im, s5_d, s5_log_dt, s5_glu_w, s5_glu_b, fox_fb, gn_lru, gn_s5, gn_attn, w_out, post_mix_g, pre_ffn_g, router_w, router_b, w_gu, b_gu, w_dn, b_dn, post_ffn_g):
    B, T, _ = x.shape
    N = B * T
    L = ada_w.shape[0]
    tc_s5 = min(TC_S5, T)
    n_blocks = (N * TOP_K) // EXP_ROWS + N_EXPERTS
    n_rows = n_blocks * EXP_ROWS
    mod_all = _modulation(c, ada_w, ada_b).reshape(L, B, N_MOD, D)

    for l in range(L):
        mod = mod_all[l]
        wl = w_in[l]
        w_aug = jnp.pad(wl, ((0, 0), (0, LANES - N_HEADS))).astype(BF16)
        fb = jnp.pad(fox_fb[l], (0, LANES - N_HEADS)).reshape(1, LANES)
        lx, lg, su, qa, ka, va = _in_proj(x, mod, pre_mix_g[l].reshape(1, D), w_aug, fb)

        yl = _lru_branch(lx, lg, conv_w[l], conv_b[l].reshape(1, LRU_W),
                         _block_diag(lru_wa[l]).astype(BF16), lru_ba[l].reshape(1, LRU_W),
                         _block_diag(lru_wx[l]).astype(BF16), lru_bx[l].reshape(1, LRU_W),
                         lru_lambda[l].reshape(1, LRU_W), gn_lru[l].reshape(1, LRU_W))

        tabs = _s5_tables(s5_a_re[l], s5_a_im[l], s5_b_re[l], s5_b_im[l], s5_c_re[l], s5_c_im[l], s5_log_dt[l], tc_s5)
        ys = _s5_branch(su, *tabs, s5_d[l].reshape(1, S5_W), s5_glu_w[l].astype(BF16),
                        s5_glu_b[l].reshape(1, S5_W), gn_s5[l].reshape(1, S5_W), tc_s5)

        ya = _attention(qa, ka, va)

        rw = jnp.pad(router_w[l], ((0, 0), (0, LANES - N_EXPERTS)))
        rwh = rw.astype(BF16)
        rwl = (rw - rwh.astype(F32)).astype(BF16)
        rb = jnp.pad(router_b[l], (0, LANES - N_EXPERTS)).reshape(1, LANES)
        x1, h_rows, idx, gates, rank, cnt = _out_proj(
            yl, ys, ya, x, mod, gn_attn[l].reshape(1, ATTN_W), w_out[l].astype(BF16),
            post_mix_g[l].reshape(1, D), pre_ffn_g[l].reshape(1, D), rwh, rwl, rb)

        counts = cnt[0, :N_EXPERTS]
        blocks_e = (counts + EXP_ROWS - 1) // EXP_ROWS
        padded = blocks_e * EXP_ROWS
        block_ends = jnp.cumsum(blocks_e)
        starts = (block_ends - blocks_e) * EXP_ROWS
        n_used = block_ends[-1:]
        bid = jnp.minimum(jnp.arange(n_blocks, dtype=I32), n_used[0] - 1)
        block_e = jnp.minimum(jnp.sum((bid[:, None] >= block_ends[None, :]).astype(I32), axis=1), N_EXPERTS - 1)

        n_assign = N * TOP_K
        dest = (jnp.take(starts.astype(I32), idx) + rank).reshape(n_assign)
        row_assign = _inverse_row_map(dest, n_rows)
        first_block = block_ends - blocks_e
        blk = jnp.arange(n_blocks, dtype=I32)
        valid = jnp.clip(jnp.take(counts, block_e) - (blk - jnp.take(first_block, block_e)) * EXP_ROWS, 0, EXP_ROWS)
        valid = jnp.where(blk < n_used[0], valid, 0)
        is_real = (jnp.arange(EXP_ROWS, dtype=I32)[None, :] < valid[:, None]).reshape(n_rows)
        src_tok = jnp.where(is_real, row_assign % N, 0)
        spare = n_assign + jnp.arange(n_rows, dtype=I32) % (2 * EXP_ROWS)
        dst_row = jnp.where(is_real, row_assign, spare)

        y_rows = _fused_experts(l, block_e, n_used.astype(I32), src_tok, dst_row, h_rows,
                                w_gu, b_gu, w_dn, b_dn, EXP_ROWS, n_assign)
        x = _dense_combine(y_rows, gates, x1, mod, post_ffn_g[l].reshape(1, D))
    return x
```

```python
import functools

import jax
import jax.numpy as jnp
from jax import lax
from jax.experimental import pallas as pl
from jax.experimental.pallas import tpu as pltpu
from jax.experimental.pallas import tpu_sc as plsc

F32 = jnp.float32
BF16 = jnp.bfloat16
I32 = jnp.int32

D = 1024
LRU_W = 256
LRU_BLOCK_W = 64
CONV_W = 4
LRU_C = 8.0
S5_W = 256
S5_GROUP_W = 16
S5_GROUPS = 16
S5_STATE = 64
S5_STATES = S5_GROUPS * S5_STATE
HEAD_DIM = 64
ATTN_W = 512
N_HEADS = 8
O_LRU_G = 256
O_S5 = 512
O_Q = 768
O_K = O_Q + ATTN_W
O_V = O_K + ATTN_W
O_F = O_V + ATTN_W
N_EXPERTS = 32
TOP_K = 4
D_FF = 1024
SWIGLU_LIMIT = 7.0
SWIGLU_ALPHA = 1.702
N_MOD = 6
RMS_EPS = 1e-6

LANES = 128
SUBLANES = 8
VMEM_LIMIT = 56 * 1024 * 1024
ROW_CHUNKS = D // LANES

C_Q = O_Q
C_K = O_K
C_V = O_V
C_F = O_F
IN_COLS_PAD = C_F + LANES
L_A0, L_A1, L_A2, L_B0, L_B1, L_B2 = 64, 65, 66, 67, 68, 69

TM_IN = 512
SUBTILES = 2
OUT_SUBTILES = 1
TC_LRU = 256
TC_S5 = 256
TQ = 512
ATTN_HEADS_PER_STEP = 4
TM_OUT = 512
TN_COMB = 512
EXP_ROWS = 512
MAP_BLOCK = 1024
SC_WINDOW = 128
SC_ROW_WORDS = 128
NEG_BIG = -1e30


def _rms(x, g):
    return x * lax.rsqrt(jnp.mean(x * x, axis=-1, keepdims=True) + RMS_EPS) * g


def _shift_rows(v, d, row, fill):
    return jnp.where(row >= d, pltpu.roll(v, d, 0), fill)


def _params(vmem=None):
    return pltpu.CompilerParams(vmem_limit_bytes=vmem) if vmem else None


def _mod_kernel(c_ref, w_ref, b_ref, o_ref):
    c = c_ref[...]
    s = c * jax.nn.sigmoid(c)
    o_ref[0] = jnp.dot(s.astype(BF16), w_ref[0].astype(BF16), preferred_element_type=F32) + b_ref[0]


def _modulation(c, ada_w, ada_b):
    L, _, W = ada_w.shape
    B = c.shape[0]
    return pl.pallas_call(
        _mod_kernel,
        grid=(L, W // D),
        in_specs=[pl.BlockSpec((B, D), lambda l, j: (0, 0)),
                  pl.BlockSpec((1, D, D), lambda l, j: (l, 0, j)),
                  pl.BlockSpec((1, 1, D), lambda l, j: (l, 0, j))],
        out_specs=pl.BlockSpec((1, B, D), lambda l, j: (l, 0, j)),
        out_shape=jax.ShapeDtypeStruct((L, B, W), F32),
        name="adaln_mod",
    )(c, ada_w, ada_b.reshape(L, 1, W))


def _in_proj_kernel(x_ref, mod_ref, g_ref, w_ref, fb_ref,
                    lx_ref, lg_ref, su_ref, q_ref, k_ref, v_ref, fcarry_ref, *, tm):
    @pl.when(pl.program_id(1) == 0)
    def _():
        fcarry_ref[...] = jnp.zeros_like(fcarry_ref)

    mod = mod_ref[0]
    ts = tm // SUBTILES
    row = lax.broadcasted_iota(I32, (ts, LANES), 0)
    lane = lax.broadcasted_iota(I32, (ts, LANES), 1)
    carry = fcarry_ref[...]
    for sub in range(SUBTILES):
        rows = slice(sub * ts, (sub + 1) * ts)
        h = _rms(x_ref[0, rows, :], g_ref[...]) * (1.0 + mod[1:2]) + mod[0:1]
        p = jnp.dot(h.astype(BF16), w_ref[...], preferred_element_type=F32)
        lx_ref[0, rows, :] = p[:, 0:O_LRU_G]
        lg_ref[0, rows, :] = p[:, O_LRU_G:O_S5]
        su_ref[0, rows, :] = p[:, O_S5:O_Q]

        z = p[:, C_F:C_F + LANES] + fb_ref[...]
        logf = jnp.minimum(z, 0.0) - jnp.log1p(jnp.exp(-jnp.abs(z)))
        d = 1
        while d < ts:
            logf = logf + _shift_rows(logf, d, row, 0.0)
            d *= 2
        fsum = logf + carry
        carry = fsum[ts - 1:ts, :]

        pq = p[:, C_Q:C_K] * (HEAD_DIM ** -0.5)
        pk = p[:, C_K:C_V]
        pv = p[:, C_V:C_F]
        for hd in range(N_HEADS):
            pair = slice((hd // 2) * LANES, (hd // 2 + 1) * LANES)

            def head_low(a):
                s = a[:, pair]
                return s if hd % 2 == 0 else pltpu.roll(s, HEAD_DIM, 1)

            f = jnp.broadcast_to(fsum[:, hd:hd + 1], (ts, LANES))
            hi = f.astype(BF16).astype(F32)
            r1 = f - hi
            mid = r1.astype(BF16).astype(F32)
            lo = r1 - mid
            q_aug = jnp.where(lane == L_A0, hi, jnp.where(lane == L_A1, mid, jnp.where(lane == L_A2, lo,
                              jnp.where((lane >= L_B0) & (lane <= L_B2), 1.0, 0.0))))
            k_aug = jnp.where(lane == L_B0, -hi, jnp.where(lane == L_B1, -mid, jnp.where(lane == L_B2, -lo,
                              jnp.where((lane >= L_A0) & (lane <= L_A2), 1.0, 0.0))))
            v_aug = jnp.where(lane == HEAD_DIM, 1.0, 0.0)
            q_ref[0, hd, rows, :] = jnp.where(lane < HEAD_DIM, head_low(pq), q_aug).astype(BF16)
            k_ref[0, hd, rows, :] = jnp.where(lane < HEAD_DIM, head_low(pk), k_aug).astype(BF16)
            v_ref[0, hd, rows, :] = jnp.where(lane < HEAD_DIM, head_low(pv), v_aug).astype(BF16)
    fcarry_ref[...] = carry


def _in_proj(x, mod, g, w_aug, fb):
    B, T, _ = x.shape
    tm = min(TM_IN, T)
    grp = jax.ShapeDtypeStruct((B, T, 256), F32)
    head = jax.ShapeDtypeStruct((B, N_HEADS, T, LANES), BF16)
    grp_spec = pl.BlockSpec((1, tm, 256), lambda b, t: (b, t, 0))
    head_spec = pl.BlockSpec((1, N_HEADS, tm, LANES), lambda b, t: (b, 0, t, 0))
    return pl.pallas_call(
        functools.partial(_in_proj_kernel, tm=tm),
        grid=(B, T // tm),
        in_specs=[pl.BlockSpec((1, tm, D), lambda b, t: (b, t, 0)),
                  pl.BlockSpec((1, N_MOD, D), lambda b, t: (b, 0, 0)),
                  pl.BlockSpec((1, D), lambda b, t: (0, 0)),
                  pl.BlockSpec((D, IN_COLS_PAD), lambda b, t: (0, 0)),
                  pl.BlockSpec((1, LANES), lambda b, t: (0, 0))],
        out_specs=[grp_spec, grp_spec, grp_spec, head_spec, head_spec, head_spec],
        out_shape=[grp, grp, grp, head, head, head],
        scratch_shapes=[pltpu.VMEM((1, LANES), F32)],
        compiler_params=_params(VMEM_LIMIT),
        name="in_proj",
    )(x, mod, g, w_aug, fb)


def _lru_kernel(lx_ref, lg_ref, cw_ref, cb_ref, wa_ref, ba_ref, wx_ref, bx_ref, lam_ref, gn_ref,
                y_ref, tail_ref, hcarry_ref, *, tc):
    @pl.when(pl.program_id(1) == 0)
    def _():
        tail_ref[...] = jnp.zeros_like(tail_ref)
        hcarry_ref[...] = jnp.zeros_like(hcarry_ref)

    x = lx_ref[0]
    xcat = jnp.concatenate([tail_ref[...], x], axis=0)
    cw = cw_ref[...]
    xr = cb_ref[...]
    for j in range(CONV_W):
        d = CONV_W - 1 - j
        xs = x if d == 0 else pltpu.roll(xcat, d, 0)[SUBLANES:SUBLANES + tc]
        xr = xr + xs * cw[j:j + 1]
    tail_ref[...] = x[tc - SUBLANES:tc]

    xb = xr.astype(BF16)
    r = jax.nn.sigmoid(jnp.dot(xb, wa_ref[...], preferred_element_type=F32) + ba_ref[...])
    i = jax.nn.sigmoid(jnp.dot(xb, wx_ref[...], preferred_element_type=F32) + bx_ref[...])
    nl = -lam_ref[...]
    softplus = jnp.maximum(nl, 0.0) + jnp.log1p(jnp.exp(-jnp.abs(nl)))
    log_a = -LRU_C * r * softplus
    a = jnp.exp(log_a)
    th = jnp.tanh(log_a)
    u = xr * i * jnp.sqrt(-2.0 * th / (1.0 - th))

    row = lax.broadcasted_iota(I32, (tc, LRU_W), 0)
    d = 1
    while d < tc:
        u = u + a * _shift_rows(u, d, row, 0.0)
        a = a * _shift_rows(a, d, row, 1.0)
        d *= 2
    h = u + a * hcarry_ref[...]
    hcarry_ref[...] = h[tc - 1:tc]

    y = h * jax.nn.gelu(lg_ref[0])
    y_ref[0] = _rms(y, gn_ref[...]).astype(BF16)


def _lru_branch(lx, lg, conv_w, conv_b, wa, ba, wx, bx, lam, gn):
    B, T, _ = lx.shape
    tc = min(TC_LRU, T)
    tile = pl.BlockSpec((1, tc, LRU_W), lambda b, t: (b, t, 0))
    vec = pl.BlockSpec((1, LRU_W), lambda b, t: (0, 0))
    mat = pl.BlockSpec((LRU_W, LRU_W), lambda b, t: (0, 0))
    return pl.pallas_call(
        functools.partial(_lru_kernel, tc=tc),
        grid=(B, T // tc),
        in_specs=[tile, tile, pl.BlockSpec((CONV_W, LRU_W), lambda b, t: (0, 0)), vec, mat, vec, mat, vec, vec, vec],
        out_specs=tile,
        out_shape=jax.ShapeDtypeStruct((B, T, LRU_W), BF16),
        scratch_shapes=[pltpu.VMEM((SUBLANES, LRU_W), F32), pltpu.VMEM((1, LRU_W), F32)],
        name="rg_lru",
    )(lx, lg, conv_w, conv_b, wa, ba, wx, bx, lam, gn)


def _s5_kernel(u_ref, bcat_ref, apr_ref, api_ref, p8r_ref, p8i_ref, bpr_ref, bpi_ref, ccat_ref, d_ref, gw_ref,
               gb_ref, gn_ref, y_ref, cr_ref, ci_ref, xsr_ref, xsi_ref, gr_ref, gi_ref, *, tc):
    @pl.when(pl.program_id(1) == 0)
    def _():
        cr_ref[...] = jnp.zeros_like(cr_ref)
        ci_ref[...] = jnp.zeros_like(ci_ref)

    def cmul_add(xr, xi, ar, ai, sr, si):
        return xr + (ar * sr - ai * si), xi + (ar * si + ai * sr)

    u = u_ref[0]
    bu = jnp.dot(u.astype(BF16), bcat_ref[...], preferred_element_type=F32)
    xr = bu[:, :S5_STATES]
    xi = bu[:, S5_STATES:]
    nb = tc // SUBLANES
    blocks = (nb, SUBLANES, S5_STATES)
    sub = lax.broadcasted_iota(I32, blocks, 1)
    xr = xr.reshape(blocks)
    xi = xi.reshape(blocks)
    for k in range(3):
        d = 1 << k
        ar = apr_ref[k:k + 1, :].reshape(1, 1, S5_STATES)
        ai = api_ref[k:k + 1, :].reshape(1, 1, S5_STATES)
        xr, xi = cmul_add(xr, xi, ar, ai, jnp.where(sub >= d, pltpu.roll(xr, d, 1), 0.0),
                          jnp.where(sub >= d, pltpu.roll(xi, d, 1), 0.0))
    xr = xr.reshape(tc, S5_STATES)
    xi = xi.reshape(tc, S5_STATES)
    lane_tiles = range(S5_STATES // LANES)

    def put(ref, v):
        for l in lane_tiles:
            ref[l] = v[:, l * LANES:(l + 1) * LANES]

    put(xsr_ref, xr)
    put(xsi_ref, xi)
    er =jnp.concatenate([xsr_ref[l, pl.ds(SUBLANES - 1, nb, stride=SUBLANES), :] for l in lane_tiles], axis=1)
    ei = jnp.concatenate([xsi_ref[l, pl.ds(SUBLANES - 1, nb, stride=SUBLANES), :] for l in lane_tiles], axis=1)
    blk = lax.broadcasted_iota(I32, (nb, S5_STATES), 0)
    k, d = 3, 1
    while d < nb:
        er, ei = cmul_add(er, ei, apr_ref[k:k + 1, :], api_ref[k:k + 1, :],
                          _shift_rows(er, d, blk, 0.0), _shift_rows(ei, d, blk, 0.0))
        k, d = k + 1, d * 2
    gr, gi = cmul_add(_shift_rows(er, 1, blk, 0.0), _shift_rows(ei, 1, blk, 0.0),
                      bpr_ref[...], bpi_ref[...], cr_ref[...], ci_ref[...])
    for j in range(SUBLANES):
        for l in lane_tiles:
            gr_ref[l, pl.ds(j, nb, stride=SUBLANES), :] = gr[:, l * LANES:(l + 1) * LANES]
            gi_ref[l, pl.ds(j, nb, stride=SUBLANES), :] = gi[:, l * LANES:(l + 1) * LANES]
    xr, xi = cmul_add(xr, xi, jnp.tile(p8r_ref[...], (nb, 1)), jnp.tile(p8i_ref[...], (nb, 1)),
                      jnp.concatenate([gr_ref[l] for l in lane_tiles], axis=1),
                      jnp.concatenate([gi_ref[l] for l in lane_tiles], axis=1))
    cr_ref[...] = xr[tc - 1:tc]
    ci_ref[...] = xi[tc - 1:tc]

    xcat = jnp.concatenate([xr, xi], axis=1).astype(BF16)
    y = jnp.dot(xcat, ccat_ref[...], preferred_element_type=F32) + d_ref[...] * u
    y = jax.nn.gelu(y)
    y = y * jax.nn.sigmoid(jnp.dot(y.astype(BF16), gw_ref[...], preferred_element_type=F32) + gb_ref[...])
    y_ref[0] = _rms(y, gn_ref[...]).astype(BF16)


def _s5_branch(u, bcat, apow_re, apow_im, p8_re, p8_im, bp_re, bp_im, ccat, d, glu_w, glu_b, gn, tc):
    B, T, _ = u.shape
    tile = pl.BlockSpec((1, tc, S5_W), lambda b, t: (b, t, 0))
    vec = pl.BlockSpec((1, S5_W), lambda b, t: (0, 0))

    def full(a):
        return pl.BlockSpec(a.shape, lambda b, t: (0, 0))

    return pl.pallas_call(
        functools.partial(_s5_kernel, tc=tc),
        grid=(B, T // tc),
        in_specs=[tile, full(bcat), full(apow_re), full(apow_im), full(p8_re), full(p8_im), full(bp_re),
                  full(bp_im), full(ccat), vec, full(glu_w), vec, vec],
        out_specs=tile,
        out_shape=jax.ShapeDtypeStruct((B, T, S5_W), BF16),
        scratch_shapes=[pltpu.VMEM((1, S5_STATES), F32), pltpu.VMEM((1, S5_STATES), F32)]
        + [pltpu.VMEM((S5_STATES // LANES, tc, LANES), F32)] * 4,
        compiler_params=_params(VMEM_LIMIT),
        name="s5",
    )(u, bcat, apow_re, apow_im, p8_re, p8_im, bp_re, bp_im, ccat, d, glu_w, glu_b, gn)


def _s5_tables(a_re, a_im, b_re, b_im, c_re, c_im, log_dt, tc):
    dt = jnp.exp(log_dt)[:, None]
    mag = jnp.exp(a_re * dt)
    abar_re = mag * jnp.cos(a_im * dt)
    abar_im = mag * jnp.sin(a_im * dt)
    den = a_re * a_re + a_im * a_im
    num_re = abar_re - 1.0
    k_re = (num_re * a_re + abar_im * a_im) / den
    k_im = (abar_im * a_re - num_re * a_im) / den
    bbar_re = k_re[..., None] * b_re - k_im[..., None] * b_im
    bbar_im = k_re[..., None] * b_im + k_im[..., None] * b_re
    eye = jnp.eye(S5_GROUPS, dtype=F32)
    bd_re = jnp.einsum('gpc,gh->gchp', bbar_re, eye).reshape(S5_W, S5_STATES)
    bd_im = jnp.einsum('gpc,gh->gchp', bbar_im, eye).reshape(S5_W, S5_STATES)
    bcat = jnp.concatenate([bd_re, bd_im], axis=1).astype(BF16)
    cd_re = jnp.einsum('gcp,gh->gphc', c_re, eye).reshape(S5_STATES, S5_W)
    cd_im = jnp.einsum('gcp,gh->gphc', c_im, eye).reshape(S5_STATES, S5_W)
    ccat = jnp.concatenate([cd_re, -cd_im], axis=0).astype(BF16)
    ar = abar_re.reshape(1, S5_STATES)
    ai = abar_im.reshape(1, S5_STATES)
    pows_r, pows_i = [ar], [ai]
    tab_r, tab_i = ar, ai
    n = 1
    while n < tc:
        sr, si = pows_r[-1], pows_i[-1]
        tab_r, tab_i = (jnp.concatenate([tab_r, tab_r * sr - tab_i * si], axis=0),
                        jnp.concatenate([tab_i, tab_r * si + tab_i * sr], axis=0))
        pows_r.append(sr * sr - si * si)
        pows_i.append(2.0 * sr * si)
        n *= 2
    levels = len(pows_r) - 1
    pad = (-levels) % SUBLANES
    apow_re = jnp.concatenate(pows_r[:levels] + [jnp.zeros((pad, S5_STATES), F32)], axis=0)
    apow_im = jnp.concatenate(pows_i[:levels] + [jnp.zeros((pad, S5_STATES), F32)], axis=0)
    nb = tc // SUBLANES
    bp_re = jnp.concatenate([jnp.ones((1, S5_STATES), F32), tab_r[SUBLANES - 1::SUBLANES][:nb - 1]], axis=0)
    bp_im = jnp.concatenate([jnp.zeros((1, S5_STATES), F32), tab_i[SUBLANES - 1::SUBLANES][:nb - 1]], axis=0)
    return bcat, apow_re, apow_im, tab_r[:SUBLANES], tab_i[:SUBLANES], bp_re, bp_im, ccat


def _attn_kernel(qt_ref, k_ref, vt_ref, o_ref, p_ref, *, tq):
    qi = pl.program_id(2)
    key = lax.broadcasted_iota(I32, (tq, tq), 0)
    qry = lax.broadcasted_iota(I32, (tq, tq), 1)
    heads = range(ATTN_HEADS_PER_STEP)
    qts =[qt_ref[0, hh] for hh in heads]

    def scores(hh, kj):
        k = k_ref[0, hh, pl.ds(pl.multiple_of(kj * tq, tq), tq), :]
        return jnp.dot(k, qts[hh], preferred_element_type=F32)

    def weighted_values(hh, kj, p):
        vt = vt_ref[0, hh, :, pl.ds(pl.multiple_of(kj * tq, tq), tq)]
        return jnp.dot(vt, p, preferred_element_type=F32)

    def softmax_step(s, m):
        m_new = jnp.maximum(m, jnp.max(s, axis=0, keepdims=True))
        return m_new, jnp.exp(m - m_new), jnp.exp(s - m_new).astype(BF16)

    p_ref[...] = jnp.zeros_like(p_ref)

    def body(kj, carry):
        ss = [scores(hh, kj) for hh in heads]
        pvs = [weighted_values(hh, jnp.maximum(kj - 1, 0), p_ref[hh]) for hh in heads]
        out = []
        for hh in heads:
            m, alpha, acc = carry[hh]
            acc = alpha * acc + pvs[hh]
            m, alpha, p = softmax_step(ss[hh], m)
            p_ref[hh] = p
            out.append((m, alpha, acc))
        return tuple(out)

    init = tuple((jnp.full((1, tq), NEG_BIG, F32), jnp.ones((1, tq), F32), jnp.zeros((LANES, tq), F32))
                 for _ in heads)
    carry = lax.fori_loop(0, qi, body, init)
    outs = []
    for hh in heads:
        m, alpha, acc = carry[hh]
        acc = alpha * acc + weighted_values(hh, jnp.maximum(qi - 1, 0), p_ref[hh])
        s = jnp.where(key <= qry, scores(hh, qi), NEG_BIG)
        m, alpha, p = softmax_step(s, m)
        acc = alpha * acc + weighted_values(hh, qi, p)
        out_t = acc / acc[HEAD_DIM:HEAD_DIM + 1, :]
        outs.append(out_t.T[:, :HEAD_DIM])
    o_ref[0] = jnp.concatenate(outs, axis=1)


def _attention(qa, ka, va):
    B, H, T, _ = qa.shape
    tq = min(TQ, T)
    hs = ATTN_HEADS_PER_STEP
    qt = jnp.swapaxes(qa, 2, 3)
    vt = jnp.swapaxes(va, 2, 3)
    return pl.pallas_call(
        functools.partial(_attn_kernel, tq=tq),
        grid=(B, H // hs, T // tq),
        in_specs=[pl.BlockSpec((1, hs, LANES, tq), lambda b, hp, qi: (b, hp, 0, qi)),
                  pl.BlockSpec((1, hs, T, LANES), lambda b, hp, qi: (b, hp, 0, 0)),
                  pl.BlockSpec((1, hs, LANES, T), lambda b, hp, qi: (b, hp, 0, 0))],
        out_specs=pl.BlockSpec((1, tq, hs * HEAD_DIM), lambda b, hp, qi: (b, qi, hp)),
        out_shape=jax.ShapeDtypeStruct((B, T, ATTN_W), F32),
        scratch_shapes=[pltpu.VMEM((hs, tq, tq), BF16)],
        compiler_params=_params(VMEM_LIMIT),
        name="fox_attention",
    )(qt, ka, vt)


def _out_proj_kernel(yl_ref, ys_ref, ya_ref, x_ref, mod_ref, gna_ref, wout_ref, pmg_ref, pfg_ref,
                     rw_ref, rb_ref,
                     x1_ref, h2_ref, idx_ref, gate_ref, rank_ref, cnt_ref, carry_ref, *, tm):
    @pl.when((pl.program_id(0) == 0) & (pl.program_id(1) == 0))
    def _():
        carry_ref[...] = jnp.zeros_like(carry_ref)

    mod = mod_ref[0]
    ts = tm // OUT_SUBTILES
    lane = lax.broadcasted_iota(I32, (ts, LANES), 1)
    lane_f = lane.astype(F32)
    r_i = lax.broadcasted_iota(I32, (ts, ts), 0)
    c_i = lax.broadcasted_iota(I32, (ts, ts), 1)
    below = jnp.where(c_i < r_i, 1.0, 0.0).astype(BF16)
    carry = carry_ref[...]
    for sub in range(OUT_SUBTILES):
        rows = slice(sub * ts, (sub + 1) * ts)
        ya = _rms(ya_ref[0, rows, :], gna_ref[...])
        ycat = jnp.concatenate([yl_ref[0, rows, :], ys_ref[0, rows, :], ya.astype(BF16)], axis=1)
        y = jnp.dot(ycat, wout_ref[...], preferred_element_type=F32)
        x1 = x_ref[0, rows, :] + mod[2:3] * _rms(y, pmg_ref[...])
        x1_ref[0, rows, :] = x1
        h2 = _rms(x1, pfg_ref[...]) * (1.0 + mod[4:5]) + mod[3:4]
        for c in range(ROW_CHUNKS):
            h2_ref[pl.ds(sub * ts * ROW_CHUNKS + c, ts, stride=ROW_CHUNKS), :] = h2[:, c * LANES:(c + 1) * LANES]

        hh = h2.astype(BF16)
        hl = (h2 - hh.astype(F32)).astype(BF16)
        both = jnp.dot(hh, rw_ref[...], preferred_element_type=F32)
        logits = (both[:, :LANES] + jnp.dot(hl, rw_ref[:, :LANES], preferred_element_type=F32)
                  + both[:, LANES:] + rb_ref[...])
        work = jnp.where(lane < N_EXPERTS, logits, NEG_BIG)
        vals, idxs = [], []
        for _ in range(TOP_K):
            mx = jnp.max(work, axis=1, keepdims=True)
            ik = jnp.min(jnp.where(work == mx, lane_f, float(LANES)), axis=1, keepdims=True)
            vals.append(mx)
            idxs.append(ik)
            work = jnp.where(lane_f == ik, 2.0 * NEG_BIG, work)
        es = [jnp.exp(v - vals[0]) for v in vals]
        den = es[0] + es[1] + es[2] + es[3]

        onehot = jnp.zeros((ts, LANES), F32)
        for ik in idxs:
            onehot = onehot + jnp.where(lane_f == ik, 1.0, 0.0)
        prior = jnp.dot(below, onehot.astype(BF16), preferred_element_type=F32) + carry
        carry = carry + jnp.sum(onehot, axis=0, keepdims=True)

        idx_full = jnp.zeros((ts, LANES), F32)
        gate_full = jnp.zeros((ts, LANES), F32)
        rank_full = jnp.zeros((ts, LANES), F32)
        for k in range(TOP_K):
            rk = jnp.sum(jnp.where(lane_f == idxs[k], prior, 0.0), axis=1, keepdims=True)
            idx_full = jnp.where(lane == k, idxs[k], idx_full)
            gate_full = jnp.where(lane == k, es[k] / den, gate_full)
            rank_full = jnp.where(lane == k, rk, rank_full)
        idx_ref[rows, :] = idx_full[:, :TOP_K].astype(I32)
        gate_ref[rows, :] = gate_full[:, :TOP_K]
        rank_ref[rows, :] = rank_full[:, :TOP_K].astype(I32)
    carry_ref[...] = carry
    cnt_ref[...] = carry.astype(I32)


def _out_proj(yl, ys, ya, x, mod, gna, wout, pmg, pfg, rw_split, rb):
    B, T, _ = x.shape
    tm = min(TM_OUT, T)
    nt = T // tm
    N = B * T

    def vec(n):
        return pl.BlockSpec((1, n), lambda b, t: (0, 0))

    tok = lambda b, t: (b * nt + t, 0)
    return pl.pallas_call(
        functools.partial(_out_proj_kernel, tm=tm),
        grid=(B, nt),
        in_specs=[pl.BlockSpec((1, tm, LRU_W), lambda b, t: (b, t, 0)),
                  pl.BlockSpec((1, tm, S5_W), lambda b, t: (b, t, 0)),
                  pl.BlockSpec((1, tm, ATTN_W), lambda b, t: (b, t, 0)),
                  pl.BlockSpec((1, tm, D), lambda b, t: (b, t, 0)),
                  pl.BlockSpec((1, N_MOD, D), lambda b, t: (b, 0, 0)),
                  vec(ATTN_W),
                  pl.BlockSpec((D, D), lambda b, t: (0, 0)),
                  vec(D), vec(D),
                  pl.BlockSpec((D, 2 * LANES), lambda b, t: (0, 0)),
                  vec(LANES)],
        out_specs=[pl.BlockSpec((1, tm, D), lambda b, t: (b, t, 0)),
                   pl.BlockSpec((tm * ROW_CHUNKS, LANES), tok),
                   pl.BlockSpec((tm, TOP_K), tok),
                   pl.BlockSpec((tm, TOP_K), tok),
                   pl.BlockSpec((tm, TOP_K), tok),
                   pl.BlockSpec((1, LANES), lambda b, t: (0, 0))],
        out_shape=[jax.ShapeDtypeStruct((B, T, D), F32),
                   jax.ShapeDtypeStruct((N * ROW_CHUNKS, LANES), F32),
                   jax.ShapeDtypeStruct((N, TOP_K), I32),
                   jax.ShapeDtypeStruct((N, TOP_K), F32),
                   jax.ShapeDtypeStruct((N, TOP_K), I32),
                   jax.ShapeDtypeStruct((1, LANES), I32)],
        scratch_shapes=[pltpu.VMEM((1, LANES), F32)],
        compiler_params=_params(VMEM_LIMIT),
        name="out_proj_router",
    )(yl, ys, ya, x, mod, gna, wout, pmg, pfg, rw_split, rb)


def _fused_expert_kernel(be_ref, nused_ref, src0_ref, src_ref, dst_ref, h_ref, wgu_ref, bgu_ref, wdn_ref, bdn_ref,
                         y_ref, xbuf, ybuf, wgu_s, wdn_s, gsem, ssem, *, rows, n_blocks, n_assign):
    b = pl.program_id(0)
    nu = nused_ref[0]
    slot = lax.rem(b, 2)
    other = 1 - slot
    xslot = lax.rem(b, 3)
    xnext = lax.rem(b + 2, 3)
    src_off = lax.rem(jnp.minimum(b + 2, n_blocks - 1) * rows, MAP_BLOCK)
    dst_off = lax.rem(jnp.maximum(b - 1, 0) * rows, MAP_BLOCK)

    def gather(i, tok, buf_slot):
        return pltpu.make_async_copy(
            h_ref.at[pl.ds(pl.multiple_of(tok * ROW_CHUNKS, ROW_CHUNKS), ROW_CHUNKS), :],
            xbuf.at[buf_slot, pl.ds(i * ROW_CHUNKS, ROW_CHUNKS), :], gsem.at[buf_slot])

    def scatter(i, dst, buf_slot):
        return pltpu.make_async_copy(
            ybuf.at[buf_slot, pl.ds(i * ROW_CHUNKS, ROW_CHUNKS), :],
            y_ref.at[pl.ds(pl.multiple_of(dst * ROW_CHUNKS, ROW_CHUNKS), ROW_CHUNKS), :], ssem.at[buf_slot])

    @pl.when(b == 0)
    def _():
        ybuf[...] = jnp.zeros(ybuf.shape, F32)
        for i in range(rows):
            gather(i, src0_ref[i], 0).start()
            gather(i, src0_ref[rows + i], 1).start()
            scatter(i, n_assign + 2 * rows + i, 0).start()

    @pl.when(b <= nu)
    def _():
        for i in range(rows):
            gather(i, 0, xslot).wait()

    prev = be_ref[jnp.maximum(b - 1, 0)]

    @pl.when((b == 0) | (be_ref[b] != prev))
    def _():
        wgu_s[...] = wgu_ref[0, 0].astype(BF16)
        wdn_s[...] = wdn_ref[0, 0].astype(BF16)

    @pl.when(b < nu)
    def _():
        for i in range(rows):
            gather(i, src_ref[src_off + i], xnext).start(priority=i % 2)
        for i in range(rows):
            dst = jnp.where(b == 0, n_assign + 3 * rows + i, dst_ref[dst_off + i])
            scatter(i, dst, other).start(priority=i % 2)
        x = jnp.concatenate([xbuf[xslot, pl.ds(c, rows, stride=ROW_CHUNKS), :] for c in range(ROW_CHUNKS)], axis=1)
        gu = jnp.dot(x.astype(BF16), wgu_s[...], preferred_element_type=F32) + bgu_ref[0, 0]
        g = jnp.minimum(gu[:, :D_FF], SWIGLU_LIMIT)
        up = jnp.clip(gu[:, D_FF:], -SWIGLU_LIMIT, SWIGLU_LIMIT)
        act = (up + 1.0) * (g * jax.nn.sigmoid(SWIGLU_ALPHA * g))
        y = jnp.dot(act.astype(BF16), wdn_s[...], preferred_element_type=F32) + bdn_ref[0, 0]
        for i in range(rows):
            scatter(i, 0, slot).wait()
        for c in range(ROW_CHUNKS):
            ybuf[slot, pl.ds(c, rows, stride=ROW_CHUNKS), :] = y[:, c * LANES:(c + 1) * LANES]

    @pl.when(b == nu)
    def _():
        for i in range(rows):
            gather(i, 0, lax.rem(b + 1, 3)).wait()
        for i in range(rows):
            scatter(i, 0, slot).wait()
        for i in range(rows):
            scatter(i, dst_ref[dst_off + i], other).start()
        for i in range(rows):
            scatter(i, 0, other).wait()


def _fused_experts(layer, block_e, n_used, src_tok, dst_row, h_rows, w_gu, b_gu, w_dn, b_dn, rows, n_assign):
    n_blocks = block_e.shape[0]
    L, E = w_gu.shape[:2]
    wsel = lambda b, be, nu: (layer, be[b], 0, 0)
    grid_spec = pltpu.PrefetchScalarGridSpec(
        num_scalar_prefetch=2,
        grid=(n_blocks,),
        in_specs=[pl.BlockSpec((MAP_BLOCK,), lambda b, be, nu: (0,), memory_space=pltpu.SMEM),
                  pl.BlockSpec((MAP_BLOCK,), lambda b, be, nu: (jnp.minimum(b + 2, n_blocks - 1) * rows // MAP_BLOCK,),
                               memory_space=pltpu.SMEM),
                  pl.BlockSpec((MAP_BLOCK,), lambda b, be, nu: (jnp.maximum(b - 1, 0) * rows // MAP_BLOCK,),
                               memory_space=pltpu.SMEM),
                  pl.BlockSpec(memory_space=pl.ANY),
                  pl.BlockSpec((1, 1, D, 2 * D_FF), wsel),
                  pl.BlockSpec((1, 1, 1, 2 * D_FF), wsel),
                  pl.BlockSpec((1, 1, D_FF, D), wsel),
                  pl.BlockSpec((1, 1, 1, D), wsel)],
        out_specs=pl.BlockSpec(memory_space=pl.ANY),
        scratch_shapes=[pltpu.VMEM((3, rows * ROW_CHUNKS, LANES), F32),
                        pltpu.VMEM((2, rows * ROW_CHUNKS, LANES), F32),
                        pltpu.VMEM((D, 2 * D_FF), BF16), pltpu.VMEM((D_FF, D), BF16),
                        pltpu.SemaphoreType.DMA((3,)), pltpu.SemaphoreType.DMA((2,))],
    )
    return pl.pallas_call(
        functools.partial(_fused_expert_kernel, rows=rows, n_blocks=n_blocks, n_assign=n_assign),
        grid_spec=grid_spec,
        out_shape=jax.ShapeDtypeStruct(((n_assign + 4 * rows) * ROW_CHUNKS, LANES), F32),
        compiler_params=_params(VMEM_LIMIT),
        name="moe_experts",
    )(block_e, n_used, src_tok, src_tok, dst_row, h_rows, w_gu, b_gu.reshape(L, E, 1, 2 * D_FF), w_dn,
      b_dn.reshape(L, E, 1, D))


def _dense_combine_kernel(y0_ref, y1_ref, y2_ref, y3_ref, gate_ref, x1_ref, mod_ref, g_ref, o_ref, *, tn):
    gates = gate_ref[...]
    y = jnp.zeros((tn, D), F32)
    for k, yk_ref in enumerate((y0_ref, y1_ref, y2_ref, y3_ref)):
        yk = jnp.concatenate([yk_ref[pl.ds(c, tn, stride=ROW_CHUNKS), :] for c in range(ROW_CHUNKS)], axis=1)
        y = y + gates[:, k:k + 1] * yk
    mod = mod_ref[0]
    o_ref[0] = x1_ref[0] + mod[5:6] * _rms(y, g_ref[...])


def _dense_combine(y_rows, gates, x1, mod, g):
    B, T, _ = x1.shape
    tn = min(TN_COMB, T)
    nt = T // tn

    def slot_spec(k):
        return pl.BlockSpec((tn * ROW_CHUNKS, LANES), lambda b, t: (k * B * nt + b * nt + t, 0))

    return pl.pallas_call(
        functools.partial(_dense_combine_kernel, tn=tn),
        grid=(B, nt),
        in_specs=[slot_spec(0), slot_spec(1), slot_spec(2), slot_spec(3),
                  pl.BlockSpec((tn, TOP_K), lambda b, t: (b * nt + t, 0)),
                  pl.BlockSpec((1, tn, D), lambda b, t: (b, t, 0)),
                  pl.BlockSpec((1, N_MOD, D), lambda b, t: (b, 0, 0)),
                  pl.BlockSpec((1, D), lambda b, t: (0, 0))],
        out_specs=pl.BlockSpec((1, tn, D), lambda b, t: (b, t, 0)),
        out_shape=jax.ShapeDtypeStruct((B, T, D), F32),
        compiler_params=_params(VMEM_LIMIT),
        name="moe_combine",
    )(y_rows, y_rows, y_rows, y_rows, gates, x1, mod, g)


def _assignment_rows(n_assign):
    return jnp.broadcast_to(jnp.arange(n_assign, dtype=I32)[:, None], (n_assign, SC_ROW_WORDS))


def _inverse_row_map(dest, vals, n_rows):
    n_assign = dest.shape[0]
    mesh = plsc.VectorSubcoreMesh(core_axis_name="core", subcore_axis_name="subcore")

    @functools.partial(pl.kernel, out_type=jax.ShapeDtypeStruct((n_rows, SC_ROW_WORDS), I32), mesh=mesh,
                       scratch_types=[])
    def scatter_rows(x_hbm, i_hbm, o_hbm):
        def body(x_vmem, i_vmem):
            pltpu.sync_copy(x_vmem, o_hbm.at[i_vmem.at[0]])

        pltpu.emit_pipeline(
            body,
            grid=(n_assign // SC_WINDOW,),
            in_specs=[pl.BlockSpec((SC_WINDOW, SC_ROW_WORDS), lambda i: (i, 0)),
                      pl.BlockSpec((1, SC_WINDOW), lambda i: (0, i))],
            out_specs=[],
            core_axis_name=("core", "subcore"),
            dimension_semantics=(pltpu.PARALLEL,),
        )(x_hbm, i_hbm)

    return scatter_rows(vals, dest.reshape(1, n_assign))[:, 0]


def _block_diag(w):
    nb = w.shape[0]
    return jnp.einsum('hij,hg->higj', w, jnp.eye(nb, dtype=w.dtype)).reshape(LRU_W, LRU_W)


def kernel(x, c, ada_w, ada_b, pre_mix_g, w_in, conv_w, conv_b, lru_wa, lru_ba, lru_wx, lru_bx, lru_lambda, s5_a_re, s5_a_im, s5_b_re, s5_b_im, s5_c_re, s5_c_im, s5_d, s5_log_dt, s5_glu_w, s5_glu_b, fox_fb, gn_lru, gn_s5, gn_attn, w_out, post_mix_g, pre_ffn_g, router_w, router_b, w_gu, b_gu, w_dn, b_dn, post_ffn_g):
    B, T, _ = x.shape
    N = B * T
    L = ada_w.shape[0]
    tc_s5 = min(TC_S5, T)
    n_blocks = (N * TOP_K) // EXP_ROWS + N_EXPERTS
    n_rows = n_blocks * EXP_ROWS
    mod_all = _modulation(c, ada_w, ada_b).reshape(L, B, N_MOD, D)
    assign_rows = _assignment_rows(N * TOP_K)

    for l in range(L):
        mod = mod_all[l]
        wl = w_in[l]
        w_aug = jnp.pad(wl, ((0, 0), (0, LANES - N_HEADS))).astype(BF16)
        fb = jnp.pad(fox_fb[l], (0, LANES - N_HEADS)).reshape(1, LANES)
        lx, lg, su, qa, ka, va = _in_proj(x, mod, pre_mix_g[l].reshape(1, D), w_aug, fb)

        yl = _lru_branch(lx, lg, conv_w[l], conv_b[l].reshape(1, LRU_W),
                         _block_diag(lru_wa[l]).astype(BF16), lru_ba[l].reshape(1, LRU_W),
                         _block_diag(lru_wx[l]).astype(BF16), lru_bx[l].reshape(1, LRU_W),
                         lru_lambda[l].reshape(1, LRU_W), gn_lru[l].reshape(1, LRU_W))

        tabs = _s5_tables(s5_a_re[l], s5_a_im[l], s5_b_re[l], s5_b_im[l], s5_c_re[l], s5_c_im[l], s5_log_dt[l], tc_s5)
        ys = _s5_branch(su, *tabs, s5_d[l].reshape(1, S5_W), s5_glu_w[l].astype(BF16),
                        s5_glu_b[l].reshape(1, S5_W), gn_s5[l].reshape(1, S5_W), tc_s5)

        ya = _attention(qa, ka, va)

        rw = jnp.pad(router_w[l], ((0, 0), (0, LANES - N_EXPERTS)))
        rwh = rw.astype(BF16)
        rw_split = jnp.concatenate([rwh, (rw - rwh.astype(F32)).astype(BF16)], axis=1)
        rb = jnp.pad(router_b[l], (0, LANES - N_EXPERTS)).reshape(1, LANES)
        x1, h_rows, idx, gates, rank, cnt = _out_proj(
            yl, ys, ya, x, mod, gn_attn[l].reshape(1, ATTN_W), w_out[l].astype(BF16),
            post_mix_g[l].reshape(1, D), pre_ffn_g[l].reshape(1, D), rw_split, rb)

        counts = cnt[0, :N_EXPERTS]
        blocks_e = (counts + EXP_ROWS - 1) // EXP_ROWS
        padded = blocks_e * EXP_ROWS
        block_ends = jnp.cumsum(blocks_e)
        starts = (block_ends - blocks_e) * EXP_ROWS
        n_used = block_ends[-1:]
        bid = jnp.minimum(jnp.arange(n_blocks, dtype=I32), n_used[0] - 1)
        block_e = jnp.minimum(jnp.sum((bid[:, None] >= block_ends[None, :]).astype(I32), axis=1), N_EXPERTS - 1)

        n_assign = N * TOP_K
        dest = (jnp.take(starts.astype(I32), idx) + rank).reshape(n_assign)
        row_assign = _inverse_row_map(dest, assign_rows, n_rows)
        first_block = block_ends - blocks_e
        blk = jnp.arange(n_blocks, dtype=I32)
        valid = jnp.clip(jnp.take(counts, block_e) - (blk - jnp.take(first_block, block_e)) * EXP_ROWS, 0, EXP_ROWS)
        valid = jnp.where(blk < n_used[0], valid, 0)
        is_real = (jnp.arange(EXP_ROWS, dtype=I32)[None, :] < valid[:, None]).reshape(n_rows)
        src_tok = jnp.where(is_real, row_assign // TOP_K, 0)
        spare = n_assign + jnp.arange(n_rows, dtype=I32) % (2 * EXP_ROWS)
        dst_row = jnp.where(is_real, (row_assign % TOP_K) * N + row_assign // TOP_K, spare)

        y_rows = _fused_experts(l, block_e, n_used.astype(I32), src_tok, dst_row, h_rows,
                                w_gu, b_gu, w_dn, b_dn, EXP_ROWS, n_assign)
        x = _dense_combine(y_rows, gates, x1, mod, post_ffn_g[l].reshape(1, D))
    return x
```

```python
import functools

import jax
import jax.numpy as jnp
from jax import lax
from jax.experimental import pallas as pl
from jax.experimental.pallas import tpu as pltpu
from jax.experimental.pallas import tpu_sc as plsc

F32 = jnp.float32
BF16 = jnp.bfloat16
I32 = jnp.int32

D = 1024
LRU_W = 256
LRU_BLOCK_W = 64
CONV_W = 4
LRU_C = 8.0
S5_W = 256
S5_GROUP_W = 16
S5_GROUPS = 16
S5_STATE = 64
S5_STATES = S5_GROUPS * S5_STATE
HEAD_DIM = 64
ATTN_W = 512
N_HEADS = 8
O_LRU_G = 256
O_S5 = 512
O_Q = 768
O_K = O_Q + ATTN_W
O_V = O_K + ATTN_W
O_F = O_V + ATTN_W
N_EXPERTS = 32
TOP_K = 4
D_FF = 1024
SWIGLU_LIMIT = 7.0
SWIGLU_ALPHA = 1.702
N_MOD = 6
RMS_EPS = 1e-6

LANES = 128
SUBLANES = 8
VMEM_LIMIT = 56 * 1024 * 1024
ROW_CHUNKS = D // LANES

C_Q = O_Q
C_K = O_K
C_V = O_V
C_F = O_F
IN_COLS_PAD = C_F + LANES
L_A0, L_A1, L_A2, L_B0, L_B1, L_B2 = 64, 65, 66, 67, 68, 69

TM_IN = 512
SUBTILES = 2
OUT_SUBTILES = 1
TC_LRU = 256
TC_S5 = 256
TQ = 512
ATTN_HEADS_PER_STEP = 4
TM_OUT = 512
TN_COMB = 512
EXP_ROWS = 512
MAP_BLOCK = 1024
SC_WINDOW = 128
SC_ROW_WORDS = 128
NEG_BIG = -1e30


def _rms(x, g):
    return x * lax.rsqrt(jnp.mean(x * x, axis=-1, keepdims=True) + RMS_EPS) * g


def _shift_rows(v, d, row, fill):
    return jnp.where(row >= d, pltpu.roll(v, d, 0), fill)


def _params(vmem=None):
    return pltpu.CompilerParams(vmem_limit_bytes=vmem) if vmem else None


def _mod_kernel(c_ref, w_ref, b_ref, o_ref):
    c = c_ref[...]
    s = c * jax.nn.sigmoid(c)
    o_ref[0] = jnp.dot(s.astype(BF16), w_ref[0].astype(BF16), preferred_element_type=F32) + b_ref[0]


def _modulation(c, ada_w, ada_b):
    L, _, W = ada_w.shape
    B = c.shape[0]
    return pl.pallas_call(
        _mod_kernel,
        grid=(L, W // D),
        in_specs=[pl.BlockSpec((B, D), lambda l, j: (0, 0)),
                  pl.BlockSpec((1, D, D), lambda l, j: (l, 0, j)),
                  pl.BlockSpec((1, 1, D), lambda l, j: (l, 0, j))],
        out_specs=pl.BlockSpec((1, B, D), lambda l, j: (l, 0, j)),
        out_shape=jax.ShapeDtypeStruct((L, B, W), F32),
        name="adaln_mod",
    )(c, ada_w, ada_b.reshape(L, 1, W))


def _in_proj_kernel(x_ref, mod_ref, g_ref, w_ref, fb_ref,
                    lx_ref, lg_ref, su_ref, q_ref, k_ref, v_ref, fcarry_ref, *, tm):
    @pl.when(pl.program_id(1) == 0)
    def _():
        fcarry_ref[...] = jnp.zeros_like(fcarry_ref)

    mod = mod_ref[0]
    ts = tm // SUBTILES
    row = lax.broadcasted_iota(I32, (ts, LANES), 0)
    lane = lax.broadcasted_iota(I32, (ts, LANES), 1)
    carry = fcarry_ref[...]
    for sub in range(SUBTILES):
        rows = slice(sub * ts, (sub + 1) * ts)
        h = _rms(x_ref[0, rows, :], g_ref[...]) * (1.0 + mod[1:2]) + mod[0:1]
        p = jnp.dot(h.astype(BF16), w_ref[...], preferred_element_type=F32)
        lx_ref[0, rows, :] = p[:, 0:O_LRU_G]
        lg_ref[0, rows, :] = p[:, O_LRU_G:O_S5]
        su_ref[0, rows, :] = p[:, O_S5:O_Q]

        z = p[:, C_F:C_F + LANES] + fb_ref[...]
        logf = jnp.minimum(z, 0.0) - jnp.log1p(jnp.exp(-jnp.abs(z)))
        d = 1
        while d < ts:
            logf = logf + _shift_rows(logf, d, row, 0.0)
            d *= 2
        fsum = logf + carry
        carry = fsum[ts - 1:ts, :]

        pq = p[:, C_Q:C_K] * (HEAD_DIM ** -0.5)
        pk = p[:, C_K:C_V]
        pv = p[:, C_V:C_F]
        for hd in range(N_HEADS):
            pair = slice((hd // 2) * LANES, (hd // 2 + 1) * LANES)

            def head_low(a):
                s = a[:, pair]
                return s if hd % 2 == 0 else pltpu.roll(s, HEAD_DIM, 1)

            f = jnp.broadcast_to(fsum[:, hd:hd + 1], (ts, LANES))
            hi = f.astype(BF16).astype(F32)
            r1 = f - hi
            mid = r1.astype(BF16).astype(F32)
            lo = r1 - mid
            q_aug = jnp.where(lane == L_A0, hi, jnp.where(lane == L_A1, mid, jnp.where(lane == L_A2, lo,
                              jnp.where((lane >= L_B0) & (lane <= L_B2), 1.0, 0.0))))
            k_aug = jnp.where(lane == L_B0, -hi, jnp.where(lane == L_B1, -mid, jnp.where(lane == L_B2, -lo,
                              jnp.where((lane >= L_A0) & (lane <= L_A2), 1.0, 0.0))))
            v_aug = jnp.where(lane == HEAD_DIM, 1.0, 0.0)
            q_ref[0, hd, :, rows] = jnp.where(lane < HEAD_DIM, head_low(pq), q_aug).T.astype(BF16)
            k_ref[0, hd, rows, :] = jnp.where(lane < HEAD_DIM, head_low(pk), k_aug).astype(BF16)
            v_ref[0, hd, :, rows] = jnp.where(lane < HEAD_DIM, head_low(pv), v_aug).T.astype(BF16)
    fcarry_ref[...] = carry


def _in_proj(x, mod, g, w_aug, fb):
    B, T, _ = x.shape
    tm = min(TM_IN, T)
    grp = jax.ShapeDtypeStruct((B, T, 256), F32)
    head = jax.ShapeDtypeStruct((B, N_HEADS, T, LANES), BF16)
    grp_spec = pl.BlockSpec((1, tm, 256), lambda b, t: (b, t, 0))
    head_spec = pl.BlockSpec((1, N_HEADS, tm, LANES), lambda b, t: (b, 0, t, 0))
    head_t = jax.ShapeDtypeStruct((B, N_HEADS, LANES, T), BF16)
    head_t_spec = pl.BlockSpec((1, N_HEADS, LANES, tm), lambda b, t: (b, 0, 0, t))
    return pl.pallas_call(
        functools.partial(_in_proj_kernel, tm=tm),
        grid=(B, T // tm),
        in_specs=[pl.BlockSpec((1, tm, D), lambda b, t: (b, t, 0)),
                  pl.BlockSpec((1, N_MOD, D), lambda b, t: (b, 0, 0)),
                  pl.BlockSpec((1, D), lambda b, t: (0, 0)),
                  pl.BlockSpec((D, IN_COLS_PAD), lambda b, t: (0, 0)),
                  pl.BlockSpec((1, LANES), lambda b, t: (0, 0))],
        out_specs=[grp_spec, grp_spec, grp_spec, head_t_spec, head_spec, head_t_spec],
        out_shape=[grp, grp, grp, head_t, head, head_t],
        scratch_shapes=[pltpu.VMEM((1, LANES), F32)],
        compiler_params=_params(VMEM_LIMIT),
        name="in_proj",
    )(x, mod, g, w_aug, fb)


def _lru_kernel(lx_ref, lg_ref, cw_ref, cb_ref, wa_ref, ba_ref, wx_ref, bx_ref, lam_ref, gn_ref,
                y_ref, tail_ref, hcarry_ref, *, tc):
    @pl.when(pl.program_id(1) == 0)
    def _():
        tail_ref[...] = jnp.zeros_like(tail_ref)
        hcarry_ref[...] = jnp.zeros_like(hcarry_ref)

    x = lx_ref[0]
    xcat = jnp.concatenate([tail_ref[...], x], axis=0)
    cw = cw_ref[...]
    xr = cb_ref[...]
    for j in range(CONV_W):
        d = CONV_W - 1 - j
        xs = x if d == 0 else pltpu.roll(xcat, d, 0)[SUBLANES:SUBLANES + tc]
        xr = xr + xs * cw[j:j + 1]
    tail_ref[...] = x[tc - SUBLANES:tc]

    xb = xr.astype(BF16)
    r = jax.nn.sigmoid(jnp.dot(xb, wa_ref[...], preferred_element_type=F32) + ba_ref[...])
    i = jax.nn.sigmoid(jnp.dot(xb, wx_ref[...], preferred_element_type=F32) + bx_ref[...])
    nl = -lam_ref[...]
    softplus = jnp.maximum(nl, 0.0) + jnp.log1p(jnp.exp(-jnp.abs(nl)))
    log_a = -LRU_C * r * softplus
    a = jnp.exp(log_a)
    th = jnp.tanh(log_a)
    u = xr * i * jnp.sqrt(-2.0 * th / (1.0 - th))

    row = lax.broadcasted_iota(I32, (tc, LRU_W), 0)
    d = 1
    while d < tc:
        u = u + a * _shift_rows(u, d, row, 0.0)
        a = a * _shift_rows(a, d, row, 1.0)
        d *= 2
    h = u + a * hcarry_ref[...]
    hcarry_ref[...] = h[tc - 1:tc]

    y = h * jax.nn.gelu(lg_ref[0])
    y_ref[0] = _rms(y, gn_ref[...]).astype(BF16)


def _lru_branch(lx, lg, conv_w, conv_b, wa, ba, wx, bx, lam, gn):
    B, T, _ = lx.shape
    tc = min(TC_LRU, T)
    tile = pl.BlockSpec((1, tc, LRU_W), lambda b, t: (b, t, 0))
    vec = pl.BlockSpec((1, LRU_W), lambda b, t: (0, 0))
    mat = pl.BlockSpec((LRU_W, LRU_W), lambda b, t: (0, 0))
    return pl.pallas_call(
        functools.partial(_lru_kernel, tc=tc),
        grid=(B, T // tc),
        in_specs=[tile, tile, pl.BlockSpec((CONV_W, LRU_W), lambda b, t: (0, 0)), vec, mat, vec, mat, vec, vec, vec],
        out_specs=tile,
        out_shape=jax.ShapeDtypeStruct((B, T, LRU_W), BF16),
        scratch_shapes=[pltpu.VMEM((SUBLANES, LRU_W), F32), pltpu.VMEM((1, LRU_W), F32)],
        name="rg_lru",
    )(lx, lg, conv_w, conv_b, wa, ba, wx, bx, lam, gn)


def _s5_kernel(u_ref, bcat_ref, apr_ref, api_ref, p8r_ref, p8i_ref, bpr_ref, bpi_ref, ccat_ref, d_ref, gw_ref,
               gb_ref, gn_ref, y_ref, cr_ref, ci_ref, xsr_ref, xsi_ref, gr_ref, gi_ref, *, tc):
    @pl.when(pl.program_id(1) == 0)
    def _():
        cr_ref[...] = jnp.zeros_like(cr_ref)
        ci_ref[...] = jnp.zeros_like(ci_ref)

    def cmul_add(xr, xi, ar, ai, sr, si):
        return xr + (ar * sr - ai * si), xi + (ar * si + ai * sr)

    u = u_ref[0]
    bu = jnp.dot(u.astype(BF16), bcat_ref[...], preferred_element_type=F32)
    xr = bu[:, :S5_STATES]
    xi = bu[:, S5_STATES:]
    nb = tc // SUBLANES
    blocks = (nb, SUBLANES, S5_STATES)
    sub = lax.broadcasted_iota(I32, blocks, 1)
    xr = xr.reshape(blocks)
    xi = xi.reshape(blocks)
    for k in range(3):
        d = 1 << k
        ar = apr_ref[k:k + 1, :].reshape(1, 1, S5_STATES)
        ai = api_ref[k:k + 1, :].reshape(1, 1, S5_STATES)
        xr, xi = cmul_add(xr, xi, ar, ai, jnp.where(sub >= d, pltpu.roll(xr, d, 1), 0.0),
                          jnp.where(sub >= d, pltpu.roll(xi, d, 1), 0.0))
    xr = xr.reshape(tc, S5_STATES)
    xi = xi.reshape(tc, S5_STATES)
    lane_tiles = range(S5_STATES // LANES)

    def put(ref, v):
        for l in lane_tiles:
            ref[l] = v[:, l * LANES:(l + 1) * LANES]

    put(xsr_ref, xr)
    put(xsi_ref, xi)
    er =jnp.concatenate([xsr_ref[l, pl.ds(SUBLANES - 1, nb, stride=SUBLANES), :] for l in lane_tiles], axis=1)
    ei = jnp.concatenate([xsi_ref[l, pl.ds(SUBLANES - 1, nb, stride=SUBLANES), :] for l in lane_tiles], axis=1)
    blk = lax.broadcasted_iota(I32, (nb, S5_STATES), 0)
    k, d = 3, 1
    while d < nb:
        er, ei = cmul_add(er, ei, apr_ref[k:k + 1, :], api_ref[k:k + 1, :],
                          _shift_rows(er, d, blk, 0.0), _shift_rows(ei, d, blk, 0.0))
        k, d = k + 1, d * 2
    gr, gi = cmul_add(_shift_rows(er, 1, blk, 0.0), _shift_rows(ei, 1, blk, 0.0),
                      bpr_ref[...], bpi_ref[...], cr_ref[...], ci_ref[...])
    for j in range(SUBLANES):
        for l in lane_tiles:
            gr_ref[l, pl.ds(j, nb, stride=SUBLANES), :] = gr[:, l * LANES:(l + 1) * LANES]
            gi_ref[l, pl.ds(j, nb, stride=SUBLANES), :] = gi[:, l * LANES:(l + 1) * LANES]
    xr, xi = cmul_add(xr, xi, jnp.tile(p8r_ref[...], (nb, 1)), jnp.tile(p8i_ref[...], (nb, 1)),
                      jnp.concatenate([gr_ref[l] for l in lane_tiles], axis=1),
                      jnp.concatenate([gi_ref[l] for l in lane_tiles], axis=1))
    cr_ref[...] = xr[tc - 1:tc]
    ci_ref[...] = xi[tc - 1:tc]

    xcat = jnp.concatenate([xr, xi], axis=1).astype(BF16)
    y = jnp.dot(xcat, ccat_ref[...], preferred_element_type=F32) + d_ref[...] * u
    y = jax.nn.gelu(y)
    y = y * jax.nn.sigmoid(jnp.dot(y.astype(BF16), gw_ref[...], preferred_element_type=F32) + gb_ref[...])
    y_ref[0] = _rms(y, gn_ref[...]).astype(BF16)


def _s5_branch(u, bcat, apow_re, apow_im, p8_re, p8_im, bp_re, bp_im, ccat, d, glu_w, glu_b, gn, tc):
    B, T, _ = u.shape
    tile = pl.BlockSpec((1, tc, S5_W), lambda b, t: (b, t, 0))
    vec = pl.BlockSpec((1, S5_W), lambda b, t: (0, 0))

    def full(a):
        return pl.BlockSpec(a.shape, lambda b, t: (0, 0))

    return pl.pallas_call(
        functools.partial(_s5_kernel, tc=tc),
        grid=(B, T // tc),
        in_specs=[tile, full(bcat), full(apow_re), full(apow_im), full(p8_re), full(p8_im), full(bp_re),
                  full(bp_im), full(ccat), vec, full(glu_w), vec, vec],
        out_specs=tile,
        out_shape=jax.ShapeDtypeStruct((B, T, S5_W), BF16),
        scratch_shapes=[pltpu.VMEM((1, S5_STATES), F32), pltpu.VMEM((1, S5_STATES), F32)]
        + [pltpu.VMEM((S5_STATES // LANES, tc, LANES), F32)] * 4,
        compiler_params=_params(VMEM_LIMIT),
        name="s5",
    )(u, bcat, apow_re, apow_im, p8_re, p8_im, bp_re, bp_im, ccat, d, glu_w, glu_b, gn)


def _s5_tables(a_re, a_im, b_re, b_im, c_re, c_im, log_dt, tc):
    dt = jnp.exp(log_dt)[:, None]
    mag = jnp.exp(a_re * dt)
    abar_re = mag * jnp.cos(a_im * dt)
    abar_im = mag * jnp.sin(a_im * dt)
    den = a_re * a_re + a_im * a_im
    num_re = abar_re - 1.0
    k_re = (num_re * a_re + abar_im * a_im) / den
    k_im = (abar_im * a_re - num_re * a_im) / den
    bbar_re = k_re[..., None] * b_re - k_im[..., None] * b_im
    bbar_im = k_re[..., None] * b_im + k_im[..., None] * b_re
    eye = jnp.eye(S5_GROUPS, dtype=F32)
    bd_re = jnp.einsum('gpc,gh->gchp', bbar_re, eye).reshape(S5_W, S5_STATES)
    bd_im = jnp.einsum('gpc,gh->gchp', bbar_im, eye).reshape(S5_W, S5_STATES)
    bcat = jnp.concatenate([bd_re, bd_im], axis=1).astype(BF16)
    cd_re = jnp.einsum('gcp,gh->gphc', c_re, eye).reshape(S5_STATES, S5_W)
    cd_im = jnp.einsum('gcp,gh->gphc', c_im, eye).reshape(S5_STATES, S5_W)
    ccat = jnp.concatenate([cd_re, -cd_im], axis=0).astype(BF16)
    ar = abar_re.reshape(1, S5_STATES)
    ai = abar_im.reshape(1, S5_STATES)
    pows_r, pows_i = [ar], [ai]
    tab_r, tab_i = ar, ai
    n = 1
    while n < tc:
        sr, si = pows_r[-1], pows_i[-1]
        tab_r, tab_i = (jnp.concatenate([tab_r, tab_r * sr - tab_i * si], axis=0),
                        jnp.concatenate([tab_i, tab_r * si + tab_i * sr], axis=0))
        pows_r.append(sr * sr - si * si)
        pows_i.append(2.0 * sr * si)
        n *= 2
    levels = len(pows_r) - 1
    pad = (-levels) % SUBLANES
    apow_re = jnp.concatenate(pows_r[:levels] + [jnp.zeros((pad, S5_STATES), F32)], axis=0)
    apow_im = jnp.concatenate(pows_i[:levels] + [jnp.zeros((pad, S5_STATES), F32)], axis=0)
    nb = tc // SUBLANES
    bp_re = jnp.concatenate([jnp.ones((1, S5_STATES), F32), tab_r[SUBLANES - 1::SUBLANES][:nb - 1]], axis=0)
    bp_im = jnp.concatenate([jnp.zeros((1, S5_STATES), F32), tab_i[SUBLANES - 1::SUBLANES][:nb - 1]], axis=0)
    return bcat, apow_re, apow_im, tab_r[:SUBLANES], tab_i[:SUBLANES], bp_re, bp_im, ccat


def _attn_kernel(qt_ref, k_ref, vt_ref, o_ref, p_ref, *, tq):
    qi = pl.program_id(2)
    key = lax.broadcasted_iota(I32, (tq, tq), 0)
    qry = lax.broadcasted_iota(I32, (tq, tq), 1)
    heads = range(ATTN_HEADS_PER_STEP)
    qts =[qt_ref[0, hh] for hh in heads]

    def scores(hh, kj):
        k = k_ref[0, hh, pl.ds(pl.multiple_of(kj * tq, tq), tq), :]
        return jnp.dot(k, qts[hh], preferred_element_type=F32)

    def weighted_values(hh, kj, p):
        vt = vt_ref[0, hh, :, pl.ds(pl.multiple_of(kj * tq, tq), tq)]
        return jnp.dot(vt, p, preferred_element_type=F32)

    def softmax_step(s, m):
        m_new = jnp.maximum(m, jnp.max(s, axis=0, keepdims=True))
        return m_new, jnp.exp(m - m_new), jnp.exp(s - m_new).astype(BF16)

    p_ref[...] = jnp.zeros_like(p_ref)

    def body(kj, carry):
        ss = [scores(hh, kj) for hh in heads]
        pvs = [weighted_values(hh, jnp.maximum(kj - 1, 0), p_ref[hh]) for hh in heads]
        out = []
        for hh in heads:
            m, alpha, acc = carry[hh]
            acc = alpha * acc + pvs[hh]
            m, alpha, p = softmax_step(ss[hh], m)
            p_ref[hh] = p
            out.append((m, alpha, acc))
        return tuple(out)

    init = tuple((jnp.full((1, tq), NEG_BIG, F32), jnp.ones((1, tq), F32), jnp.zeros((LANES, tq), F32))
                 for _ in heads)
    carry = lax.fori_loop(0, qi, body, init)
    outs = []
    for hh in heads:
        m, alpha, acc = carry[hh]
        acc = alpha * acc + weighted_values(hh, jnp.maximum(qi - 1, 0), p_ref[hh])
        s = jnp.where(key <= qry, scores(hh, qi), NEG_BIG)
        m, alpha, p = softmax_step(s, m)
        acc = alpha * acc + weighted_values(hh, qi, p)
        out_t = acc / acc[HEAD_DIM:HEAD_DIM + 1, :]
        outs.append(out_t.T[:, :HEAD_DIM])
    o_ref[0] = jnp.concatenate(outs, axis=1)


def _attention(qt, ka, vt):
    B, H, T, _ = ka.shape
    tq = min(TQ, T)
    hs = ATTN_HEADS_PER_STEP
    return pl.pallas_call(
        functools.partial(_attn_kernel, tq=tq),
        grid=(B, H // hs, T // tq),
        in_specs=[pl.BlockSpec((1, hs, LANES, tq), lambda b, hp, qi: (b, hp, 0, qi)),
                  pl.BlockSpec((1, hs, T, LANES), lambda b, hp, qi: (b, hp, 0, 0)),
                  pl.BlockSpec((1, hs, LANES, T), lambda b, hp, qi: (b, hp, 0, 0))],
        out_specs=pl.BlockSpec((1, tq, hs * HEAD_DIM), lambda b, hp, qi: (b, qi, hp)),
        out_shape=jax.ShapeDtypeStruct((B, T, ATTN_W), F32),
        scratch_shapes=[pltpu.VMEM((hs, tq, tq), BF16)],
        compiler_params=_params(VMEM_LIMIT),
        name="fox_attention",
    )(qt, ka, vt)


def _out_proj_kernel(yl_ref, ys_ref, ya_ref, x_ref, mod_ref, gna_ref, wout_ref, pmg_ref, pfg_ref,
                     rw_ref, rb_ref,
                     x1_ref, h2_ref, idx_ref, gate_ref, rank_ref, cnt_ref, carry_ref, *, tm):
    @pl.when((pl.program_id(0) == 0) & (pl.program_id(1) == 0))
    def _():
        carry_ref[...] = jnp.zeros_like(carry_ref)

    mod = mod_ref[0]
    ts = tm // OUT_SUBTILES
    lane = lax.broadcasted_iota(I32, (ts, LANES), 1)
    lane_f = lane.astype(F32)
    r_i = lax.broadcasted_iota(I32, (ts, ts), 0)
    c_i = lax.broadcasted_iota(I32, (ts, ts), 1)
    below = jnp.where(c_i < r_i, 1.0, 0.0).astype(BF16)
    carry = carry_ref[...]
    for sub in range(OUT_SUBTILES):
        rows = slice(sub * ts, (sub + 1) * ts)
        ya = _rms(ya_ref[0, rows, :], gna_ref[...])
        ycat = jnp.concatenate([yl_ref[0, rows, :], ys_ref[0, rows, :], ya.astype(BF16)], axis=1)
        y = jnp.dot(ycat, wout_ref[...], preferred_element_type=F32)
        x1 = x_ref[0, rows, :] + mod[2:3] * _rms(y, pmg_ref[...])
        x1_ref[0, rows, :] = x1
        h2 = _rms(x1, pfg_ref[...]) * (1.0 + mod[4:5]) + mod[3:4]
        for c in range(ROW_CHUNKS):
            h2_ref[pl.ds(sub * ts * ROW_CHUNKS + c, ts, stride=ROW_CHUNKS), :] = h2[:, c * LANES:(c + 1) * LANES]

        hh = h2.astype(BF16)
        hl = (h2 - hh.astype(F32)).astype(BF16)
        both = jnp.dot(hh, rw_ref[...], preferred_element_type=F32)
        logits = (both[:, :LANES] + jnp.dot(hl, rw_ref[:, :LANES], preferred_element_type=F32)
                  + both[:, LANES:] + rb_ref[...])
        work = jnp.where(lane < N_EXPERTS, logits, NEG_BIG)
        vals, idxs = [], []
        for _ in range(TOP_K):
            mx = jnp.max(work, axis=1, keepdims=True)
            ik = jnp.min(jnp.where(work == mx, lane_f, float(LANES)), axis=1, keepdims=True)
            vals.append(mx)
            idxs.append(ik)
            work = jnp.where(lane_f == ik, 2.0 * NEG_BIG, work)
        es = [jnp.exp(v - vals[0]) for v in vals]
        den = es[0] + es[1] + es[2] + es[3]

        onehot = jnp.zeros((ts, LANES), F32)
        for ik in idxs:
            onehot = onehot + jnp.where(lane_f == ik, 1.0, 0.0)
        prior = jnp.dot(below, onehot.astype(BF16), preferred_element_type=F32) + carry
        carry = carry + jnp.sum(onehot, axis=0, keepdims=True)

        idx_full = jnp.zeros((ts, LANES), F32)
        gate_full = jnp.zeros((ts, LANES), F32)
        rank_full = jnp.zeros((ts, LANES), F32)
        for k in range(TOP_K):
            rk = jnp.sum(jnp.where(lane_f == idxs[k], prior, 0.0), axis=1, keepdims=True)
            idx_full = jnp.where(lane == k, idxs[k], idx_full)
            gate_full = jnp.where(lane == k, es[k] / den, gate_full)
            rank_full = jnp.where(lane == k, rk, rank_full)
        idx_ref[rows, :] = idx_full[:, :TOP_K].astype(I32)
        gate_ref[rows, :] = gate_full[:, :TOP_K]
        rank_ref[rows, :] = rank_full[:, :TOP_K].astype(I32)
    carry_ref[...] = carry
    cnt_ref[...] = carry.astype(I32)


def _out_proj(yl, ys, ya, x, mod, gna, wout, pmg, pfg, rw_split, rb):
    B, T, _ = x.shape
    tm = min(TM_OUT, T)
    nt = T // tm
    N = B * T

    def vec(n):
        return pl.BlockSpec((1, n), lambda b, t: (0, 0))

    tok = lambda b, t: (b * nt + t, 0)
    return pl.pallas_call(
        functools.partial(_out_proj_kernel, tm=tm),
        grid=(B, nt),
        in_specs=[pl.BlockSpec((1, tm, LRU_W), lambda b, t: (b, t, 0)),
                  pl.BlockSpec((1, tm, S5_W), lambda b, t: (b, t, 0)),
                  pl.BlockSpec((1, tm, ATTN_W), lambda b, t: (b, t, 0)),
                  pl.BlockSpec((1, tm, D), lambda b, t: (b, t, 0)),
                  pl.BlockSpec((1, N_MOD, D), lambda b, t: (b, 0, 0)),
                  vec(ATTN_W),
                  pl.BlockSpec((D, D), lambda b, t: (0, 0)),
                  vec(D), vec(D),
                  pl.BlockSpec((D, 2 * LANES), lambda b, t: (0, 0)),
                  vec(LANES)],
        out_specs=[pl.BlockSpec((1, tm, D), lambda b, t: (b, t, 0)),
                   pl.BlockSpec((tm * ROW_CHUNKS, LANES), tok),
                   pl.BlockSpec((tm, TOP_K), tok),
                   pl.BlockSpec((tm, TOP_K), tok),
                   pl.BlockSpec((tm, TOP_K), tok),
                   pl.BlockSpec((1, LANES), lambda b, t: (0, 0))],
        out_shape=[jax.ShapeDtypeStruct((B, T, D), F32),
                   jax.ShapeDtypeStruct((N * ROW_CHUNKS, LANES), F32),
                   jax.ShapeDtypeStruct((N, TOP_K), I32),
                   jax.ShapeDtypeStruct((N, TOP_K), F32),
                   jax.ShapeDtypeStruct((N, TOP_K), I32),
                   jax.ShapeDtypeStruct((1, LANES), I32)],
        scratch_shapes=[pltpu.VMEM((1, LANES), F32)],
        compiler_params=_params(VMEM_LIMIT),
        name="out_proj_router",
    )(yl, ys, ya, x, mod, gna, wout, pmg, pfg, rw_split, rb)


def _fused_expert_kernel(be_ref, nused_ref, src0_ref, src_ref, dst_ref, h_ref, wgu_ref, bgu_ref, wdn_ref, bdn_ref,
                         y_ref, xbuf, ybuf, wgu_s, wdn_s, gsem, ssem, *, rows, n_blocks, n_assign):
    b = pl.program_id(0)
    nu = nused_ref[0]
    slot = lax.rem(b, 2)
    other = 1 - slot
    xslot = lax.rem(b, 3)
    xnext = lax.rem(b + 2, 3)
    src_off = lax.rem(jnp.minimum(b + 2, n_blocks - 1) * rows, MAP_BLOCK)
    dst_off = lax.rem(jnp.maximum(b - 1, 0) * rows, MAP_BLOCK)

    def gather(i, tok, buf_slot):
        return pltpu.make_async_copy(
            h_ref.at[pl.ds(pl.multiple_of(tok * ROW_CHUNKS, ROW_CHUNKS), ROW_CHUNKS), :],
            xbuf.at[buf_slot, pl.ds(i * ROW_CHUNKS, ROW_CHUNKS), :], gsem.at[buf_slot])

    def scatter(i, dst, buf_slot):
        return pltpu.make_async_copy(
            ybuf.at[buf_slot, pl.ds(i * ROW_CHUNKS, ROW_CHUNKS), :],
            y_ref.at[pl.ds(pl.multiple_of(dst * ROW_CHUNKS, ROW_CHUNKS), ROW_CHUNKS), :], ssem.at[buf_slot])

    @pl.when(b == 0)
    def _():
        ybuf[...] = jnp.zeros(ybuf.shape, F32)
        for i in range(rows):
            gather(i, src0_ref[i], 0).start()
            gather(i, src0_ref[rows + i], 1).start()
            scatter(i, n_assign + 2 * rows + i, 0).start()

    @pl.when(b <= nu)
    def _():
        for i in range(rows):
            gather(i, 0, xslot).wait()

    prev = be_ref[jnp.maximum(b - 1, 0)]

    @pl.when((b == 0) | (be_ref[b] != prev))
    def _():
        wgu_s[...] = wgu_ref[0, 0].astype(BF16)
        wdn_s[...] = wdn_ref[0, 0].astype(BF16)

    @pl.when(b < nu)
    def _():
        for i in range(rows):
            gather(i, src_ref[src_off + i], xnext).start(priority=i % 2)
        for i in range(rows):
            dst = jnp.where(b == 0, n_assign + 3 * rows + i, dst_ref[dst_off + i])
            scatter(i, dst, other).start(priority=i % 2)
        x = jnp.concatenate([xbuf[xslot, pl.ds(c, rows, stride=ROW_CHUNKS), :] for c in range(ROW_CHUNKS)], axis=1)
        gu = jnp.dot(x.astype(BF16), wgu_s[...], preferred_element_type=F32) + bgu_ref[0, 0]
        g = jnp.minimum(gu[:, :D_FF], SWIGLU_LIMIT)
        up = jnp.clip(gu[:, D_FF:], -SWIGLU_LIMIT, SWIGLU_LIMIT)
        act = (up + 1.0) * (g * jax.nn.sigmoid(SWIGLU_ALPHA * g))
        y = jnp.dot(act.astype(BF16), wdn_s[...], preferred_element_type=F32) + bdn_ref[0, 0]
        for i in range(rows):
            scatter(i, 0, slot).wait()
        for c in range(ROW_CHUNKS):
            ybuf[slot, pl.ds(c, rows, stride=ROW_CHUNKS), :] = y[:, c * LANES:(c + 1) * LANES]

    @pl.when(b == nu)
    def _():
        for i in range(rows):
            gather(i, 0, lax.rem(b + 1, 3)).wait()
        for i in range(rows):
            scatter(i, 0, slot).wait()
        for i in range(rows):
            scatter(i, dst_ref[dst_off + i], other).start()
        for i in range(rows):
            scatter(i, 0, other).wait()


def _fused_experts(layer, block_e, n_used, src_tok, dst_row, h_rows, w_gu, b_gu, w_dn, b_dn, rows, n_assign):
    n_blocks = block_e.shape[0]
    L, E = w_gu.shape[:2]
    wsel = lambda b, be, nu: (layer, be[b], 0, 0)
    grid_spec = pltpu.PrefetchScalarGridSpec(
        num_scalar_prefetch=2,
        grid=(n_blocks,),
        in_specs=[pl.BlockSpec((MAP_BLOCK,), lambda b, be, nu: (0,), memory_space=pltpu.SMEM),
                  pl.BlockSpec((MAP_BLOCK,), lambda b, be, nu: (jnp.minimum(b + 2, n_blocks - 1) * rows // MAP_BLOCK,),
                               memory_space=pltpu.SMEM),
                  pl.BlockSpec((MAP_BLOCK,), lambda b, be, nu: (jnp.maximum(b - 1, 0) * rows // MAP_BLOCK,),
                               memory_space=pltpu.SMEM),
                  pl.BlockSpec(memory_space=pl.ANY),
                  pl.BlockSpec((1, 1, D, 2 * D_FF), wsel),
                  pl.BlockSpec((1, 1, 1, 2 * D_FF), wsel),
                  pl.BlockSpec((1, 1, D_FF, D), wsel),
                  pl.BlockSpec((1, 1, 1, D), wsel)],
        out_specs=pl.BlockSpec(memory_space=pl.ANY),
        scratch_shapes=[pltpu.VMEM((3, rows * ROW_CHUNKS, LANES), F32),
                        pltpu.VMEM((2, rows * ROW_CHUNKS, LANES), F32),
                        pltpu.VMEM((D, 2 * D_FF), BF16), pltpu.VMEM((D_FF, D), BF16),
                        pltpu.SemaphoreType.DMA((3,)), pltpu.SemaphoreType.DMA((2,))],
    )
    return pl.pallas_call(
        functools.partial(_fused_expert_kernel, rows=rows, n_blocks=n_blocks, n_assign=n_assign),
        grid_spec=grid_spec,
        out_shape=jax.ShapeDtypeStruct(((n_assign + 4 * rows) * ROW_CHUNKS, LANES), F32),
        compiler_params=_params(VMEM_LIMIT),
        name="moe_experts",
    )(block_e, n_used, src_tok, src_tok, dst_row, h_rows, w_gu, b_gu.reshape(L, E, 1, 2 * D_FF), w_dn,
      b_dn.reshape(L, E, 1, D))


def _dense_combine_kernel(y0_ref, y1_ref, y2_ref, y3_ref, gate_ref, x1_ref, mod_ref, g_ref, o_ref, *, tn):
    gates = gate_ref[...]
    y = jnp.zeros((tn, D), F32)
    for k, yk_ref in enumerate((y0_ref, y1_ref, y2_ref, y3_ref)):
        yk = jnp.concatenate([yk_ref[pl.ds(c, tn, stride=ROW_CHUNKS), :] for c in range(ROW_CHUNKS)], axis=1)
        y = y + gates[:, k:k + 1] * yk
    mod = mod_ref[0]
    o_ref[0] = x1_ref[0] + mod[5:6] * _rms(y, g_ref[...])


def _dense_combine(y_rows, gates, x1, mod, g):
    B, T, _ = x1.shape
    tn = min(TN_COMB, T)
    nt = T // tn

    def slot_spec(k):
        return pl.BlockSpec((tn * ROW_CHUNKS, LANES), lambda b, t: (k * B * nt + b * nt + t, 0))

    return pl.pallas_call(
        functools.partial(_dense_combine_kernel, tn=tn),
        grid=(B, nt),
        in_specs=[slot_spec(0), slot_spec(1), slot_spec(2), slot_spec(3),
                  pl.BlockSpec((tn, TOP_K), lambda b, t: (b * nt + t, 0)),
                  pl.BlockSpec((1, tn, D), lambda b, t: (b, t, 0)),
                  pl.BlockSpec((1, N_MOD, D), lambda b, t: (b, 0, 0)),
                  pl.BlockSpec((1, D), lambda b, t: (0, 0))],
        out_specs=pl.BlockSpec((1, tn, D), lambda b, t: (b, t, 0)),
        out_shape=jax.ShapeDtypeStruct((B, T, D), F32),
        compiler_params=_params(VMEM_LIMIT),
        name="moe_combine",
    )(y_rows, y_rows, y_rows, y_rows, gates, x1, mod, g)


def _assignment_rows(n_assign):
    return jnp.broadcast_to(jnp.arange(n_assign, dtype=I32)[:, None], (n_assign, SC_ROW_WORDS))


def _inverse_row_map(dest, vals, n_rows):
    n_assign = dest.shape[0]
    mesh = plsc.VectorSubcoreMesh(core_axis_name="core", subcore_axis_name="subcore")

    @functools.partial(pl.kernel, out_type=jax.ShapeDtypeStruct((n_rows, SC_ROW_WORDS), I32), mesh=mesh,
                       scratch_types=[])
    def scatter_rows(x_hbm, i_hbm, o_hbm):
        def body(x_vmem, i_vmem):
            pltpu.sync_copy(x_vmem, o_hbm.at[i_vmem.at[0]])

        pltpu.emit_pipeline(
            body,
            grid=(n_assign // SC_WINDOW,),
            in_specs=[pl.BlockSpec((SC_WINDOW, SC_ROW_WORDS), lambda i: (i, 0)),
                      pl.BlockSpec((1, SC_WINDOW), lambda i: (0, i))],
            out_specs=[],
            core_axis_name=("core", "subcore"),
            dimension_semantics=(pltpu.PARALLEL,),
        )(x_hbm, i_hbm)

    return scatter_rows(vals, dest.reshape(1, n_assign))[:, 0]


def _block_diag(w):
    nb = w.shape[0]
    return jnp.einsum('hij,hg->higj', w, jnp.eye(nb, dtype=w.dtype)).reshape(LRU_W, LRU_W)


def kernel(x, c, ada_w, ada_b, pre_mix_g, w_in, conv_w, conv_b, lru_wa, lru_ba, lru_wx, lru_bx, lru_lambda, s5_a_re, s5_a_im, s5_b_re, s5_b_im, s5_c_re, s5_c_im, s5_d, s5_log_dt, s5_glu_w, s5_glu_b, fox_fb, gn_lru, gn_s5, gn_attn, w_out, post_mix_g, pre_ffn_g, router_w, router_b, w_gu, b_gu, w_dn, b_dn, post_ffn_g):
    B, T, _ = x.shape
    N = B * T
    L = ada_w.shape[0]
    tc_s5 = min(TC_S5, T)
    n_blocks = (N * TOP_K) // EXP_ROWS + N_EXPERTS
    n_rows = n_blocks * EXP_ROWS
    mod_all = _modulation(c, ada_w, ada_b).reshape(L, B, N_MOD, D)
    assign_rows = _assignment_rows(N * TOP_K)

    for l in range(L):
        mod = mod_all[l]
        wl = w_in[l]
        w_aug = jnp.pad(wl, ((0, 0), (0, LANES - N_HEADS))).astype(BF16)
        fb = jnp.pad(fox_fb[l], (0, LANES - N_HEADS)).reshape(1, LANES)
        lx, lg, su, qa, ka, va = _in_proj(x, mod, pre_mix_g[l].reshape(1, D), w_aug, fb)

        yl = _lru_branch(lx, lg, conv_w[l], conv_b[l].reshape(1, LRU_W),
                         _block_diag(lru_wa[l]).astype(BF16), lru_ba[l].reshape(1, LRU_W),
                         _block_diag(lru_wx[l]).astype(BF16), lru_bx[l].reshape(1, LRU_W),
                         lru_lambda[l].reshape(1, LRU_W), gn_lru[l].reshape(1, LRU_W))

        tabs = _s5_tables(s5_a_re[l], s5_a_im[l], s5_b_re[l], s5_b_im[l], s5_c_re[l], s5_c_im[l], s5_log_dt[l], tc_s5)
        ys = _s5_branch(su, *tabs, s5_d[l].reshape(1, S5_W), s5_glu_w[l].astype(BF16),
                        s5_glu_b[l].reshape(1, S5_W), gn_s5[l].reshape(1, S5_W), tc_s5)

        ya = _attention(qa, ka, va)

        rw = jnp.pad(router_w[l], ((0, 0), (0, LANES - N_EXPERTS)))
        rwh = rw.astype(BF16)
        rw_split = jnp.concatenate([rwh, (rw - rwh.astype(F32)).astype(BF16)], axis=1)
        rb = jnp.pad(router_b[l], (0, LANES - N_EXPERTS)).reshape(1, LANES)
        x1, h_rows, idx, gates, rank, cnt = _out_proj(
            yl, ys, ya, x, mod, gn_attn[l].reshape(1, ATTN_W), w_out[l].astype(BF16),
            post_mix_g[l].reshape(1, D), pre_ffn_g[l].reshape(1, D), rw_split, rb)

        counts = cnt[0, :N_EXPERTS]
        blocks_e = (counts + EXP_ROWS - 1) // EXP_ROWS
        padded = blocks_e * EXP_ROWS
        block_ends = jnp.cumsum(blocks_e)
        starts = (block_ends - blocks_e) * EXP_ROWS
        n_used = block_ends[-1:]
        bid = jnp.minimum(jnp.arange(n_blocks, dtype=I32), n_used[0] - 1)
        block_e = jnp.minimum(jnp.sum((bid[:, None] >= block_ends[None, :]).astype(I32), axis=1), N_EXPERTS - 1)

        n_assign = N * TOP_K
        dest = (jnp.take(starts.astype(I32), idx) + rank).reshape(n_assign)
        row_assign = _inverse_row_map(dest, assign_rows, n_rows)
        first_block = block_ends - blocks_e
        blk = jnp.arange(n_blocks, dtype=I32)
        valid = jnp.clip(jnp.take(counts, block_e) - (blk - jnp.take(first_block, block_e)) * EXP_ROWS, 0, EXP_ROWS)
        valid = jnp.where(blk < n_used[0], valid, 0)
        is_real = (jnp.arange(EXP_ROWS, dtype=I32)[None, :] < valid[:, None]).reshape(n_rows)
        src_tok = jnp.where(is_real, row_assign // TOP_K, 0)
        spare = n_assign + jnp.arange(n_rows, dtype=I32) % (2 * EXP_ROWS)
        dst_row = jnp.where(is_real, (row_assign % TOP_K) * N + row_assign // TOP_K, spare)

        y_rows = _fused_experts(l, block_e, n_used.astype(I32), src_tok, dst_row, h_rows,
                                w_gu, b_gu, w_dn, b_dn, EXP_ROWS, n_assign)
        x = _dense_combine(y_rows, gates, x1, mod, post_ffn_g[l].reshape(1, D))
    return x
```
